```python
import jax, jax.numpy as jnp
from jax import lax
import numpy as np

D_MODEL = 1024
BATCH = 1
SEQ = 16384
DEPTH = 4

N_MEM = 256
N_EVEN = (DEPTH + 1) // 2
N_ODD = DEPTH // 2
MIX_WIDTH = D_MODEL
EPS = 1e-6

MLA_HEADS = 8
MLA_NOPE = 64
MLA_ROPE = 32
MLA_V = 64
MLA_Q_RANK = 256
MLA_KV_RANK = 128
ROPE_THETA = 10000.0
Q_BLOCK = 128

GLA_HEADS = 4
GLA_DK = 64
GLA_DV = 128
GLA_GATE_RANK = 16
GLA_TAU = 16.0
GLA_CHUNK = 64

MLSTM_HEADS = 4
MLSTM_DH = MIX_WIDTH // MLSTM_HEADS
MLSTM_CONV = 4
MLSTM_CHUNK = 64

XATTN_HEADS = 4
XATTN_DH = D_MODEL // XATTN_HEADS

D_FF = 4 * D_MODEL

EVEN_SPLITS = (MLA_Q_RANK, MLA_KV_RANK, MLA_ROPE, GLA_HEADS * GLA_DK, GLA_HEADS * GLA_DK,
               GLA_HEADS * GLA_DV, GLA_GATE_RANK, GLA_HEADS * GLA_DV)
EVEN_IN = sum(EVEN_SPLITS)
EVEN_OUT = MLA_HEADS * MLA_V + GLA_HEADS * GLA_DV

kernel_name = "hybrid_mla_gla_mlstm_sandwich_trunk"


def _rmsnorm(x, g):
    xf = x.astype(jnp.float32)
    y = xf * lax.rsqrt(jnp.mean(xf * xf, axis=-1, keepdims=True) + EPS)
    return (y * g.astype(jnp.float32)).astype(x.dtype)


def _layernorm(x, g):
    xf = x.astype(jnp.float32)
    mu = jnp.mean(xf, axis=-1, keepdims=True)
    xc = xf - mu
    y = xc * lax.rsqrt(jnp.mean(xc * xc, axis=-1, keepdims=True) + EPS)
    return (y * g.astype(jnp.float32)).astype(x.dtype)


def _rope_tables(positions, dim):
    inv_freq = ROPE_THETA ** (-jnp.arange(0, dim, 2, dtype=jnp.float32) / dim)
    ang = positions.astype(jnp.float32)[..., None] * inv_freq
    return jnp.cos(ang), jnp.sin(ang)


def _apply_rope(x, cos, sin):
    x1, x2 = jnp.split(x.astype(jnp.float32), 2, axis=-1)
    return jnp.concatenate([x1 * cos - x2 * sin, x2 * cos + x1 * sin], axis=-1).astype(x.dtype)


def _to_chunks(t, L):
    B, S, H, d = t.shape
    return t.reshape(B, S // L, L, H, d).transpose(1, 0, 3, 2, 4)


def _from_chunks(t):
    nc, B, H, L, d = t.shape
    return t.transpose(1, 0, 3, 2, 4).reshape(B, nc * L, H, d)


def _mla(c_q, c_kv, k_pe, positions, g_q, w_uq, g_kv, w_ukv):
    B, S, _ = c_q.shape
    H = MLA_HEADS
    q = (_rmsnorm(c_q, g_q) @ w_uq).reshape(B, S, H, MLA_NOPE + MLA_ROPE)
    q_nope, q_pe = q[..., :MLA_NOPE], q[..., MLA_NOPE:]
    kv = (_rmsnorm(c_kv, g_kv) @ w_ukv).reshape(B, S, H, MLA_NOPE + MLA_V)
    k_nope, v = kv[..., :MLA_NOPE], kv[..., MLA_NOPE:]
    cos, sin = _rope_tables(positions, MLA_ROPE)
    q_pe = _apply_rope(q_pe, cos[:, :, None, :], sin[:, :, None, :])
    k_pe = _apply_rope(k_pe, cos, sin)
    scale = (MLA_NOPE + MLA_ROPE) ** -0.5
    nb = S // Q_BLOCK
    key_idx = jnp.arange(S)

    def to_blocks(t):
        return jnp.moveaxis(t.reshape(B, nb, Q_BLOCK, *t.shape[2:]), 1, 0)

    def attend(blk):
        qn, qp, bi = blk
        s = (jnp.einsum('bqhd,bkhd->bhqk', qn, k_nope)
             + jnp.einsum('bqhr,bkr->bhqk', qp, k_pe)).astype(jnp.float32) * scale
        q_idx = bi * Q_BLOCK + jnp.arange(Q_BLOCK)
        s = jnp.where(key_idx[None, :] <= q_idx[:, None], s, -jnp.inf)
        p = jax.nn.softmax(s, axis=-1).astype(v.dtype)
        return jnp.einsum('bhqk,bkhd->bqhd', p, v)

    o = lax.map(attend, (to_blocks(q_nope), to_blocks(q_pe), jnp.arange(nb)))
    return jnp.moveaxis(o, 0, 1).reshape(B, S, H * MLA_V)


def _gla(q, k, v, g_lr, r, w_gate, b_gate, g_norm):
    B, S, _ = q.shape
    H, DK, DV, L = GLA_HEADS, GLA_DK, GLA_DV, GLA_CHUNK
    f32 = jnp.float32
    log_a = jax.nn.log_sigmoid((g_lr @ w_gate + b_gate).astype(f32)) / GLA_TAU
    qc = _to_chunks(q.astype(f32).reshape(B, S, H, DK) * DK ** -0.5, L)
    kc = _to_chunks(k.astype(f32).reshape(B, S, H, DK), L)
    vc = _to_chunks(v.astype(f32).reshape(B, S, H, DV), L)
    gc = _to_chunks(log_a.reshape(B, S, H, DK), L)
    causal = jnp.tril(jnp.ones((L, L), dtype=bool))

    def step(state, inp):
        qi, ki, vi, gi = inp
        b = jnp.cumsum(gi, axis=2)
        b_end = b[:, :, -1:, :]
        diff = b[:, :, :, None, :] - b[:, :, None, :, :]
        decay = jnp.exp(jnp.where(causal[:, :, None], diff, -jnp.inf))
        attn = jnp.einsum('bhtd,bhsd,bhtsd->bhts', qi, ki, decay)
        o = attn @ vi + (qi * jnp.exp(b)) @ state
        state = (jnp.exp(b_end).swapaxes(-1, -2) * state
                 + (ki * jnp.exp(b_end - b)).swapaxes(-1, -2) @ vi)
        return state, o

    _, o = lax.scan(step, jnp.zeros((B, H, DK, DV), f32), (qc, kc, vc, gc))
    o = _rmsnorm(_from_chunks(o), g_norm).reshape(B, S, H * DV)
    return (o * jax.nn.silu(r.astype(f32))).astype(q.dtype)


def _mlstm_cell(q, k, v, i_pre, f_pre):
    B, S, H, DH = q.shape
    L = MLSTM_CHUNK
    f32 = jnp.float32
    qc = _to_chunks(q.astype(f32) * DH ** -0.5, L)
    kc = _to_chunks(k.astype(f32), L)
    vc = _to_chunks(v.astype(f32), L)
    ic = _to_chunks(i_pre.astype(f32)[..., None], L)[..., 0]
    fc = _to_chunks(jax.nn.log_sigmoid(f_pre.astype(f32))[..., None], L)[..., 0]
    causal = jnp.tril(jnp.ones((L, L), dtype=bool))

    def step(carry, inp):
        C, n, m = carry
        qi, ki, vi, ii, lfi = inp
        b = jnp.cumsum(lfi, axis=-1)
        log_d = jnp.where(causal, b[..., :, None] - b[..., None, :] + ii[..., None, :], -jnp.inf)
        m_inter = b + m[..., None]
        m_t = jnp.maximum(m_inter, jnp.max(log_d, axis=-1))
        w_intra = jnp.exp(log_d - m_t[..., None]) * jnp.einsum('bhtd,bhsd->bhts', qi, ki)
        w_inter = jnp.exp(m_inter - m_t)
        num = w_intra @ vi + w_inter[..., None] * (qi @ C)
        den = jnp.sum(w_intra, axis=-1) + w_inter * jnp.einsum('bhtd,bhd->bht', qi, n)
        h = num / jnp.maximum(jnp.abs(den), jnp.exp(-m_t))[..., None]
        b_end = b[..., -1]
        log_w = b_end[..., None] - b + ii
        m_new = jnp.maximum(b_end + m, jnp.max(log_w, axis=-1))
        w_s = jnp.exp(log_w - m_new[..., None])
        carry_decay = jnp.exp(b_end + m - m_new)
        C = carry_decay[..., None, None] * C + jnp.einsum('bhs,bhsd,bhse->bhde', w_s, ki, vi)
        n = carry_decay[..., None] * n + jnp.einsum('bhs,bhsd->bhd', w_s, ki)
        return (C, n, m_new), h

    init = (jnp.zeros((B, H, DH, DH), f32), jnp.zeros((B, H, DH), f32), jnp.zeros((B, H), f32))
    _, h = lax.scan(step, init, (qc, kc, vc, ic, fc))
    return _from_chunks(h).astype(q.dtype)


def _causal_conv(x, w, b):
    C = x.shape[-1]
    y = lax.conv_general_dilated(x, w[:, None, :].astype(x.dtype), window_strides=(1,),
                                 padding=[(MLSTM_CONV - 1, 0)],
                                 dimension_numbers=('NWC', 'WIO', 'NWC'),
                                 feature_group_count=C)
    return y + b


def _even_mixer(hn, positions, w_in, g_q, w_uq, g_kv, w_ukv, w_gate, b_gate, g_gla, w_out):
    proj = hn @ w_in
    offs = np.cumsum(EVEN_SPLITS)[:-1].tolist()
    c_q, c_kv, k_pe, gq, gk, gv, g_lr, r = jnp.split(proj, offs, axis=-1)
    a = _mla(c_q, c_kv, k_pe, positions, g_q, w_uq, g_kv, w_ukv)
    g = _gla(gq, gk, gv, g_lr, r, w_gate, b_gate, g_gla)
    return jnp.concatenate([a, g], axis=-1) @ w_out


def _odd_mixer(hn, w_in, conv_w, conv_b, w_q, w_k, w_v, w_gates, b_gates, g_hnorm, skip, w_out):
    B, S, _ = hn.shape
    H, DH = MLSTM_HEADS, MLSTM_DH
    x_m, z = jnp.split(hn @ w_in, 2, axis=-1)
    x_c = jax.nn.silu(_causal_conv(x_m, conv_w, conv_b))
    xch = x_c.reshape(B, S, H, DH)
    xmh = x_m.reshape(B, S, H, DH)
    q = jnp.einsum('bshd,hde->bshe', xch, w_q)
    k = jnp.einsum('bshd,hde->bshe', xch, w_k)
    v = jnp.einsum('bshd,hde->bshe', xmh, w_v)
    gates = jnp.concatenate([q, k, v], axis=-1).reshape(B, S, 3 * MIX_WIDTH) @ w_gates + b_gates
    i_pre, f_pre = gates[..., :H], gates[..., H:]
    h = _mlstm_cell(q, k, v, i_pre, f_pre)
    h = _layernorm(h, g_hnorm.reshape(H, DH)).reshape(B, S, MIX_WIDTH)
    out = (h + skip * x_c) * jax.nn.silu(z)
    return out @ w_out


def _mem_xattn(hn, mem_n, w_q, w_k, w_v, w_o):
    B, S, _ = hn.shape
    M = mem_n.shape[1]
    q = (hn @ w_q).reshape(B, S, XATTN_HEADS, XATTN_DH)
    k = (mem_n @ w_k).reshape(B, M, XATTN_HEADS, XATTN_DH)
    v = (mem_n @ w_v).reshape(B, M, XATTN_HEADS, XATTN_DH)
    s = jnp.einsum('bshd,bmhd->bhsm', q, k).astype(jnp.float32) * XATTN_DH ** -0.5
    p = jax.nn.softmax(s, axis=-1).astype(v.dtype)
    return jnp.einsum('bhsm,bmhd->bshd', p, v).reshape(B, S, D_MODEL) @ w_o


def _sq_relu_mlp(hn, w1, w2):
    return jnp.square(jax.nn.relu(hn @ w1)) @ w2


def setup_inputs(seed: int = 0) -> dict:
    key = jax.random.key(seed)
    keys = jax.random.split(key, 64)
    ctr = [0]

    def nk():
        ctr[0] += 1
        return keys[ctr[0] - 1]

    def w(shape, fan_in):
        return jax.random.normal(nk(), shape, jnp.float32) * fan_in ** -0.5

    def gain(shape):
        return 1.0 + 0.05 * jax.random.normal(nk(), shape, jnp.float32)

    def small(shape, s=0.02):
        return s * jax.random.normal(nk(), shape, jnp.float32)

    x = jax.random.normal(nk(), (BATCH, SEQ, D_MODEL), jnp.float32)
    mem = jax.random.normal(nk(), (BATCH, N_MEM, D_MODEL), jnp.float32)
    offset = jax.random.randint(nk(), (BATCH, 1), 0, 4096, dtype=jnp.int32)
    positions = (offset + jnp.arange(SEQ, dtype=jnp.int32)[None, :]).astype(jnp.int32)

    f_bias = jnp.broadcast_to(jnp.linspace(3.0, 6.0, MLSTM_HEADS, dtype=jnp.float32), (N_ODD, MLSTM_HEADS))
    od_b_gates = jnp.concatenate([small((N_ODD, MLSTM_HEADS), 0.1),
                                  f_bias + small((N_ODD, MLSTM_HEADS), 0.1)], axis=-1)
    return {
        "x": x, "mem": mem, "positions": positions,
        "g_mix_pre": gain((DEPTH, D_MODEL)), "g_mix_post": gain((DEPTH, D_MODEL)),
        "g_xattn_pre": gain((DEPTH, D_MODEL)), "g_xattn_post": gain((DEPTH, D_MODEL)),
        "g_mem": gain((DEPTH, D_MODEL)),
        "g_ffn_pre": gain((DEPTH, D_MODEL)), "g_ffn_post": gain((DEPTH, D_MODEL)),
        "ev_w_in": w((N_EVEN, D_MODEL, EVEN_IN), D_MODEL),
        "ev_g_q": gain((N_EVEN, MLA_Q_RANK)),
        "ev_w_uq": w((N_EVEN, MLA_Q_RANK, MLA_HEADS * (MLA_NOPE + MLA_ROPE)), MLA_Q_RANK),
        "ev_g_kv": gain((N_EVEN, MLA_KV_RANK)),
        "ev_w_ukv": w((N_EVEN, MLA_KV_RANK, MLA_HEADS * (MLA_NOPE + MLA_V)), MLA_KV_RANK),
        "ev_w_gate": w((N_EVEN, GLA_GATE_RANK, GLA_HEADS * GLA_DK), GLA_GATE_RANK),
        "ev_b_gate": small((N_EVEN, GLA_HEADS * GLA_DK), 0.1),
        "ev_g_gla": gain((N_EVEN, GLA_DV)),
        "ev_w_out": w((N_EVEN, EVEN_OUT, D_MODEL), EVEN_OUT),
        "od_w_in": w((N_ODD, D_MODEL, 2 * MIX_WIDTH), D_MODEL),
        "od_conv_w": w((N_ODD, MLSTM_CONV, MIX_WIDTH), MLSTM_CONV),
        "od_conv_b": small((N_ODD, MIX_WIDTH)),
        "od_w_q": w((N_ODD, MLSTM_HEADS, MLSTM_DH, MLSTM_DH), MLSTM_DH),
        "od_w_k": w((N_ODD, MLSTM_HEADS, MLSTM_DH, MLSTM_DH), MLSTM_DH),
        "od_w_v": w((N_ODD, MLSTM_HEADS, MLSTM_DH, MLSTM_DH), MLSTM_DH),
        "od_w_gates": w((N_ODD, 3 * MIX_WIDTH, 2 * MLSTM_HEADS), 3 * MIX_WIDTH),
        "od_b_gates": od_b_gates,
        "od_g_hnorm": gain((N_ODD, MIX_WIDTH)),
        "od_skip": gain((N_ODD, MIX_WIDTH)),
        "od_w_out": w((N_ODD, MIX_WIDTH, D_MODEL), MIX_WIDTH),
        "xa_w_q": w((DEPTH, D_MODEL, D_MODEL), D_MODEL),
        "xa_w_k": w((DEPTH, D_MODEL, D_MODEL), D_MODEL),
        "xa_w_v": w((DEPTH, D_MODEL, D_MODEL), D_MODEL),
        "xa_w_o": w((DEPTH, D_MODEL, D_MODEL), D_MODEL),
        "ffn_w1": w((DEPTH, D_MODEL, D_FF), D_MODEL),
        "ffn_w2": w((DEPTH, D_FF, D_MODEL), D_FF),
    }


def reference(x, mem, positions,
              g_mix_pre, g_mix_post, g_xattn_pre, g_xattn_post, g_mem, g_ffn_pre, g_ffn_post,
              ev_w_in, ev_g_q, ev_w_uq, ev_g_kv, ev_w_ukv, ev_w_gate, ev_b_gate, ev_g_gla, ev_w_out,
              od_w_in, od_conv_w, od_conv_b, od_w_q, od_w_k, od_w_v, od_w_gates, od_b_gates,
              od_g_hnorm, od_skip, od_w_out,
              xa_w_q, xa_w_k, xa_w_v, xa_w_o,
              ffn_w1, ffn_w2):
    h = x
    for layer in range(DEPTH):
        j = layer // 2
        hn = _rmsnorm(h, g_mix_pre[layer])
        if layer % 2 == 0:
            mix = _even_mixer(hn, positions, ev_w_in[j], ev_g_q[j], ev_w_uq[j], ev_g_kv[j],
                              ev_w_ukv[j], ev_w_gate[j], ev_b_gate[j], ev_g_gla[j], ev_w_out[j])
        else:
            mix = _odd_mixer(hn, od_w_in[j], od_conv_w[j], od_conv_b[j], od_w_q[j], od_w_k[j],
                             od_w_v[j], od_w_gates[j], od_b_gates[j], od_g_hnorm[j], od_skip[j],
                             od_w_out[j])
        h = h + _rmsnorm(mix, g_mix_post[layer])
        mem_n = _rmsnorm(mem, g_mem[layer])
        xa = _mem_xattn(_rmsnorm(h, g_xattn_pre[layer]), mem_n,
                        xa_w_q[layer], xa_w_k[layer], xa_w_v[layer], xa_w_o[layer])
        h = h + _rmsnorm(xa, g_xattn_post[layer])
        f = _sq_relu_mlp(_rmsnorm(h, g_ffn_pre[layer]), ffn_w1[layer], ffn_w2[layer])
        h = h + _rmsnorm(f, g_ffn_post[layer])
    return h
```

```python
import functools

import numpy as np
import jax
import jax.numpy as jnp
from jax import lax
from jax.experimental import pallas as pl
from jax.experimental.pallas import tpu as pltpu

F32 = jnp.float32
BF16 = jnp.bfloat16

D_MODEL = 1024
DEPTH = 4
EPS = 1e-6
MLA_HEADS = 8
MLA_NOPE = 64
MLA_ROPE = 32
MLA_V = 64
MLA_Q_RANK = 256
MLA_KV_RANK = 128
ROPE_THETA = 10000.0
GLA_HEADS = 4
GLA_DK = 64
GLA_DV = 128
GLA_GATE_RANK = 16
GLA_TAU = 16.0
MLSTM_HEADS = 4
MLSTM_DH = 256
MLSTM_CONV = 4
XATTN_HEADS = 4
XATTN_DH = 256
D_FF = 4096
EVEN_SPLITS = (256, 128, 32, 256, 256, 512, 16, 512)

V7X_LANES = 128
V7X_VMEM_BYTES = 64 * 1024 * 1024
VMEM_LIMIT = (V7X_VMEM_BYTES * 3) // 4

ROW_TILE = 512
FF_CHUNK = 1024
ATTN_TQ = 512
ATTN_TK = 512
SCAN_CHUNK = 128
HEAD_PAD = V7X_LANES

LOG2E = 1.4426950408889634
NEG_BIG = -1e30


def _cparams(sem):
    return pltpu.CompilerParams(dimension_semantics=sem, vmem_limit_bytes=VMEM_LIMIT)


def _resident(shape):
    nd = len(shape)
    return pl.BlockSpec(shape, lambda *_: (0,) * nd, pipeline_mode=pl.Buffered(1))


def _rows(tile, width):
    return pl.BlockSpec((tile, width), lambda i: (i, 0))


def _rms(x, g):
    return x * lax.rsqrt(jnp.mean(x * x, axis=-1, keepdims=True) + EPS) * g


def _silu(x):
    return x * (1.0 / (1.0 + jnp.exp(-x)))


def _log_sigmoid(x):
    return jnp.minimum(x, 0.0) - jnp.log(1.0 + jnp.exp(-jnp.abs(x)))


def _dot(a, b):
    return jnp.dot(a, b, preferred_element_type=F32)


def _dot_nt(a, b):
    return lax.dot_general(a, b, (((1,), (1,)), ((), ())), preferred_element_type=F32)


def _dot_tn(a, b):
    return lax.dot_general(a, b, (((0,), (0,)), ((), ())), preferred_element_type=F32)


def _split3(x):
    x1 = x.astype(BF16)
    r = x - x1.astype(F32)
    x2 = r.astype(BF16)
    x3 = (r - x2.astype(F32)).astype(BF16)
    return x1, x2, x3


def _rope_body(pos_ref, freq_ref, cos_ref, sin_ref):
    ang = pos_ref[...] * freq_ref[...]
    lane = lax.broadcasted_iota(jnp.int32, ang.shape, 1)
    on = jnp.logical_and(lane >= MLA_NOPE, lane < MLA_NOPE + MLA_ROPE)
    cos_ref[...] = jnp.where(on, jnp.cos(ang), 0.0)
    sin_ref[...] = jnp.where(on, jnp.sin(ang), 0.0)


def _rope_tables(positions):
    s = positions.shape[-1]
    pos = positions.astype(F32).reshape(s, 1)
    inv_freq = ROPE_THETA ** (-jnp.arange(0, MLA_ROPE, 2, dtype=F32) / MLA_ROPE)
    freq = jnp.concatenate([jnp.zeros((MLA_NOPE,), F32), inv_freq, inv_freq,
                            jnp.zeros((HEAD_PAD - MLA_NOPE - MLA_ROPE,), F32)]).reshape(1, HEAD_PAD)
    tile = min(2048, s)
    return pl.pallas_call(
        _rope_body,
        grid=(s // tile,),
        in_specs=[_rows(tile, 1), _resident((1, HEAD_PAD))],
        out_specs=[_rows(tile, HEAD_PAD), _rows(tile, HEAD_PAD)],
        out_shape=[jax.ShapeDtypeStruct((s, HEAD_PAD), F32)] * 2,
        compiler_params=_cparams(("parallel",)),
        name="rope_tables",
    )(pos, freq)


_EV_OFF = np.cumsum((0, 256, 128, 128, 128, 256, 256, 512, 128, 512))


def _even_proj_body(h_ref, gpre_ref, win_ref, gq_ref, wuq_ref, gkv_ref, wukv_ref, wgate_ref, bgate_ref,
                    cos_ref, sin_ref,
                    q_ref, k_ref, v_ref, gq_out, gk_out, gv_out, la_out, r_out):
    o = _EV_OFF
    hn = _rms(h_ref[...], gpre_ref[...]).astype(BF16)
    proj = _dot(hn, win_ref[...])
    c_q = proj[:, o[0]:o[1]]
    c_kv = proj[:, o[1]:o[2]]
    kpe_a = proj[:, o[2]:o[3]]
    kpe_b = proj[:, o[3]:o[4]]
    cosk = cos_ref[...]
    sink = sin_ref[...]
    nh = MLA_HEADS
    width = nh * HEAD_PAD

    qs = (MLA_NOPE + MLA_ROPE) ** -0.5 * LOG2E
    lane = lax.broadcasted_iota(jnp.int32, cosk.shape, 1)
    cosq = qs * jnp.where(lane < MLA_NOPE, 1.0, cosk)
    sinq = qs * sink
    cqn = _rms(c_q, gq_ref[...]).astype(BF16)
    qab = _dot(cqn, wuq_ref[...])
    q = qab[:, :width] * jnp.tile(cosq, (1, nh)) + qab[:, width:] * jnp.tile(sinq, (1, nh))
    q_ref[...] = q.astype(BF16)

    ckvn = _rms(c_kv, gkv_ref[...]).astype(BF16)
    kv = _dot(ckvn, wukv_ref[...])
    kpe = kpe_a * cosk + kpe_b * sink
    k_ref[...] = (kv[:, :width] + jnp.tile(kpe, (1, nh))).astype(BF16)
    v_ref[...] = kv[:, width:].astype(BF16)

    gq_out[...] = proj[:, o[4]:o[5]]
    gk_out[...] = proj[:, o[5]:o[6]]
    gv_out[...] = proj[:, o[6]:o[7]]
    glr = proj[:, o[7]:o[8]].astype(BF16)
    x = _dot(glr, wgate_ref[...]) + bgate_ref[...]
    la_out[...] = _log_sigmoid(x) * (1.0 / GLA_TAU)
    r_out[...] = proj[:, o[8]:o[9]]


def _even_proj(h, g_pre, w_in, g_q, w_uq, g_kv, w_ukv, w_gate, b_gate, cos_t, sin_t):
    s = h.shape[0]
    tm = min(ROW_TILE, s)
    width = MLA_HEADS * HEAD_PAD
    gdk = GLA_HEADS * GLA_DK
    gdv = GLA_HEADS * GLA_DV
    outs = [(width, BF16), (width, BF16), (width, BF16), (gdk, F32), (gdk, F32), (gdv, F32), (gdk, F32), (gdv, F32)]
    return pl.pallas_call(
        _even_proj_body,
        grid=(s // tm,),
        in_specs=[_rows(tm, D_MODEL), _resident(g_pre.shape), _resident(w_in.shape), _resident(g_q.shape),
                  _resident(w_uq.shape), _resident(g_kv.shape), _resident(w_ukv.shape), _resident(w_gate.shape),
                  _resident(b_gate.shape), _rows(tm, HEAD_PAD), _rows(tm, HEAD_PAD)],
        out_specs=[_rows(tm, w) for w, _ in outs],
        out_shape=[jax.ShapeDtypeStruct((s, w), dt) for w, dt in outs],
        compiler_params=_cparams(("parallel",)),
        name="even_proj",
    )(h, g_pre, w_in, g_q, w_uq, g_kv, w_ukv, w_gate, b_gate, cos_t, sin_t)


def _prep_even_weights(w_in, w_uq, w_ukv, w_gate, b_gate):
    d = w_in.shape[0]
    off = np.cumsum((0,) + EVEN_SPLITS)
    seg = [w_in[:, off[i]:off[i + 1]] for i in range(len(EVEN_SPLITS))]
    c_q, c_kv, k_pe, gq, gk, gv, glr, r = seg
    half = MLA_ROPE // 2
    z = lambda n: jnp.zeros((d, n), F32)
    pad = HEAD_PAD - MLA_NOPE - MLA_ROPE
    kpe_a = jnp.concatenate([z(MLA_NOPE), k_pe, z(pad)], axis=1)
    kpe_b = jnp.concatenate([z(MLA_NOPE), -k_pe[:, half:], k_pe[:, :half], z(pad)], axis=1)
    glr_p = jnp.concatenate([glr, z(HEAD_PAD - GLA_GATE_RANK)], axis=1)
    w_in_p = jnp.concatenate([c_q, c_kv, kpe_a, kpe_b, gq, gk, gv, glr_p, r], axis=1).astype(BF16)

    nh = MLA_HEADS
    wq3 = w_uq.reshape(MLA_Q_RANK, nh, MLA_NOPE + MLA_ROPE)
    nope, rope = wq3[..., :MLA_NOPE], wq3[..., MLA_NOPE:]
    zq = lambda n: jnp.zeros((MLA_Q_RANK, nh, n), F32)
    wa = jnp.concatenate([nope, rope, zq(pad)], axis=-1).reshape(MLA_Q_RANK, nh * HEAD_PAD)
    wb = jnp.concatenate([zq(MLA_NOPE), -rope[..., half:], rope[..., :half], zq(pad)],
                         axis=-1).reshape(MLA_Q_RANK, nh * HEAD_PAD)
    w_uq_p = jnp.concatenate([wa, wb], axis=1).astype(BF16)

    wkv3 = w_ukv.reshape(MLA_KV_RANK, nh, MLA_NOPE + MLA_V)
    zk = jnp.zeros((MLA_KV_RANK, nh, HEAD_PAD - MLA_NOPE), F32)
    zv = jnp.zeros((MLA_KV_RANK, nh, HEAD_PAD - MLA_V), F32)
    wk = jnp.concatenate([wkv3[..., :MLA_NOPE], zk], axis=-1).reshape(MLA_KV_RANK, nh * HEAD_PAD)
    wv = jnp.concatenate([wkv3[..., MLA_NOPE:], zv], axis=-1).reshape(MLA_KV_RANK, nh * HEAD_PAD)
    w_ukv_p = jnp.concatenate([wk, wv], axis=1).astype(BF16)

    w_gate_p = jnp.concatenate([w_gate, jnp.zeros((HEAD_PAD - GLA_GATE_RANK, w_gate.shape[1]), F32)],
                               axis=0).astype(BF16)
    return w_in_p, w_uq_p, w_ukv_p, w_gate_p, b_gate.reshape(1, -1)


def _mla_attn_body(q_ref, k_ref, v_ref, o_ref, m_scr, l_scr, acc_scr, *, tq, tk):
    i = pl.program_id(1)
    q = q_ref[...]
    m_scr[...] = jnp.full(m_scr.shape, NEG_BIG, F32)
    l_scr[...] = jnp.zeros(l_scr.shape, F32)
    acc_scr[...] = jnp.zeros(acc_scr.shape, F32)

    def block(j, masked):
        start = pl.multiple_of(j * tk, tk)
        kj = k_ref[pl.ds(start, tk), :]
        vj = v_ref[pl.ds(start, tk), :]
        s = _dot_nt(q, kj)
        if masked:
            row = lax.broadcasted_iota(jnp.int32, s.shape, 0)
            col = lax.broadcasted_iota(jnp.int32, s.shape, 1)
            s = jnp.where(col <= row, s, NEG_BIG)
        m_prev = m_scr[...]
        m_next = jnp.maximum(m_prev, jnp.max(s, axis=-1, keepdims=True))
        p = jnp.exp2(s - m_next)
        alpha = jnp.exp2(m_prev - m_next)
        l_scr[...] = alpha * l_scr[...] + jnp.sum(p, axis=-1, keepdims=True)
        acc_scr[...] = alpha * acc_scr[...] + _dot(p.astype(BF16), vj)
        m_scr[...] = m_next

    def body(j, carry):
        block(j, False)
        return carry

    lax.fori_loop(0, i, body, 0)
    block(i, True)
    o_ref[...] = (acc_scr[...] * (1.0 / l_scr[...])).astype(o_ref.dtype)


def _mla_attn(q, k, v):
    s = q.shape[0]
    tq = min(ATTN_TQ, s)
    tk = tq
    kv_spec = pl.BlockSpec((s, HEAD_PAD), lambda h, i: (0, h))
    return pl.pallas_call(
        functools.partial(_mla_attn_body, tq=tq, tk=tk),
        grid=(MLA_HEADS, s // tq),
        in_specs=[pl.BlockSpec((tq, HEAD_PAD), lambda h, i: (i, h)), kv_spec, kv_spec],
        out_specs=pl.BlockSpec((tq, HEAD_PAD), lambda h, i: (i, h)),
        out_shape=jax.ShapeDtypeStruct((s, MLA_HEADS * HEAD_PAD), BF16),
        scratch_shapes=[pltpu.VMEM((tq, 1), F32), pltpu.VMEM((tq, 1), F32), pltpu.VMEM((tq, HEAD_PAD), F32)],
        compiler_params=_cparams(("parallel", "arbitrary")),
        name="mla_attn",
    )(q, k, v)


def _gla_levels(chunk):
    n = 0
    while (1 << n) < chunk:
        n += 1
    return n


def _gla_constants(chunk):
    t = np.arange(chunk)
    mats = [t[None, :] <= t[:, None]]
    m_list, n_list, masks = [], [], []
    b = 1
    while b < chunk:
        blk = t // (2 * b)
        upper = (t % (2 * b)) >= b
        e = blk * 2 * b + b - 1
        m_list.append(upper[:, None] & (t[None, :] > e[:, None]) & (t[None, :] <= t[:, None]))
        n_list.append((~upper)[:, None] & (t[None, :] > t[:, None]) & (t[None, :] <= e[:, None]))
        masks.append(upper[:, None] & (~upper)[None, :] & (blk[:, None] == blk[None, :]))
        b *= 2
    masks.append(t[:, None] == t[None, :])
    mats = np.concatenate(mats + m_list + n_list, axis=0).astype(np.float32)
    masks = np.stack(masks).astype(np.float32)
    masks = np.tile(masks, (1, GLA_HEADS, 1))
    return jnp.asarray(mats, BF16), jnp.asarray(masks, F32)


def _gla_body(q_ref, k_ref, v_ref, g_ref, mats_ref, masks_ref, o_ref, st_ref, *, chunk):
    nlev = _gla_levels(chunk)
    nh, dk, dv = GLA_HEADS, GLA_DK, GLA_DV

    @pl.when(pl.program_id(0) == 0)
    def _():
        st_ref[...] = jnp.zeros(st_ref.shape, F32)

    q = q_ref[...] * (dk ** -0.5)
    k = k_ref[...]
    g = g_ref[...]
    w = nh * dk
    e_all = _dot(mats_ref[...], jnp.concatenate(_split3(g), axis=1))
    e_all = e_all[:, :w] + e_all[:, w:2 * w] + e_all[:, 2 * w:]
    b = e_all[:chunk]
    b_end = b[chunk - 1:chunk]

    head_of_lane = lax.broadcasted_iota(jnp.int32, (chunk, w), 1) // dk

    def stack_heads(x):
        return jnp.concatenate([jnp.where(head_of_lane == hd, x, 0.0) for hd in range(nh)], axis=0).astype(BF16)

    attn = _dot_nt(stack_heads(q), k.astype(BF16)) * masks_ref[nlev]
    for lv in range(nlev):
        ql = q * jnp.exp(e_all[(1 + lv) * chunk:(2 + lv) * chunk])
        kl = k * jnp.exp(e_all[(1 + nlev + lv) * chunk:(2 + nlev + lv) * chunk])
        attn = attn + _dot_nt(stack_heads(ql), kl.astype(BF16)) * masks_ref[lv]
    attn = attn.astype(BF16)

    st = st_ref[...]
    inter = _dot_nt(stack_heads(q * jnp.exp(b)), st.astype(BF16))
    kg = (k * jnp.exp(b_end - b)).astype(BF16)
    head_of_state_lane = lax.broadcasted_iota(jnp.int32, st.shape, 1) // dk
    new = st * jnp.exp(b_end)
    for hd in range(nh):
        vh = v_ref[:, hd * dv:(hd + 1) * dv]
        intra = _dot(attn[hd * chunk:(hd + 1) * chunk], vh.astype(BF16))
        o_ref[:, hd * dv:(hd + 1) * dv] = intra + inter[hd * chunk:(hd + 1) * chunk]
        upd = _dot_tn(vh.astype(BF16), kg)
        new = new + jnp.where(head_of_state_lane == hd, upd, 0.0)
    st_ref[...] = new


def _gla(gq, gk, gv, log_a):
    s = gq.shape[0]
    chunk = min(SCAN_CHUNK, s)
    mats, masks = _gla_constants(chunk)
    gdk = GLA_HEADS * GLA_DK
    gdv = GLA_HEADS * GLA_DV
    return pl.pallas_call(
        functools.partial(_gla_body, chunk=chunk),
        grid=(s // chunk,),
        in_specs=[_rows(chunk, gdk), _rows(chunk, gdk), _rows(chunk, gdv), _rows(chunk, gdk),
                  _resident(mats.shape), _resident(masks.shape)],
        out_specs=_rows(chunk, gdv),
        out_shape=jax.ShapeDtypeStruct((s, gdv), F32),
        scratch_shapes=[pltpu.VMEM((GLA_DV, gdk), F32)],
        compiler_params=_cparams(("arbitrary",)),
        name="gla_scan",
    )(gq, gk, gv, log_a, mats, masks)


def _even_out_body(h_ref, a_ref, o_ref, r_ref, ggla_ref, wa_ref, wg_ref, gpost_ref, out_ref):
    og = o_ref[...]
    gn = jnp.concatenate([_rms(og[:, hd * GLA_DV:(hd + 1) * GLA_DV], ggla_ref[...]) for hd in range(GLA_HEADS)],
                         axis=1)
    g = (gn * _silu(r_ref[...])).astype(BF16)
    mix = _dot(a_ref[...], wa_ref[...]) + _dot(g, wg_ref[...])
    out_ref[...] = h_ref[...] + _rms(mix, gpost_ref[...])


def _even_out(h, a, o_gla, r, g_gla, w_a, w_g, g_post):
    s = h.shape[0]
    tm = min(ROW_TILE, s)
    return pl.pallas_call(
        _even_out_body,
        grid=(s // tm,),
        in_specs=[_rows(tm, D_MODEL), _rows(tm, a.shape[1]), _rows(tm, o_gla.shape[1]), _rows(tm, r.shape[1]),
                  _resident(g_gla.shape), _resident(w_a.shape), _resident(w_g.shape), _resident(g_post.shape)],
        out_specs=_rows(tm, D_MODEL),
        out_shape=jax.ShapeDtypeStruct((s, D_MODEL), F32),
        compiler_params=_cparams(("parallel",)),
        name="even_out",
    )(h, a, o_gla, r, g_gla, w_a, w_g, g_post)


CONV_HALO = 8


def _odd_proj_body(h_ref, gpre_ref, win_ref, cw_ref, cb_ref, wq_ref, wk_ref, wv_ref, wgq_ref, wgk_ref, wgv_ref,
                   bg_ref, q_out, k_out, v_out, gates_out, xc_out, z_out, xbuf):
    tm = h_ref.shape[0]
    width = D_MODEL

    @pl.when(pl.program_id(0) == 0)
    def _():
        xbuf[0:CONV_HALO, :] = jnp.zeros((CONV_HALO, width), F32)

    hn = _rms(h_ref[...], gpre_ref[...]).astype(BF16)
    xz = _dot(hn, win_ref[...])
    x_m = xz[:, :width]
    z_out[...] = xz[:, width:]
    xbuf[CONV_HALO:CONV_HALO + tm, :] = x_m
    conv = cb_ref[...] + cw_ref[MLSTM_CONV - 1:MLSTM_CONV, :] * x_m
    for j in range(MLSTM_CONV - 1):
        back = MLSTM_CONV - 1 - j
        conv = conv + cw_ref[j:j + 1, :] * xbuf[CONV_HALO - back:CONV_HALO - back + tm, :]
    xbuf[0:CONV_HALO, :] = x_m[tm - CONV_HALO:, :]
    x_c = _silu(conv)
    xc_out[...] = x_c

    dh = MLSTM_DH
    xcb = x_c.astype(BF16)
    xmb = x_m.astype(BF16)
    qs, ks, vs = [], [], []
    for hd in range(MLSTM_HEADS):
        sl = slice(hd * dh, (hd + 1) * dh)
        qs.append(_dot(xcb[:, sl], wq_ref[hd]))
        ks.append(_dot(xcb[:, sl], wk_ref[hd]))
        vs.append(_dot(xmb[:, sl], wv_ref[hd]))
    q = jnp.concatenate(qs, axis=1).astype(BF16)
    k = jnp.concatenate(ks, axis=1).astype(BF16)
    v = jnp.concatenate(vs, axis=1).astype(BF16)
    gates_out[...] = _dot(q, wgq_ref[...]) + _dot(k, wgk_ref[...]) + _dot(v, wgv_ref[...]) + bg_ref[...]
    q_out[...] = (q.astype(F32) * (dh ** -0.5)).astype(BF16)
    k_out[...] = k
    v_out[...] = v


def _odd_proj(h, g_pre, w_in, conv_w, conv_b, w_q, w_k, w_v, wg_q, wg_k, wg_v, b_g):
    s = h.shape[0]
    tm = min(ROW_TILE, s)
    outs = [(D_MODEL, BF16), (D_MODEL, BF16), (D_MODEL, BF16), (V7X_LANES, F32), (D_MODEL, F32), (D_MODEL, F32)]
    ins = (g_pre, w_in, conv_w, conv_b, w_q, w_k, w_v, wg_q, wg_k, wg_v, b_g)
    return pl.pallas_call(
        _odd_proj_body,
        grid=(s // tm,),
        in_specs=[_rows(tm, D_MODEL)] + [_resident(a.shape) for a in ins],
        out_specs=[_rows(tm, w) for w, _ in outs],
        out_shape=[jax.ShapeDtypeStruct((s, w), dt) for w, dt in outs],
        scratch_shapes=[pltpu.VMEM((CONV_HALO + tm, D_MODEL), F32)],
        compiler_params=_cparams(("arbitrary",)),
        name="odd_proj",
    )(h, *ins)


def _prep_odd_gate_weights(w_gates, b_gates):
    nh, dh = MLSTM_HEADS, MLSTM_DH
    w4 = w_gates.reshape(nh, 3, dh, 2 * nh)
    pad = jnp.zeros((nh * dh, V7X_LANES - 2 * nh), F32)
    parts = [jnp.concatenate([w4[:, j].reshape(nh * dh, 2 * nh), pad], axis=1).astype(BF16) for j in range(3)]
    b = jnp.concatenate([b_gates, jnp.zeros((V7X_LANES - 2 * nh,), F32)]).reshape(1, V7X_LANES)
    return parts[0], parts[1], parts[2], b


def _mlstm_body(q_ref, k_ref, v_ref, gcol_ref, grow_ref, tri_ref, trit_ref, o_ref, c_scr, n_scr, m_scr, *, chunk):
    nh, dh = MLSTM_HEADS, MLSTM_DH

    @pl.when(pl.program_id(0) == 0)
    def _():
        c_scr[...] = jnp.zeros(c_scr.shape, F32)
        n_scr[...] = jnp.zeros(n_scr.shape, F32)
        m_scr[...] = jnp.zeros(m_scr.shape, F32)

    gc = gcol_ref[...]
    gr = grow_ref[0]
    lanes = gc.shape[1]
    bc_all = _dot(tri_ref[...], jnp.concatenate(_split3(_log_sigmoid(gc)), axis=1))
    bc_all = bc_all[:, :lanes] + bc_all[:, lanes:2 * lanes] + bc_all[:, 2 * lanes:]
    nr = gr.shape[0]
    br_all = _dot(jnp.concatenate(_split3(_log_sigmoid(gr)), axis=0), trit_ref[...])
    br_all = br_all[:nr] + br_all[nr:2 * nr] + br_all[2 * nr:]

    row = lax.broadcasted_iota(jnp.int32, (chunk, chunk), 0)
    col = lax.broadcasted_iota(jnp.int32, (chunk, chunk), 1)
    causal = col <= row
    for hd in range(nh):
        sl = slice(hd * dh, (hd + 1) * dh)
        b_c = bc_all[:, nh + hd:nh + hd + 1]
        i_c = gc[:, hd:hd + 1]
        b_r = br_all[nh + hd:nh + hd + 1, :]
        i_r = gr[hd:hd + 1, :]
        m_old = m_scr[hd:hd + 1, 0:1]
        qh = q_ref[:, sl]
        kh = k_ref[:, sl]
        vh = v_ref[:, sl]

        log_d = jnp.where(causal, b_c - b_r + i_r, NEG_BIG)
        m_inter = b_c + m_old
        m_t = jnp.maximum(m_inter, jnp.max(log_d, axis=-1, keepdims=True))
        w_intra = jnp.exp(log_d - m_t) * _dot_nt(qh, kh)
        w_inter = jnp.exp(m_inter - m_t)
        c_old = c_scr[hd]
        n_old = n_scr[hd:hd + 1, :]
        num = _dot(w_intra.astype(BF16), vh) + w_inter * _dot(qh, c_old.astype(BF16))
        den = (jnp.sum(w_intra, axis=-1, keepdims=True)
               + w_inter * jnp.sum(qh.astype(F32) * n_old, axis=-1, keepdims=True))
        o_ref[:, sl] = num / jnp.maximum(jnp.abs(den), jnp.exp(-m_t))

        b_end = b_c[chunk - 1:chunk, :]
        log_w = b_end - b_c + i_c
        m_new = jnp.maximum(b_end + m_old, jnp.max(log_w, axis=0, keepdims=True))
        w_s = jnp.exp(log_w - m_new)
        decay = jnp.exp(b_end + m_old - m_new)
        kw = kh.astype(F32) * w_s
        c_scr[hd] = decay * c_old + _dot_tn(kw.astype(BF16), vh)
        n_scr[hd:hd + 1, :] = decay * n_old + jnp.sum(kw, axis=0, keepdims=True)
        m_scr[hd:hd + 1, :] = jnp.broadcast_to(m_new, (1, m_scr.shape[1]))


def _mlstm(q, k, v, gates):
    s = q.shape[0]
    chunk = min(SCAN_CHUNK, s)
    nc = s // chunk
    ng = 2 * MLSTM_HEADS
    grow = gates[:, :ng].reshape(nc, chunk, ng).transpose(0, 2, 1)
    t = np.arange(chunk)
    tri = (t[None, :] <= t[:, None]).astype(np.float32)
    return pl.pallas_call(
        functools.partial(_mlstm_body, chunk=chunk),
        grid=(nc,),
        in_specs=[_rows(chunk, D_MODEL), _rows(chunk, D_MODEL), _rows(chunk, D_MODEL), _rows(chunk, V7X_LANES),
                  pl.BlockSpec((1, ng, chunk), lambda c: (c, 0, 0)),
                  _resident((chunk, chunk)), _resident((chunk, chunk))],
        out_specs=_rows(chunk, D_MODEL),
        out_shape=jax.ShapeDtypeStruct((s, D_MODEL), F32),
        scratch_shapes=[pltpu.VMEM((MLSTM_HEADS, MLSTM_DH, MLSTM_DH), F32),
                        pltpu.VMEM((8, MLSTM_DH), F32),
                        pltpu.VMEM((8, V7X_LANES), F32)],
        compiler_params=_cparams(("arbitrary",)),
        name="mlstm_scan",
    )(q, k, v, gates, grow, jnp.asarray(tri, BF16), jnp.asarray(tri.T, BF16))


def _odd_out_body(h_ref, hc_ref, xc_ref, z_ref, gh_ref, skip_ref, wo_ref, gpost_ref, out_ref):
    hc = hc_ref[...]
    dh = MLSTM_DH
    parts = []
    for hd in range(MLSTM_HEADS):
        seg = hc[:, hd * dh:(hd + 1) * dh]
        cen = seg - jnp.mean(seg, axis=-1, keepdims=True)
        parts.append(cen * lax.rsqrt(jnp.mean(cen * cen, axis=-1, keepdims=True) + EPS))
    hn = jnp.concatenate(parts, axis=1) * gh_ref[...]
    out = ((hn + skip_ref[...] * xc_ref[...]) * _silu(z_ref[...])).astype(BF16)
    out_ref[...] = h_ref[...] + _rms(_dot(out, wo_ref[...]), gpost_ref[...])


def _odd_out(h, hcell, x_c, z, g_hnorm, skip, w_out, g_post):
    s = h.shape[0]
    tm = min(ROW_TILE, s)
    return pl.pallas_call(
        _odd_out_body,
        grid=(s // tm,),
        in_specs=[_rows(tm, D_MODEL)] * 4 + [_resident(g_hnorm.shape), _resident(skip.shape),
                                              _resident(w_out.shape), _resident(g_post.shape)],
        out_specs=_rows(tm, D_MODEL),
        out_shape=jax.ShapeDtypeStruct((s, D_MODEL), F32),
        compiler_params=_cparams(("parallel",)),
        name="odd_out",
    )(h, hcell, x_c, z, g_hnorm, skip, w_out, g_post)


def _xattn_body(h_ref, mem_ref, gmem_ref, gpre_ref, wq_ref, wk_ref, wv_ref, wo_ref, gpost_ref, out_ref,
                k_scr, v_scr):
    @pl.when(pl.program_id(0) == 0)
    def _():
        mem_n = _rms(mem_ref[...], gmem_ref[...]).astype(BF16)
        k_scr[...] = _dot(mem_n, wk_ref[...]).astype(BF16)
        v_scr[...] = _dot(mem_n, wv_ref[...]).astype(BF16)

    h = h_ref[...]
    hn = _rms(h, gpre_ref[...]).astype(BF16)
    q = _dot(hn, wq_ref[...]).astype(BF16)
    dh = XATTN_DH
    scale = dh ** -0.5
    outs = []
    for hd in range(XATTN_HEADS):
        sl = slice(hd * dh, (hd + 1) * dh)
        s = _dot_nt(q[:, sl], k_scr[:, sl]) * scale
        p = jnp.exp(s - jnp.max(s, axis=-1, keepdims=True))
        inv_l = 1.0 / jnp.sum(p, axis=-1, keepdims=True)
        outs.append((_dot(p.astype(BF16), v_scr[:, sl]) * inv_l).astype(BF16))
    o = jnp.concatenate(outs, axis=1)
    out_ref[...] = h + _rms(_dot(o, wo_ref[...]), gpost_ref[...])


def _xattn(h, mem, g_mem, g_pre, w_q, w_k, w_v, w_o, g_post):
    s = h.shape[0]
    tm = min(ROW_TILE, s)
    ins = (mem, g_mem, g_pre, w_q, w_k, w_v, w_o, g_post)
    return pl.pallas_call(
        _xattn_body,
        grid=(s // tm,),
        in_specs=[_rows(tm, D_MODEL)] + [_resident(a.shape) for a in ins],
        out_specs=_rows(tm, D_MODEL),
        out_shape=jax.ShapeDtypeStruct((s, D_MODEL), F32),
        scratch_shapes=[pltpu.VMEM((mem.shape[0], D_MODEL), BF16), pltpu.VMEM((mem.shape[0], D_MODEL), BF16)],
        compiler_params=_cparams(("arbitrary",)),
        name="mem_xattn",
    )(h, *ins)


def _ffn_body(h_ref, gpre_ref, w1_ref, w2_ref, gpost_ref, out_ref):
    h = h_ref[...]
    xn = _rms(h, gpre_ref[...]).astype(BF16)
    acc = None
    for c in range(D_FF // FF_CHUNK):
        sl = slice(c * FF_CHUNK, (c + 1) * FF_CHUNK)
        a = jnp.square(jnp.maximum(_dot(xn, w1_ref[:, sl]), 0.0)).astype(BF16)
        part = _dot(a, w2_ref[sl, :])
        acc = part if acc is None else acc + part
    out_ref[...] = h + _rms(acc, gpost_ref[...])


def _ffn(h, g_pre, w1, w2, g_post):
    s = h.shape[0]
    tm = min(ROW_TILE, s)
    return pl.pallas_call(
        _ffn_body,
        grid=(s // tm,),
        in_specs=[_rows(tm, D_MODEL), _resident(g_pre.shape), _resident(w1.shape), _resident(w2.shape),
                  _resident(g_post.shape)],
        out_specs=_rows(tm, D_MODEL),
        out_shape=jax.ShapeDtypeStruct((s, D_MODEL), F32),
        compiler_params=_cparams(("parallel",)),
        name="sq_relu_mlp",
    )(h, g_pre, w1, w2, g_post)


def _row(g):
    return g.reshape(1, -1)


def kernel(x, mem, positions, g_mix_pre, g_mix_post, g_xattn_pre, g_xattn_post, g_mem, g_ffn_pre, g_ffn_post, ev_w_in, ev_g_q, ev_w_uq, ev_g_kv, ev_w_ukv, ev_w_gate, ev_b_gate, ev_g_gla, ev_w_out, od_w_in, od_conv_w, od_conv_b, od_w_q, od_w_k, od_w_v, od_w_gates, od_b_gates, od_g_hnorm, od_skip, od_w_out, xa_w_q, xa_w_k, xa_w_v, xa_w_o, ffn_w1, ffn_w2):
    batch, seq, d = x.shape
    assert batch == 1 and d == D_MODEL and seq % ROW_TILE == 0 and seq % ATTN_TQ == 0 and seq % SCAN_CHUNK == 0
    h = x.reshape(seq, d)
    mem2 = mem.reshape(mem.shape[1], d)
    cos_t, sin_t = _rope_tables(positions)
    n_a = MLA_HEADS * MLA_V

    for layer in range(DEPTH):
        j = layer // 2
        if layer % 2 == 0:
            w_in_p, w_uq_p, w_ukv_p, w_gate_p, b_gate = _prep_even_weights(
                ev_w_in[j], ev_w_uq[j], ev_w_ukv[j], ev_w_gate[j], ev_b_gate[j])
            q, k, v, gq, gk, gv, log_a, r = _even_proj(
                h, _row(g_mix_pre[layer]), w_in_p, _row(ev_g_q[j]), w_uq_p, _row(ev_g_kv[j]), w_ukv_p, w_gate_p,
                b_gate, cos_t, sin_t)
            a = _mla_attn(q, k, v)
            o_gla = _gla(gq, gk, gv, log_a)
            w_out = ev_w_out[j]
            w_a = jnp.concatenate(
                [w_out[:n_a].reshape(MLA_HEADS, MLA_V, d), jnp.zeros((MLA_HEADS, HEAD_PAD - MLA_V, d), F32)],
                axis=1).reshape(MLA_HEADS * HEAD_PAD, d).astype(BF16)
            h = _even_out(h, a, o_gla, r, _row(ev_g_gla[j]), w_a, w_out[n_a:].astype(BF16),
                          _row(g_mix_post[layer]))
        else:
            wg_q, wg_k, wg_v, b_g = _prep_odd_gate_weights(od_w_gates[j], od_b_gates[j])
            q, k, v, gates, x_c, z = _odd_proj(
                h, _row(g_mix_pre[layer]), od_w_in[j].astype(BF16), od_conv_w[j], _row(od_conv_b[j]),
                od_w_q[j].astype(BF16), od_w_k[j].astype(BF16), od_w_v[j].astype(BF16), wg_q, wg_k, wg_v, b_g)
            hcell = _mlstm(q, k, v, gates)
            h = _odd_out(h, hcell, x_c, z, _row(od_g_hnorm[j]), _row(od_skip[j]), od_w_out[j].astype(BF16),
                         _row(g_mix_post[layer]))
        h = _xattn(h, mem2, _row(g_mem[layer]), _row(g_xattn_pre[layer]), xa_w_q[layer].astype(BF16),
                   xa_w_k[layer].astype(BF16), xa_w_v[layer].astype(BF16), xa_w_o[layer].astype(BF16),
                   _row(g_xattn_post[layer]))
        h = _ffn(h, _row(g_ffn_pre[layer]), ffn_w1[layer].astype(BF16), ffn_w2[layer].astype(BF16),
                 _row(g_ffn_post[layer]))
    return h.reshape(batch, seq, d)
```

```python
import functools

import numpy as np
import jax
import jax.numpy as jnp
from jax import lax
from jax.experimental import pallas as pl
from jax.experimental.pallas import tpu as pltpu

F32 = jnp.float32
BF16 = jnp.bfloat16

D_MODEL = 1024
DEPTH = 4
EPS = 1e-6
MLA_HEADS = 8
MLA_NOPE = 64
MLA_ROPE = 32
MLA_V = 64
MLA_Q_RANK = 256
MLA_KV_RANK = 128
ROPE_THETA = 10000.0
GLA_HEADS = 4
GLA_DK = 64
GLA_DV = 128
GLA_GATE_RANK = 16
GLA_TAU = 16.0
MLSTM_HEADS = 4
MLSTM_DH = 256
MLSTM_CONV = 4
XATTN_HEADS = 4
XATTN_DH = 256
D_FF = 4096
EVEN_SPLITS = (256, 128, 32, 256, 256, 512, 16, 512)

V7X_LANES = 128
V7X_VMEM_BYTES = 64 * 1024 * 1024
VMEM_LIMIT = (V7X_VMEM_BYTES * 3) // 4

ROW_TILE = 512
FF_CHUNK = 1024
ATTN_TQ = 512
ATTN_TK = 512
ATTN_HEADS_PER_STEP = 2
SCAN_CHUNK = 128
HEAD_PAD = V7X_LANES

LOG2E = 1.4426950408889634
NEG_BIG = -1e30


def _cparams(sem):
    return pltpu.CompilerParams(dimension_semantics=sem, vmem_limit_bytes=VMEM_LIMIT)


def _resident(shape):
    nd = len(shape)
    return pl.BlockSpec(shape, lambda *_: (0,) * nd, pipeline_mode=pl.Buffered(1))


def _rows(tile, width):
    return pl.BlockSpec((tile, width), lambda i: (i, 0))


def _rms(x, g):
    return x * lax.rsqrt(jnp.mean(x * x, axis=-1, keepdims=True) + EPS) * g


def _silu(x):
    return x * (1.0 / (1.0 + jnp.exp(-x)))


def _log_sigmoid(x):
    return jnp.minimum(x, 0.0) - jnp.log(1.0 + jnp.exp(-jnp.abs(x)))


def _dot(a, b):
    return jnp.dot(a, b, preferred_element_type=F32)


def _dot_nt(a, b):
    return lax.dot_general(a, b, (((1,), (1,)), ((), ())), preferred_element_type=F32)


def _dot_tn(a, b):
    return lax.dot_general(a, b, (((0,), (0,)), ((), ())), preferred_element_type=F32)


def _split3(x):
    x1 = x.astype(BF16)
    r = x - x1.astype(F32)
    x2 = r.astype(BF16)
    x3 = (r - x2.astype(F32)).astype(BF16)
    return x1, x2, x3


def _rope_body(pos_ref, freq_ref, cos_ref, sin_ref):
    ang = pos_ref[...] * freq_ref[...]
    lane = lax.broadcasted_iota(jnp.int32, ang.shape, 1)
    on = jnp.logical_and(lane >= MLA_NOPE, lane < MLA_NOPE + MLA_ROPE)
    cos_ref[...] = jnp.where(on, jnp.cos(ang), 0.0)
    sin_ref[...] = jnp.where(on, jnp.sin(ang), 0.0)


def _rope_tables(positions):
    s = positions.shape[-1]
    pos = positions.astype(F32).reshape(s, 1)
    inv_freq = ROPE_THETA ** (-jnp.arange(0, MLA_ROPE, 2, dtype=F32) / MLA_ROPE)
    freq = jnp.concatenate([jnp.zeros((MLA_NOPE,), F32), inv_freq, inv_freq,
                            jnp.zeros((HEAD_PAD - MLA_NOPE - MLA_ROPE,), F32)]).reshape(1, HEAD_PAD)
    tile = min(2048, s)
    return pl.pallas_call(
        _rope_body,
        grid=(s // tile,),
        in_specs=[_rows(tile, 1), _resident((1, HEAD_PAD))],
        out_specs=[_rows(tile, HEAD_PAD), _rows(tile, HEAD_PAD)],
        out_shape=[jax.ShapeDtypeStruct((s, HEAD_PAD), F32)] * 2,
        compiler_params=_cparams(("parallel",)),
        name="rope_tables",
    )(pos, freq)


_EV_OFF = np.cumsum((0, 256, 128, 128, 128, 256, 256, 512, 128, 512))


def _even_proj_body(h_ref, gpre_ref, win_ref, gq_ref, wuq_ref, gkv_ref, wukv_ref, wgate_ref, bgate_ref,
                    cos_ref, sin_ref,
                    q_ref, k_ref, v_ref, gq_out, gk_out, gv_out, la_out, r_out):
    o = _EV_OFF
    hn = _rms(h_ref[...], gpre_ref[...]).astype(BF16)
    proj = _dot(hn, win_ref[...])
    c_q = proj[:, o[0]:o[1]]
    c_kv = proj[:, o[1]:o[2]]
    kpe_a = proj[:, o[2]:o[3]]
    kpe_b = proj[:, o[3]:o[4]]
    cosk = cos_ref[...]
    sink = sin_ref[...]
    nh = MLA_HEADS
    width = nh * HEAD_PAD

    qs = (MLA_NOPE + MLA_ROPE) ** -0.5 * LOG2E
    lane = lax.broadcasted_iota(jnp.int32, cosk.shape, 1)
    cosq = qs * jnp.where(lane < MLA_NOPE, 1.0, cosk)
    sinq = qs * sink
    cqn = _rms(c_q, gq_ref[...]).astype(BF16)
    qab = _dot(cqn, wuq_ref[...])
    q = qab[:, :width] * jnp.tile(cosq, (1, nh)) + qab[:, width:] * jnp.tile(sinq, (1, nh))
    q_ref[...] = q.astype(BF16)

    ckvn = _rms(c_kv, gkv_ref[...]).astype(BF16)
    kv = _dot(ckvn, wukv_ref[...])
    kpe = kpe_a * cosk + kpe_b * sink
    k_ref[...] = (kv[:, :width] + jnp.tile(kpe, (1, nh))).astype(BF16)
    v_ref[...] = kv[:, width:].astype(BF16)

    gq_out[...] = proj[:, o[4]:o[5]]
    gk_out[...] = proj[:, o[5]:o[6]]
    gv_out[...] = proj[:, o[6]:o[7]]
    glr = proj[:, o[7]:o[8]].astype(BF16)
    x = _dot(glr, wgate_ref[...]) + bgate_ref[...]
    la_out[...] = _log_sigmoid(x) * (1.0 / GLA_TAU)
    r_out[...] = proj[:, o[8]:o[9]]


def _even_proj(h, g_pre, w_in, g_q, w_uq, g_kv, w_ukv, w_gate, b_gate, cos_t, sin_t):
    s = h.shape[0]
    tm = min(ROW_TILE, s)
    width = MLA_HEADS * HEAD_PAD
    gdk = GLA_HEADS * GLA_DK
    gdv = GLA_HEADS * GLA_DV
    outs = [(width, BF16), (width, BF16), (MLA_HEADS * MLA_V, BF16), (gdk, F32), (gdk, F32), (gdv, F32), (gdk, F32),
            (gdv, F32)]
    return pl.pallas_call(
        _even_proj_body,
        grid=(s // tm,),
        in_specs=[_rows(tm, D_MODEL), _resident(g_pre.shape), _resident(w_in.shape), _resident(g_q.shape),
                  _resident(w_uq.shape), _resident(g_kv.shape), _resident(w_ukv.shape), _resident(w_gate.shape),
                  _resident(b_gate.shape), _rows(tm, HEAD_PAD), _rows(tm, HEAD_PAD)],
        out_specs=[_rows(tm, w) for w, _ in outs],
        out_shape=[jax.ShapeDtypeStruct((s, w), dt) for w, dt in outs],
        compiler_params=_cparams(("parallel",)),
        name="even_proj",
    )(h, g_pre, w_in, g_q, w_uq, g_kv, w_ukv, w_gate, b_gate, cos_t, sin_t)


def _prep_even_weights(w_in, w_uq, w_ukv, w_gate, b_gate):
    d = w_in.shape[0]
    off = np.cumsum((0,) + EVEN_SPLITS)
    seg = [w_in[:, off[i]:off[i + 1]] for i in range(len(EVEN_SPLITS))]
    c_q, c_kv, k_pe, gq, gk, gv, glr, r = seg
    half = MLA_ROPE // 2
    z = lambda n: jnp.zeros((d, n), F32)
    pad = HEAD_PAD - MLA_NOPE - MLA_ROPE
    kpe_a = jnp.concatenate([z(MLA_NOPE), k_pe, z(pad)], axis=1)
    kpe_b = jnp.concatenate([z(MLA_NOPE), -k_pe[:, half:], k_pe[:, :half], z(pad)], axis=1)
    glr_p = jnp.concatenate([glr, z(HEAD_PAD - GLA_GATE_RANK)], axis=1)
    w_in_p = jnp.concatenate([c_q, c_kv, kpe_a, kpe_b, gq, gk, gv, glr_p, r], axis=1).astype(BF16)

    nh = MLA_HEADS
    wq3 = w_uq.reshape(MLA_Q_RANK, nh, MLA_NOPE + MLA_ROPE)
    nope, rope = wq3[..., :MLA_NOPE], wq3[..., MLA_NOPE:]
    zq = lambda n: jnp.zeros((MLA_Q_RANK, nh, n), F32)
    wa = jnp.concatenate([nope, rope, zq(pad)], axis=-1).reshape(MLA_Q_RANK, nh * HEAD_PAD)
    wb = jnp.concatenate([zq(MLA_NOPE), -rope[..., half:], rope[..., :half], zq(pad)],
                         axis=-1).reshape(MLA_Q_RANK, nh * HEAD_PAD)
    w_uq_p = jnp.concatenate([wa, wb], axis=1).astype(BF16)

    wkv3 = w_ukv.reshape(MLA_KV_RANK, nh, MLA_NOPE + MLA_V)
    zk = jnp.zeros((MLA_KV_RANK, nh, HEAD_PAD - MLA_NOPE), F32)
    wk = jnp.concatenate([wkv3[..., :MLA_NOPE], zk], axis=-1).reshape(MLA_KV_RANK, nh * HEAD_PAD)
    wv = wkv3[..., MLA_NOPE:].reshape(MLA_KV_RANK, nh * MLA_V)
    w_ukv_p = jnp.concatenate([wk, wv], axis=1).astype(BF16)

    w_gate_p = jnp.concatenate([w_gate, jnp.zeros((HEAD_PAD - GLA_GATE_RANK, w_gate.shape[1]), F32)],
                               axis=0).astype(BF16)
    return w_in_p, w_uq_p, w_ukv_p, w_gate_p, b_gate.reshape(1, -1)


def _mla_attn_body(q_ref, k_ref, vt_ref, o_ref, acc_scr, *, tq, tk, hg):
    i = pl.program_id(1)
    acc_scr[...] = jnp.zeros(acc_scr.shape, F32)

    def block(j, carry, masked):
        start = pl.multiple_of(j * tk, tk)
        sts = []
        for hd in range(hg):
            lanes = slice(hd * HEAD_PAD, (hd + 1) * HEAD_PAD)
            sts.append(_dot_nt(k_ref[pl.ds(start, tk), lanes], q_ref[:, lanes]))
        out = []
        for hd in range(hg):
            m_prev, l_prev = carry[hd]
            st = sts[hd]
            if masked:
                key = lax.broadcasted_iota(jnp.int32, st.shape, 0)
                qry = lax.broadcasted_iota(jnp.int32, st.shape, 1)
                st = jnp.where(key <= qry, st, NEG_BIG)
            m_next = jnp.maximum(m_prev, jnp.max(st, axis=0, keepdims=True))
            p = jnp.exp2(st - m_next)
            alpha = jnp.exp2(m_prev - m_next)
            l_next = alpha * l_prev + jnp.sum(p, axis=0, keepdims=True)
            acc_scr[hd] = alpha * acc_scr[hd] + _dot(vt_ref[hd, j], p.astype(BF16))
            out.append((m_next, l_next))
        return tuple(out)

    init = tuple((jnp.full((1, tq), NEG_BIG, F32), jnp.zeros((1, tq), F32)) for _ in range(hg))
    carry = lax.fori_loop(0, i, lambda j, c: block(j, c, False), init)
    carry = block(i, carry, True)
    for hd in range(hg):
        o_ref[hd] = (acc_scr[hd] * (1.0 / carry[hd][1])).astype(o_ref.dtype)


def _mla_attn(q, k, v):
    s = q.shape[0]
    tq = min(ATTN_TQ, s)
    tk = tq
    nk = s // tk
    hg = ATTN_HEADS_PER_STEP
    vt = v.reshape(nk, tk, MLA_HEADS, MLA_V).transpose(2, 0, 3, 1)
    out = pl.pallas_call(
        functools.partial(_mla_attn_body, tq=tq, tk=tk, hg=hg),
        grid=(MLA_HEADS // hg, s // tq),
        in_specs=[pl.BlockSpec((tq, hg * HEAD_PAD), lambda g, i: (i, g)),
                  pl.BlockSpec((s, hg * HEAD_PAD), lambda g, i: (0, g)),
                  pl.BlockSpec((hg, nk, MLA_V, tk), lambda g, i: (g, 0, 0, 0))],
        out_specs=pl.BlockSpec((hg, MLA_V, tq), lambda g, i: (g, 0, i)),
        out_shape=jax.ShapeDtypeStruct((MLA_HEADS, MLA_V, s), BF16),
        scratch_shapes=[pltpu.VMEM((hg, MLA_V, tq), F32)],
        compiler_params=_cparams(("parallel", "arbitrary")),
        name="mla_attn",
    )(q, k, vt)
    return out.reshape(MLA_HEADS * MLA_V, s)


def _gla_levels(chunk):
    n = 0
    while (1 << n) < chunk:
        n += 1
    return n


def _gla_constants(chunk):
    t = np.arange(chunk)
    mats = [t[None, :] <= t[:, None]]
    m_list, n_list, masks = [], [], []
    b = 1
    while b < chunk:
        blk = t // (2 * b)
        upper = (t % (2 * b)) >= b
        e = blk * 2 * b + b - 1
        m_list.append(upper[:, None] & (t[None, :] > e[:, None]) & (t[None, :] <= t[:, None]))
        n_list.append((~upper)[:, None] & (t[None, :] > t[:, None]) & (t[None, :] <= e[:, None]))
        masks.append(upper[:, None] & (~upper)[None, :] & (blk[:, None] == blk[None, :]))
        b *= 2
    masks.append(t[:, None] == t[None, :])
    mats = np.concatenate(mats + m_list + n_list, axis=0).astype(np.float32)
    masks = np.stack(masks).astype(np.float32)
    masks = np.tile(masks, (1, GLA_HEADS, 1))
    return jnp.asarray(mats, BF16), jnp.asarray(masks, F32)


def _gla_body(q_ref, k_ref, v_ref, g_ref, mats_ref, masks_ref, o_ref, st_ref, *, chunk):
    nlev = _gla_levels(chunk)
    nh, dk, dv = GLA_HEADS, GLA_DK, GLA_DV

    @pl.when(pl.program_id(0) == 0)
    def _():
        st_ref[...] = jnp.zeros(st_ref.shape, F32)

    q = q_ref[...] * (dk ** -0.5)
    k = k_ref[...]
    g = g_ref[...]
    w = nh * dk
    e_all = _dot(mats_ref[...], jnp.concatenate(_split3(g), axis=1))
    e_all = e_all[:, :w] + e_all[:, w:2 * w] + e_all[:, 2 * w:]
    b = e_all[:chunk]
    b_end = b[chunk - 1:chunk]

    head_of_lane = lax.broadcasted_iota(jnp.int32, (chunk, w), 1) // dk

    def stack_heads(x):
        return jnp.concatenate([jnp.where(head_of_lane == hd, x, 0.0) for hd in range(nh)], axis=0).astype(BF16)

    attn = _dot_nt(stack_heads(q), k.astype(BF16)) * masks_ref[nlev]
    for lv in range(nlev):
        ql = q * jnp.exp(e_all[(1 + lv) * chunk:(2 + lv) * chunk])
        kl = k * jnp.exp(e_all[(1 + nlev + lv) * chunk:(2 + nlev + lv) * chunk])
        attn = attn + _dot_nt(stack_heads(ql), kl.astype(BF16)) * masks_ref[lv]
    attn = attn.astype(BF16)

    st = st_ref[...]
    inter = _dot_nt(stack_heads(q * jnp.exp(b)), st.astype(BF16))
    kg = (k * jnp.exp(b_end - b)).astype(BF16)
    head_of_state_lane = lax.broadcasted_iota(jnp.int32, st.shape, 1) // dk
    new = st * jnp.exp(b_end)
    for hd in range(nh):
        vh = v_ref[:, hd * dv:(hd + 1) * dv]
        intra = _dot(attn[hd * chunk:(hd + 1) * chunk], vh.astype(BF16))
        o_ref[:, hd * dv:(hd + 1) * dv] = intra + inter[hd * chunk:(hd + 1) * chunk]
        upd = _dot_tn(vh.astype(BF16), kg)
        new = new + jnp.where(head_of_state_lane == hd, upd, 0.0)
    st_ref[...] = new


def _gla(gq, gk, gv, log_a):
    s = gq.shape[0]
    chunk = min(SCAN_CHUNK, s)
    mats, masks = _gla_constants(chunk)
    gdk = GLA_HEADS * GLA_DK
    gdv = GLA_HEADS * GLA_DV
    return pl.pallas_call(
        functools.partial(_gla_body, chunk=chunk),
        grid=(s // chunk,),
        in_specs=[_rows(chunk, gdk), _rows(chunk, gdk), _rows(chunk, gdv), _rows(chunk, gdk),
                  _resident(mats.shape), _resident(masks.shape)],
        out_specs=_rows(chunk, gdv),
        out_shape=jax.ShapeDtypeStruct((s, gdv), F32),
        scratch_shapes=[pltpu.VMEM((GLA_DV, gdk), F32)],
        compiler_params=_cparams(("arbitrary",)),
        name="gla_scan",
    )(gq, gk, gv, log_a, mats, masks)


def _even_out_body(h_ref, at_ref, o_ref, r_ref, ggla_ref, wa_ref, wg_ref, gpost_ref, out_ref):
    og = o_ref[...]
    gn = jnp.concatenate([_rms(og[:, hd * GLA_DV:(hd + 1) * GLA_DV], ggla_ref[...]) for hd in range(GLA_HEADS)],
                         axis=1)
    g = (gn * _silu(r_ref[...])).astype(BF16)
    mix = _dot_tn(at_ref[...], wa_ref[...]) + _dot(g, wg_ref[...])
    out_ref[...] = h_ref[...] + _rms(mix, gpost_ref[...])


def _even_out(h, a_t, o_gla, r, g_gla, w_a, w_g, g_post):
    s = h.shape[0]
    tm = min(ROW_TILE, s)
    return pl.pallas_call(
        _even_out_body,
        grid=(s // tm,),
        in_specs=[_rows(tm, D_MODEL), pl.BlockSpec((a_t.shape[0], tm), lambda i: (0, i)),
                  _rows(tm, o_gla.shape[1]), _rows(tm, r.shape[1]),
                  _resident(g_gla.shape), _resident(w_a.shape), _resident(w_g.shape), _resident(g_post.shape)],
        out_specs=_rows(tm, D_MODEL),
        out_shape=jax.ShapeDtypeStruct((s, D_MODEL), F32),
        compiler_params=_cparams(("parallel",)),
        name="even_out",
    )(h, a_t, o_gla, r, g_gla, w_a, w_g, g_post)


CONV_HALO = 8


def _odd_proj_body(h_ref, gpre_ref, win_ref, cw_ref, cb_ref, wq_ref, wk_ref, wv_ref, wgq_ref, wgk_ref, wgv_ref,
                   bg_ref, q_out, k_out, v_out, gates_out, xc_out, z_out, xbuf):
    tm = h_ref.shape[0]
    width = D_MODEL

    @pl.when(pl.program_id(0) == 0)
    def _():
        xbuf[0:CONV_HALO, :] = jnp.zeros((CONV_HALO, width), F32)

    hn = _rms(h_ref[...], gpre_ref[...]).astype(BF16)
    xz = _dot(hn, win_ref[...])
    x_m = xz[:, :width]
    z_out[...] = xz[:, width:]
    xbuf[CONV_HALO:CONV_HALO + tm, :] = x_m
    conv = cb_ref[...] + cw_ref[MLSTM_CONV - 1:MLSTM_CONV, :] * x_m
    for j in range(MLSTM_CONV - 1):
        back = MLSTM_CONV - 1 - j
        conv = conv + cw_ref[j:j + 1, :] * xbuf[CONV_HALO - back:CONV_HALO - back + tm, :]
    xbuf[0:CONV_HALO, :] = x_m[tm - CONV_HALO:, :]
    x_c = _silu(conv)
    xc_out[...] = x_c

    dh = MLSTM_DH
    xcb = x_c.astype(BF16)
    xmb = x_m.astype(BF16)
    qs, ks, vs = [], [], []
    for hd in range(MLSTM_HEADS):
        sl = slice(hd * dh, (hd + 1) * dh)
        qs.append(_dot(xcb[:, sl], wq_ref[hd]))
        ks.append(_dot(xcb[:, sl], wk_ref[hd]))
        vs.append(_dot(xmb[:, sl], wv_ref[hd]))
    q = jnp.concatenate(qs, axis=1).astype(BF16)
    k = jnp.concatenate(ks, axis=1).astype(BF16)
    v = jnp.concatenate(vs, axis=1).astype(BF16)
    gates_out[...] = _dot(q, wgq_ref[...]) + _dot(k, wgk_ref[...]) + _dot(v, wgv_ref[...]) + bg_ref[...]
    q_out[...] = (q.astype(F32) * (dh ** -0.5)).astype(BF16)
    k_out[...] = k
    v_out[...] = v


def _odd_proj(h, g_pre, w_in, conv_w, conv_b, w_q, w_k, w_v, wg_q, wg_k, wg_v, b_g):
    s = h.shape[0]
    tm = min(ROW_TILE, s)
    outs = [(D_MODEL, BF16), (D_MODEL, BF16), (D_MODEL, BF16), (V7X_LANES, F32), (D_MODEL, F32), (D_MODEL, F32)]
    ins = (g_pre, w_in, conv_w, conv_b, w_q, w_k, w_v, wg_q, wg_k, wg_v, b_g)
    return pl.pallas_call(
        _odd_proj_body,
        grid=(s // tm,),
        in_specs=[_rows(tm, D_MODEL)] + [_resident(a.shape) for a in ins],
        out_specs=[_rows(tm, w) for w, _ in outs],
        out_shape=[jax.ShapeDtypeStruct((s, w), dt) for w, dt in outs],
        scratch_shapes=[pltpu.VMEM((CONV_HALO + tm, D_MODEL), F32)],
        compiler_params=_cparams(("arbitrary",)),
        name="odd_proj",
    )(h, *ins)


def _prep_odd_gate_weights(w_gates, b_gates):
    nh, dh = MLSTM_HEADS, MLSTM_DH
    w4 = w_gates.reshape(nh, 3, dh, 2 * nh)
    pad = jnp.zeros((nh * dh, V7X_LANES - 2 * nh), F32)
    parts = [jnp.concatenate([w4[:, j].reshape(nh * dh, 2 * nh), pad], axis=1).astype(BF16) for j in range(3)]
    b = jnp.concatenate([b_gates, jnp.zeros((V7X_LANES - 2 * nh,), F32)]).reshape(1, V7X_LANES)
    return parts[0], parts[1], parts[2], b


def _mlstm_body(q_ref, k_ref, v_ref, gcol_ref, grow_ref, tri_ref, trit_ref, o_ref, c_scr, n_scr, m_scr, *, chunk):
    nh, dh = MLSTM_HEADS, MLSTM_DH

    @pl.when(pl.program_id(0) == 0)
    def _():
        c_scr[...] = jnp.zeros(c_scr.shape, F32)
        n_scr[...] = jnp.zeros(n_scr.shape, F32)
        m_scr[...] = jnp.zeros(m_scr.shape, F32)

    gc = gcol_ref[...]
    gr = grow_ref[0]
    lanes = gc.shape[1]
    bc_all = _dot(tri_ref[...], jnp.concatenate(_split3(_log_sigmoid(gc)), axis=1))
    bc_all = bc_all[:, :lanes] + bc_all[:, lanes:2 * lanes] + bc_all[:, 2 * lanes:]
    nr = gr.shape[0]
    br_all = _dot(jnp.concatenate(_split3(_log_sigmoid(gr)), axis=0), trit_ref[...])
    br_all = br_all[:nr] + br_all[nr:2 * nr] + br_all[2 * nr:]

    row = lax.broadcasted_iota(jnp.int32, (chunk, chunk), 0)
    col = lax.broadcasted_iota(jnp.int32, (chunk, chunk), 1)
    causal = col <= row
    for hd in range(nh):
        sl = slice(hd * dh, (hd + 1) * dh)
        b_c = bc_all[:, nh + hd:nh + hd + 1]
        i_c = gc[:, hd:hd + 1]
        b_r = br_all[nh + hd:nh + hd + 1, :]
        i_r = gr[hd:hd + 1, :]
        m_old = m_scr[hd:hd + 1, 0:1]
        qh = q_ref[:, sl]
        kh = k_ref[:, sl]
        vh = v_ref[:, sl]

        log_d = jnp.where(causal, b_c - b_r + i_r, NEG_BIG)
        m_inter = b_c + m_old
        m_t = jnp.maximum(m_inter, jnp.max(log_d, axis=-1, keepdims=True))
        w_intra = jnp.exp(log_d - m_t) * _dot_nt(qh, kh)
        w_inter = jnp.exp(m_inter - m_t)
        c_old = c_scr[hd]
        n_old = n_scr[hd:hd + 1, :]
        num = _dot(w_intra.astype(BF16), vh) + w_inter * _dot(qh, c_old.astype(BF16))
        den = (jnp.sum(w_intra, axis=-1, keepdims=True)
               + w_inter * jnp.sum(qh.astype(F32) * n_old, axis=-1, keepdims=True))
        o_ref[:, sl] = num / jnp.maximum(jnp.abs(den), jnp.exp(-m_t))

        b_end = b_c[chunk - 1:chunk, :]
        log_w = b_end - b_c + i_c
        m_new = jnp.maximum(b_end + m_old, jnp.max(log_w, axis=0, keepdims=True))
        w_s = jnp.exp(log_w - m_new)
        decay = jnp.exp(b_end + m_old - m_new)
        kw = kh.astype(F32) * w_s
        c_scr[hd] = decay * c_old + _dot_tn(kw.astype(BF16), vh)
        n_scr[hd:hd + 1, :] = decay * n_old + jnp.sum(kw, axis=0, keepdims=True)
        m_scr[hd:hd + 1, :] = jnp.broadcast_to(m_new, (1, m_scr.shape[1]))


def _mlstm(q, k, v, gates):
    s = q.shape[0]
    chunk = min(SCAN_CHUNK, s)
    nc = s // chunk
    ng = 2 * MLSTM_HEADS
    grow = gates[:, :ng].reshape(nc, chunk, ng).transpose(0, 2, 1)
    t = np.arange(chunk)
    tri = (t[None, :] <= t[:, None]).astype(np.float32)
    return pl.pallas_call(
        functools.partial(_mlstm_body, chunk=chunk),
        grid=(nc,),
        in_specs=[_rows(chunk, D_MODEL), _rows(chunk, D_MODEL), _rows(chunk, D_MODEL), _rows(chunk, V7X_LANES),
                  pl.BlockSpec((1, ng, chunk), lambda c: (c, 0, 0)),
                  _resident((chunk, chunk)), _resident((chunk, chunk))],
        out_specs=_rows(chunk, D_MODEL),
        out_shape=jax.ShapeDtypeStruct((s, D_MODEL), F32),
        scratch_shapes=[pltpu.VMEM((MLSTM_HEADS, MLSTM_DH, MLSTM_DH), F32),
                        pltpu.VMEM((8, MLSTM_DH), F32),
                        pltpu.VMEM((8, V7X_LANES), F32)],
        compiler_params=_cparams(("arbitrary",)),
        name="mlstm_scan",
    )(q, k, v, gates, grow, jnp.asarray(tri, BF16), jnp.asarray(tri.T, BF16))


def _odd_out_body(h_ref, hc_ref, xc_ref, z_ref, gh_ref, skip_ref, wo_ref, gpost_ref, out_ref):
    hc = hc_ref[...]
    dh = MLSTM_DH
    parts = []
    for hd in range(MLSTM_HEADS):
        seg = hc[:, hd * dh:(hd + 1) * dh]
        cen = seg - jnp.mean(seg, axis=-1, keepdims=True)
        parts.append(cen * lax.rsqrt(jnp.mean(cen * cen, axis=-1, keepdims=True) + EPS))
    hn = jnp.concatenate(parts, axis=1) * gh_ref[...]
    out = ((hn + skip_ref[...] * xc_ref[...]) * _silu(z_ref[...])).astype(BF16)
    out_ref[...] = h_ref[...] + _rms(_dot(out, wo_ref[...]), gpost_ref[...])


def _odd_out(h, hcell, x_c, z, g_hnorm, skip, w_out, g_post):
    s = h.shape[0]
    tm = min(ROW_TILE, s)
    return pl.pallas_call(
        _odd_out_body,
        grid=(s // tm,),
        in_specs=[_rows(tm, D_MODEL)] * 4 + [_resident(g_hnorm.shape), _resident(skip.shape),
                                              _resident(w_out.shape), _resident(g_post.shape)],
        out_specs=_rows(tm, D_MODEL),
        out_shape=jax.ShapeDtypeStruct((s, D_MODEL), F32),
        compiler_params=_cparams(("parallel",)),
        name="odd_out",
    )(h, hcell, x_c, z, g_hnorm, skip, w_out, g_post)


def _xattn_body(h_ref, mem_ref, gmem_ref, gpre_ref, wq_ref, wk_ref, wv_ref, wo_ref, gpost_ref, out_ref,
                k_scr, v_scr):
    @pl.when(pl.program_id(0) == 0)
    def _():
        mem_n = _rms(mem_ref[...], gmem_ref[...]).astype(BF16)
        k_scr[...] = _dot(mem_n, wk_ref[...]).astype(BF16)
        v_scr[...] = _dot(mem_n, wv_ref[...]).astype(BF16)

    h = h_ref[...]
    hn = _rms(h, gpre_ref[...]).astype(BF16)
    q = _dot(hn, wq_ref[...]).astype(BF16)
    dh = XATTN_DH
    scale = dh ** -0.5
    outs = []
    for hd in range(XATTN_HEADS):
        sl = slice(hd * dh, (hd + 1) * dh)
        s = _dot_nt(q[:, sl], k_scr[:, sl]) * scale
        p = jnp.exp(s - jnp.max(s, axis=-1, keepdims=True))
        inv_l = 1.0 / jnp.sum(p, axis=-1, keepdims=True)
        outs.append((_dot(p.astype(BF16), v_scr[:, sl]) * inv_l).astype(BF16))
    o = jnp.concatenate(outs, axis=1)
    out_ref[...] = h + _rms(_dot(o, wo_ref[...]), gpost_ref[...])


def _xattn(h, mem, g_mem, g_pre, w_q, w_k, w_v, w_o, g_post):
    s = h.shape[0]
    tm = min(ROW_TILE, s)
    ins = (mem, g_mem, g_pre, w_q, w_k, w_v, w_o, g_post)
    return pl.pallas_call(
        _xattn_body,
        grid=(s // tm,),
        in_specs=[_rows(tm, D_MODEL)] + [_resident(a.shape) for a in ins],
        out_specs=_rows(tm, D_MODEL),
        out_shape=jax.ShapeDtypeStruct((s, D_MODEL), F32),
        scratch_shapes=[pltpu.VMEM((mem.shape[0], D_MODEL), BF16), pltpu.VMEM((mem.shape[0], D_MODEL), BF16)],
        compiler_params=_cparams(("arbitrary",)),
        name="mem_xattn",
    )(h, *ins)


def _ffn_body(h_ref, gpre_ref, w1_ref, w2_ref, gpost_ref, out_ref):
    h = h_ref[...]
    xn = _rms(h, gpre_ref[...]).astype(BF16)
    acc = None
    for c in range(D_FF // FF_CHUNK):
        sl = slice(c * FF_CHUNK, (c + 1) * FF_CHUNK)
        a = jnp.square(jnp.maximum(_dot(xn, w1_ref[:, sl]), 0.0)).astype(BF16)
        part = _dot(a, w2_ref[sl, :])
        acc = part if acc is None else acc + part
    out_ref[...] = h + _rms(acc, gpost_ref[...])


def _ffn(h, g_pre, w1, w2, g_post):
    s = h.shape[0]
    tm = min(ROW_TILE, s)
    return pl.pallas_call(
        _ffn_body,
        grid=(s // tm,),
        in_specs=[_rows(tm, D_MODEL), _resident(g_pre.shape), _resident(w1.shape), _resident(w2.shape),
                  _resident(g_post.shape)],
        out_specs=_rows(tm, D_MODEL),
        out_shape=jax.ShapeDtypeStruct((s, D_MODEL), F32),
        compiler_params=_cparams(("parallel",)),
        name="sq_relu_mlp",
    )(h, g_pre, w1, w2, g_post)


def _row(g):
    return g.reshape(1, -1)


def kernel(x, mem, positions, g_mix_pre, g_mix_post, g_xattn_pre, g_xattn_post, g_mem, g_ffn_pre, g_ffn_post, ev_w_in, ev_g_q, ev_w_uq, ev_g_kv, ev_w_ukv, ev_w_gate, ev_b_gate, ev_g_gla, ev_w_out, od_w_in, od_conv_w, od_conv_b, od_w_q, od_w_k, od_w_v, od_w_gates, od_b_gates, od_g_hnorm, od_skip, od_w_out, xa_w_q, xa_w_k, xa_w_v, xa_w_o, ffn_w1, ffn_w2):
    batch, seq, d = x.shape
    assert batch == 1 and d == D_MODEL and seq % ROW_TILE == 0 and seq % ATTN_TQ == 0 and seq % SCAN_CHUNK == 0
    h = x.reshape(seq, d)
    mem2 = mem.reshape(mem.shape[1], d)
    cos_t, sin_t = _rope_tables(positions)
    n_a = MLA_HEADS * MLA_V

    for layer in range(DEPTH):
        j = layer // 2
        if layer % 2 == 0:
            w_in_p, w_uq_p, w_ukv_p, w_gate_p, b_gate = _prep_even_weights(
                ev_w_in[j], ev_w_uq[j], ev_w_ukv[j], ev_w_gate[j], ev_b_gate[j])
            q, k, v, gq, gk, gv, log_a, r = _even_proj(
                h, _row(g_mix_pre[layer]), w_in_p, _row(ev_g_q[j]), w_uq_p, _row(ev_g_kv[j]), w_ukv_p, w_gate_p,
                b_gate, cos_t, sin_t)
            a_t = _mla_attn(q, k, v)
            o_gla = _gla(gq, gk, gv, log_a)
            w_out = ev_w_out[j].astype(BF16)
            h = _even_out(h, a_t, o_gla, r, _row(ev_g_gla[j]), w_out[:n_a], w_out[n_a:], _row(g_mix_post[layer]))
        else:
            wg_q, wg_k, wg_v, b_g = _prep_odd_gate_weights(od_w_gates[j], od_b_gates[j])
            q, k, v, gates, x_c, z = _odd_proj(
                h, _row(g_mix_pre[layer]), od_w_in[j].astype(BF16), od_conv_w[j], _row(od_conv_b[j]),
                od_w_q[j].astype(BF16), od_w_k[j].astype(BF16), od_w_v[j].astype(BF16), wg_q, wg_k, wg_v, b_g)
            hcell = _mlstm(q, k, v, gates)
            h = _odd_out(h, hcell, x_c, z, _row(od_g_hnorm[j]), _row(od_skip[j]), od_w_out[j].astype(BF16),
                         _row(g_mix_post[layer]))
        h = _xattn(h, mem2, _row(g_mem[layer]), _row(g_xattn_pre[layer]), xa_w_q[layer].astype(BF16),
                   xa_w_k[layer].astype(BF16), xa_w_v[layer].astype(BF16), xa_w_o[layer].astype(BF16),
                   _row(g_xattn_post[layer]))
        h = _ffn(h, _row(g_ffn_pre[layer]), ffn_w1[layer].astype(BF16), ffn_w2[layer].astype(BF16),
                 _row(g_ffn_post[layer]))
    return h.reshape(batch, seq, d)
```

```python
import functools

import numpy as np
import jax
import jax.numpy as jnp
from jax import lax
from jax.experimental import pallas as pl
from jax.experimental.pallas import tpu as pltpu

F32 = jnp.float32
BF16 = jnp.bfloat16

D_MODEL = 1024
DEPTH = 4
EPS = 1e-6
MLA_HEADS = 8
MLA_NOPE = 64
MLA_ROPE = 32
MLA_V = 64
MLA_Q_RANK = 256
MLA_KV_RANK = 128
ROPE_THETA = 10000.0
GLA_HEADS = 4
GLA_DK = 64
GLA_DV = 128
GLA_GATE_RANK = 16
GLA_TAU = 16.0
MLSTM_HEADS = 4
MLSTM_DH = 256
MLSTM_CONV = 4
XATTN_HEADS = 4
XATTN_DH = 256
D_FF = 4096
EVEN_SPLITS = (256, 128, 32, 256, 256, 512, 16, 512)

V7X_LANES = 128
V7X_VMEM_BYTES = 64 * 1024 * 1024
VMEM_LIMIT = (V7X_VMEM_BYTES * 3) // 4

ROW_TILE = 512
FF_CHUNK = 1024
ATTN_TK = 512
ATTN_HEADS_PER_STEP = 2
SCAN_CHUNK = 128
GLA_CHUNKS_PER_STEP = 4
HEAD_PAD = V7X_LANES

LOG2E = 1.4426950408889634
NEG_BIG = -1e30


def _cparams(sem):
    return pltpu.CompilerParams(dimension_semantics=sem, vmem_limit_bytes=VMEM_LIMIT)


def _resident(shape):
    nd = len(shape)
    return pl.BlockSpec(shape, lambda *_: (0,) * nd, pipeline_mode=pl.Buffered(1))


def _rows(tile, width):
    return pl.BlockSpec((tile, width), lambda i: (i, 0))


def _rms(x, g):
    return x * lax.rsqrt(jnp.mean(x * x, axis=-1, keepdims=True) + EPS) * g


def _silu(x):
    return x * (1.0 / (1.0 + jnp.exp(-x)))


def _log_sigmoid(x):
    return jnp.minimum(x, 0.0) - jnp.log(1.0 + jnp.exp(-jnp.abs(x)))


def _dot(a, b):
    return jnp.dot(a, b, preferred_element_type=F32)


def _dot_nt(a, b):
    return lax.dot_general(a, b, (((1,), (1,)), ((), ())), preferred_element_type=F32)


def _dot_tn(a, b):
    return lax.dot_general(a, b, (((0,), (0,)), ((), ())), preferred_element_type=F32)


def _split2(x):
    x1 = x.astype(BF16)
    return x1, (x - x1.astype(F32)).astype(BF16)


def _split3(x):
    x1 = x.astype(BF16)
    r = x - x1.astype(F32)
    x2 = r.astype(BF16)
    x3 = (r - x2.astype(F32)).astype(BF16)
    return x1, x2, x3


def _rope_body(pos_ref, freq_ref, cos_ref, sin_ref):
    ang = pos_ref[...] * freq_ref[...]
    lane = lax.broadcasted_iota(jnp.int32, ang.shape, 1)
    on = jnp.logical_and(lane >= MLA_NOPE, lane < MLA_NOPE + MLA_ROPE)
    cos_ref[...] = jnp.where(on, jnp.cos(ang), 0.0)
    sin_ref[...] = jnp.where(on, jnp.sin(ang), 0.0)


def _rope_tables(positions):
    s = positions.shape[-1]
    pos = positions.astype(F32).reshape(s, 1)
    inv_freq = ROPE_THETA ** (-jnp.arange(0, MLA_ROPE, 2, dtype=F32) / MLA_ROPE)
    freq = jnp.concatenate([jnp.zeros((MLA_NOPE,), F32), inv_freq, inv_freq,
                            jnp.zeros((HEAD_PAD - MLA_NOPE - MLA_ROPE,), F32)]).reshape(1, HEAD_PAD)
    tile = min(2048, s)
    return pl.pallas_call(
        _rope_body,
        grid=(s // tile,),
        in_specs=[_rows(tile, 1), _resident((1, HEAD_PAD))],
        out_specs=[_rows(tile, HEAD_PAD), _rows(tile, HEAD_PAD)],
        out_shape=[jax.ShapeDtypeStruct((s, HEAD_PAD), F32)] * 2,
        compiler_params=_cparams(("parallel",)),
        name="rope_tables",
    )(pos, freq)


_EV_OFF = np.cumsum((0, 256, 128, 128, 128, 256, 256, 512, 128, 512))


def _even_proj_body(h_ref, gpre_ref, win_ref, gq_ref, wuq_ref, gkv_ref, wukv_ref, wgate_ref, bgate_ref,
                    cos_ref, sin_ref,
                    q_ref, k_ref, v_ref, gq_out, gk_out, gv_out, la_out, r_out):
    o = _EV_OFF
    hn = _rms(h_ref[...], gpre_ref[...]).astype(BF16)
    proj = _dot(hn, win_ref[...])
    c_q = proj[:, o[0]:o[1]]
    c_kv = proj[:, o[1]:o[2]]
    kpe_a = proj[:, o[2]:o[3]]
    kpe_b = proj[:, o[3]:o[4]]
    cosk = cos_ref[...]
    sink = sin_ref[...]
    nh = MLA_HEADS
    width = nh * HEAD_PAD

    qs = (MLA_NOPE + MLA_ROPE) ** -0.5 * LOG2E
    lane = lax.broadcasted_iota(jnp.int32, cosk.shape, 1)
    cosq = qs * jnp.where(lane < MLA_NOPE, 1.0, cosk)
    sinq = qs * sink
    cqn = _rms(c_q, gq_ref[...]).astype(BF16)
    qab = _dot(cqn, wuq_ref[...])
    q = qab[:, :width] * jnp.tile(cosq, (1, nh)) + qab[:, width:] * jnp.tile(sinq, (1, nh))
    q_ref[...] = q.astype(BF16)

    ckvn = _rms(c_kv, gkv_ref[...]).astype(BF16)
    kv = _dot(ckvn, wukv_ref[...])
    kpe = kpe_a * cosk + kpe_b * sink
    k_ref[...] = (kv[:, :width] + jnp.tile(kpe, (1, nh))).astype(BF16)
    v_ref[...] = kv[:, width:].astype(BF16)

    gq_out[...] = proj[:, o[4]:o[5]]
    gk_out[...] = proj[:, o[5]:o[6]]
    gv_out[...] = proj[:, o[6]:o[7]]
    glr = proj[:, o[7]:o[8]].astype(BF16)
    x = _dot(glr, wgate_ref[...]) + bgate_ref[...]
    la_out[...] = _log_sigmoid(x) * (1.0 / GLA_TAU)
    r_out[...] = proj[:, o[8]:o[9]]


def _even_proj(h, g_pre, w_in, g_q, w_uq, g_kv, w_ukv, w_gate, b_gate, cos_t, sin_t):
    s = h.shape[0]
    tm = min(ROW_TILE, s)
    width = MLA_HEADS * HEAD_PAD
    gdk = GLA_HEADS * GLA_DK
    gdv = GLA_HEADS * GLA_DV
    outs = [(width, BF16), (width, BF16), (MLA_HEADS * MLA_V, BF16), (gdk, F32), (gdk, F32), (gdv, F32), (gdk, F32),
            (gdv, F32)]
    return pl.pallas_call(
        _even_proj_body,
        grid=(s // tm,),
        in_specs=[_rows(tm, D_MODEL), _resident(g_pre.shape), _resident(w_in.shape), _resident(g_q.shape),
                  _resident(w_uq.shape), _resident(g_kv.shape), _resident(w_ukv.shape), _resident(w_gate.shape),
                  _resident(b_gate.shape), _rows(tm, HEAD_PAD), _rows(tm, HEAD_PAD)],
        out_specs=[_rows(tm, w) for w, _ in outs],
        out_shape=[jax.ShapeDtypeStruct((s, w), dt) for w, dt in outs],
        compiler_params=_cparams(("parallel",)),
        name="even_proj",
    )(h, g_pre, w_in, g_q, w_uq, g_kv, w_ukv, w_gate, b_gate, cos_t, sin_t)


def _prep_even_weights(w_in, w_uq, w_ukv, w_gate, b_gate):
    d = w_in.shape[0]
    off = np.cumsum((0,) + EVEN_SPLITS)
    seg = [w_in[:, off[i]:off[i + 1]] for i in range(len(EVEN_SPLITS))]
    c_q, c_kv, k_pe, gq, gk, gv, glr, r = seg
    half = MLA_ROPE // 2
    z = lambda n: jnp.zeros((d, n), F32)
    pad = HEAD_PAD - MLA_NOPE - MLA_ROPE
    kpe_a = jnp.concatenate([z(MLA_NOPE), k_pe, z(pad)], axis=1)
    kpe_b = jnp.concatenate([z(MLA_NOPE), -k_pe[:, half:], k_pe[:, :half], z(pad)], axis=1)
    glr_p = jnp.concatenate([glr, z(HEAD_PAD - GLA_GATE_RANK)], axis=1)
    w_in_p = jnp.concatenate([c_q, c_kv, kpe_a, kpe_b, gq, gk, gv, glr_p, r], axis=1).astype(BF16)

    nh = MLA_HEADS
    wq3 = w_uq.reshape(MLA_Q_RANK, nh, MLA_NOPE + MLA_ROPE)
    nope, rope = wq3[..., :MLA_NOPE], wq3[..., MLA_NOPE:]
    zq = lambda n: jnp.zeros((MLA_Q_RANK, nh, n), F32)
    wa = jnp.concatenate([nope, rope, zq(pad)], axis=-1).reshape(MLA_Q_RANK, nh * HEAD_PAD)
    wb = jnp.concatenate([zq(MLA_NOPE), -rope[..., half:], rope[..., :half], zq(pad)],
                         axis=-1).reshape(MLA_Q_RANK, nh * HEAD_PAD)
    w_uq_p = jnp.concatenate([wa, wb], axis=1).astype(BF16)

    wkv3 = w_ukv.reshape(MLA_KV_RANK, nh, MLA_NOPE + MLA_V)
    zk = jnp.zeros((MLA_KV_RANK, nh, HEAD_PAD - MLA_NOPE), F32)
    wk = jnp.concatenate([wkv3[..., :MLA_NOPE], zk], axis=-1).reshape(MLA_KV_RANK, nh * HEAD_PAD)
    wv = wkv3[..., MLA_NOPE:].reshape(MLA_KV_RANK, nh * MLA_V)
    w_ukv_p = jnp.concatenate([wk, wv], axis=1).astype(BF16)

    w_gate_p = jnp.concatenate([w_gate, jnp.zeros((HEAD_PAD - GLA_GATE_RANK, w_gate.shape[1]), F32)],
                               axis=0).astype(BF16)
    return w_in_p, w_uq_p, w_ukv_p, w_gate_p, b_gate.reshape(1, -1)


ATTN_VROWS = MLA_V + 16


def _mla_attn_body(q_ref, k_ref, vt_ref, o_ref, acc_scr, st_scr, p_scr, m_scr, alpha_scr, *, tq, tk, hg):
    i = pl.program_id(1)
    heads = range(hg)

    def scores(t, slot):
        start = pl.multiple_of(t * tk, tk)
        for hd in heads:
            lanes = slice(hd * HEAD_PAD, (hd + 1) * HEAD_PAD)
            st_scr[slot, hd] = _dot_nt(k_ref[pl.ds(start, tk), lanes], q_ref[:, lanes])

    def accumulate(t, slot):
        for hd in heads:
            acc_scr[hd] = alpha_scr[slot, hd] * acc_scr[hd] + _dot(vt_ref[hd, t], p_scr[slot, hd])

    def softmax(slot, mask_shift):
        for hd in heads:
            st = st_scr[slot, hd]
            if mask_shift is not None:
                key = lax.broadcasted_iota(jnp.int32, st.shape, 0) + mask_shift
                qry = lax.broadcasted_iota(jnp.int32, st.shape, 1)
                st = jnp.where(key <= qry, st, NEG_BIG)
            m_prev = m_scr[hd]
            m_next = jnp.maximum(m_prev, jnp.max(st, axis=0, keepdims=True))
            p_scr[slot, hd] = jnp.exp2(st - m_next).astype(BF16)
            alpha_scr[slot, hd] = jnp.exp2(m_prev - m_next)
            m_scr[hd] = m_next

    def step(t, slot, mask_shift, more):
        softmax(slot, mask_shift)
        if more:
            scores(t + 1, 1 - slot)
        accumulate(jnp.maximum(t - 1, 0), 1 - slot)

    acc_scr[...] = jnp.zeros(acc_scr.shape, F32)
    p_scr[1] = jnp.zeros(p_scr.shape[1:], BF16)
    alpha_scr[1] = jnp.ones(alpha_scr.shape[1:], F32)
    m_scr[...] = jnp.full(m_scr.shape, NEG_BIG, F32)
    scores(0, 0)

    def pair(u, c):
        step(2 * u, 0, None, True)
        step(2 * u + 1, 1, None, True)
        return c

    lax.fori_loop(0, i, pair, 0)
    step(2 * i, 0, 0, True)
    step(2 * i + 1, 1, tk, False)
    accumulate(2 * i + 1, 1)
    for hd in heads:
        acc = acc_scr[hd]
        o_ref[hd] = (acc[:MLA_V] * (1.0 / acc[MLA_V:MLA_V + 1])).astype(o_ref.dtype)


def _mla_attn(q, k, v):
    s = q.shape[0]
    tk = min(ATTN_TK, s // 2)
    tq = 2 * tk
    nk = s // tk
    hg = ATTN_HEADS_PER_STEP
    vt = v.reshape(nk, tk, MLA_HEADS, MLA_V).transpose(2, 0, 3, 1)
    extra = jnp.zeros((MLA_HEADS, nk, ATTN_VROWS - MLA_V, tk), BF16).at[:, :, 0, :].set(1.0)
    vt = jnp.concatenate([vt, extra], axis=2)
    out = pl.pallas_call(
        functools.partial(_mla_attn_body, tq=tq, tk=tk, hg=hg),
        grid=(MLA_HEADS // hg, s // tq),
        in_specs=[pl.BlockSpec((tq, hg * HEAD_PAD), lambda g, i: (i, g)),
                  pl.BlockSpec((s, hg * HEAD_PAD), lambda g, i: (0, g), pipeline_mode=pl.Buffered(1)),
                  pl.BlockSpec((hg, nk, ATTN_VROWS, tk), lambda g, i: (g, 0, 0, 0), pipeline_mode=pl.Buffered(1))],
        out_specs=pl.BlockSpec((hg, MLA_V, tq), lambda g, i: (g, 0, i)),
        out_shape=jax.ShapeDtypeStruct((MLA_HEADS, MLA_V, s), BF16),
        scratch_shapes=[pltpu.VMEM((hg, ATTN_VROWS, tq), F32),
                        pltpu.VMEM((2, hg, tk, tq), F32),
                        pltpu.VMEM((2, hg, tk, tq), BF16),
                        pltpu.VMEM((hg, 1, tq), F32),
                        pltpu.VMEM((2, hg, 1, tq), F32)],
        compiler_params=_cparams(("parallel", "arbitrary")),
        name="mla_attn",
    )(q, k, vt)
    return out.reshape(MLA_HEADS * MLA_V, s)


def _gla_levels(chunk):
    n = 0
    while (1 << n) < chunk:
        n += 1
    return n


def _gla_constants(chunk):
    t = np.arange(chunk)
    mats = [t[None, :] <= t[:, None]]
    masks = []
    b = 1
    while b < chunk:
        blk = t // (2 * b)
        upper = (t % (2 * b)) >= b
        e = blk * 2 * b + b - 1
        up_rows = upper[:, None] & (t[None, :] > e[:, None]) & (t[None, :] <= t[:, None])
        lo_rows = (~upper)[:, None] & (t[None, :] > t[:, None]) & (t[None, :] <= e[:, None])
        mats.append(up_rows | lo_rows)
        masks.append(upper[:, None] & (~upper)[None, :] & (blk[:, None] == blk[None, :]))
        b *= 2
    masks.append(t[:, None] == t[None, :])
    mats = np.concatenate(mats, axis=0).astype(np.float32)
    masks = np.stack(masks).astype(np.float32)
    masks = np.tile(masks, (1, GLA_HEADS, 1))
    return jnp.asarray(mats, BF16), jnp.asarray(masks, F32)


def _gla_body(q_ref, k_ref, v_ref, g_ref, mats_ref, masks_ref, o_ref, st_ref, *, chunk, n_sub):
    nlev = _gla_levels(chunk)
    nh, dk, dv = GLA_HEADS, GLA_DK, GLA_DV
    w = nh * dk

    @pl.when(pl.program_id(0) == 0)
    def _():
        st_ref[...] = jnp.zeros(st_ref.shape, F32)

    head_of_lane = lax.broadcasted_iota(jnp.int32, (chunk, w), 1) // dk

    def stack_heads(x):
        return jnp.concatenate([jnp.where(head_of_lane == hd, x, 0.0) for hd in range(nh)], axis=0).astype(BF16)

    pending = []
    for c in range(n_sub):
        rows = slice(c * chunk, (c + 1) * chunk)
        q = q_ref[rows, :] * (dk ** -0.5)
        k = k_ref[rows, :]
        g = g_ref[rows, :]
        e_all = _dot(mats_ref[...], jnp.concatenate(_split2(g), axis=1))
        e_all = e_all[:, :w] + e_all[:, w:]
        b = e_all[:chunk]
        b_end = b[chunk - 1:chunk]
        attn = _dot_nt(stack_heads(q), k.astype(BF16)) * masks_ref[nlev]
        for lv in range(nlev):
            dec = jnp.exp(e_all[(1 + lv) * chunk:(2 + lv) * chunk])
            attn = attn + _dot_nt(stack_heads(q * dec), (k * dec).astype(BF16)) * masks_ref[lv]
        attn = attn.astype(BF16)
        for hd in range(nh):
            vh = v_ref[rows, hd * dv:(hd + 1) * dv].astype(BF16)
            o_ref[rows, hd * dv:(hd + 1) * dv] = _dot(attn[hd * chunk:(hd + 1) * chunk], vh)
        pending.append((stack_heads(q * jnp.exp(b)), (k * jnp.exp(b_end - b)).astype(BF16), jnp.exp(b_end)))

    st = st_ref[...]
    head_of_state_lane = lax.broadcasted_iota(jnp.int32, st.shape, 1) // dk
    for c in range(n_sub):
        rows = slice(c * chunk, (c + 1) * chunk)
        qg, kg, dec_end = pending[c]
        inter = _dot_nt(qg, st.astype(BF16))
        new = st * dec_end
        for hd in range(nh):
            cols = slice(hd * dv, (hd + 1) * dv)
            o_ref[rows, cols] = o_ref[rows, cols] + inter[hd * chunk:(hd + 1) * chunk]
            upd = _dot_tn(v_ref[rows, cols].astype(BF16), kg)
            new = new + jnp.where(head_of_state_lane == hd, upd, 0.0)
        st = new
    st_ref[...] = st


def _gla(gq, gk, gv, log_a):
    s = gq.shape[0]
    chunk = min(SCAN_CHUNK, s)
    n_sub = min(GLA_CHUNKS_PER_STEP, s // chunk)
    rows = chunk * n_sub
    mats, masks = _gla_constants(chunk)
    gdk = GLA_HEADS * GLA_DK
    gdv = GLA_HEADS * GLA_DV
    return pl.pallas_call(
        functools.partial(_gla_body, chunk=chunk, n_sub=n_sub),
        grid=(s // rows,),
        in_specs=[_rows(rows, gdk), _rows(rows, gdk), _rows(rows, gdv), _rows(rows, gdk),
                  _resident(mats.shape), _resident(masks.shape)],
        out_specs=_rows(rows, gdv),
        out_shape=jax.ShapeDtypeStruct((s, gdv), F32),
        scratch_shapes=[pltpu.VMEM((GLA_DV, gdk), F32)],
        compiler_params=_cparams(("arbitrary",)),
        name="gla_scan",
    )(gq, gk, gv, log_a, mats, masks)


def _even_out_body(h_ref, at_ref, o_ref, r_ref, ggla_ref, wa_ref, wg_ref, gpost_ref, out_ref):
    og = o_ref[...]
    gn = jnp.concatenate([_rms(og[:, hd * GLA_DV:(hd + 1) * GLA_DV], ggla_ref[...]) for hd in range(GLA_HEADS)],
                         axis=1)
    g = (gn * _silu(r_ref[...])).astype(BF16)
    mix = _dot_tn(at_ref[...], wa_ref[...]) + _dot(g, wg_ref[...])
    out_ref[...] = h_ref[...] + _rms(mix, gpost_ref[...])


def _even_out(h, a_t, o_gla, r, g_gla, w_a, w_g, g_post):
    s = h.shape[0]
    tm = min(ROW_TILE, s)
    return pl.pallas_call(
        _even_out_body,
        grid=(s // tm,),
        in_specs=[_rows(tm, D_MODEL), pl.BlockSpec((a_t.shape[0], tm), lambda i: (0, i)),
                  _rows(tm, o_gla.shape[1]), _rows(tm, r.shape[1]),
                  _resident(g_gla.shape), _resident(w_a.shape), _resident(w_g.shape), _resident(g_post.shape)],
        out_specs=_rows(tm, D_MODEL),
        out_shape=jax.ShapeDtypeStruct((s, D_MODEL), F32),
        compiler_params=_cparams(("parallel",)),
        name="even_out",
    )(h, a_t, o_gla, r, g_gla, w_a, w_g, g_post)


CONV_HALO = 8


def _odd_proj_body(h_ref, gpre_ref, win_ref, cw_ref, cb_ref, wq_ref, wk_ref, wv_ref, wgq_ref, wgk_ref, wgv_ref,
                   bg_ref, q_out, k_out, v_out, gates_out, xc_out, z_out, xbuf):
    tm = h_ref.shape[0]
    width = D_MODEL

    @pl.when(pl.program_id(0) == 0)
    def _():
        xbuf[0:CONV_HALO, :] = jnp.zeros((CONV_HALO, width), F32)

    hn = _rms(h_ref[...], gpre_ref[...]).astype(BF16)
    xz = _dot(hn, win_ref[...])
    x_m = xz[:, :width]
    z_out[...] = xz[:, width:]
    xbuf[CONV_HALO:CONV_HALO + tm, :] = x_m
    conv = cb_ref[...] + cw_ref[MLSTM_CONV - 1:MLSTM_CONV, :] * x_m
    for j in range(MLSTM_CONV - 1):
        back = MLSTM_CONV - 1 - j
        conv = conv + cw_ref[j:j + 1, :] * xbuf[CONV_HALO - back:CONV_HALO - back + tm, :]
    xbuf[0:CONV_HALO, :] = x_m[tm - CONV_HALO:, :]
    x_c = _silu(conv)
    xc_out[...] = x_c

    dh = MLSTM_DH
    xcb = x_c.astype(BF16)
    xmb = x_m.astype(BF16)
    qs, ks, vs = [], [], []
    for hd in range(MLSTM_HEADS):
        sl = slice(hd * dh, (hd + 1) * dh)
        qs.append(_dot(xcb[:, sl], wq_ref[hd]))
        ks.append(_dot(xcb[:, sl], wk_ref[hd]))
        vs.append(_dot(xmb[:, sl], wv_ref[hd]))
    q = jnp.concatenate(qs, axis=1).astype(BF16)
    k = jnp.concatenate(ks, axis=1).astype(BF16)
    v = jnp.concatenate(vs, axis=1).astype(BF16)
    gates_out[...] = _dot(q, wgq_ref[...]) + _dot(k, wgk_ref[...]) + _dot(v, wgv_ref[...]) + bg_ref[...]
    q_out[...] = (q.astype(F32) * (dh ** -0.5)).astype(BF16)
    k_out[...] = k
    v_out[...] = v


def _odd_proj(h, g_pre, w_in, conv_w, conv_b, w_q, w_k, w_v, wg_q, wg_k, wg_v, b_g):
    s = h.shape[0]
    tm = min(ROW_TILE, s)
    outs = [(D_MODEL, BF16), (D_MODEL, BF16), (D_MODEL, BF16), (V7X_LANES, F32), (D_MODEL, F32), (D_MODEL, F32)]
    ins = (g_pre, w_in, conv_w, conv_b, w_q, w_k, w_v, wg_q, wg_k, wg_v, b_g)
    return pl.pallas_call(
        _odd_proj_body,
        grid=(s // tm,),
        in_specs=[_rows(tm, D_MODEL)] + [_resident(a.shape) for a in ins],
        out_specs=[_rows(tm, w) for w, _ in outs],
        out_shape=[jax.ShapeDtypeStruct((s, w), dt) for w, dt in outs],
        scratch_shapes=[pltpu.VMEM((CONV_HALO + tm, D_MODEL), F32)],
        compiler_params=_cparams(("arbitrary",)),
        name="odd_proj",
    )(h, *ins)


def _prep_odd_gate_weights(w_gates, b_gates):
    nh, dh = MLSTM_HEADS, MLSTM_DH
    w4 = w_gates.reshape(nh, 3, dh, 2 * nh)
    pad = jnp.zeros((nh * dh, V7X_LANES - 2 * nh), F32)
    parts = [jnp.concatenate([w4[:, j].reshape(nh * dh, 2 * nh), pad], axis=1).astype(BF16) for j in range(3)]
    b = jnp.concatenate([b_gates, jnp.zeros((V7X_LANES - 2 * nh,), F32)]).reshape(1, V7X_LANES)
    return parts[0], parts[1], parts[2], b


def _mlstm_body(q_ref, k_ref, v_ref, gcol_ref, grow_ref, tri_ref, trit_ref, o_ref, c_scr, n_scr, m_scr, *, chunk):
    nh, dh = MLSTM_HEADS, MLSTM_DH

    @pl.when(pl.program_id(0) == 0)
    def _():
        c_scr[...] = jnp.zeros(c_scr.shape, F32)
        n_scr[...] = jnp.zeros(n_scr.shape, F32)
        m_scr[...] = jnp.zeros(m_scr.shape, F32)

    gc = gcol_ref[...]
    gr = grow_ref[0]
    lanes = gc.shape[1]
    bc_all = _dot(tri_ref[...], jnp.concatenate(_split3(_log_sigmoid(gc)), axis=1))
    bc_all = bc_all[:, :lanes] + bc_all[:, lanes:2 * lanes] + bc_all[:, 2 * lanes:]
    nr = gr.shape[0]
    br_all = _dot(jnp.concatenate(_split3(_log_sigmoid(gr)), axis=0), trit_ref[...])
    br_all = br_all[:nr] + br_all[nr:2 * nr] + br_all[2 * nr:]

    row = lax.broadcasted_iota(jnp.int32, (chunk, chunk), 0)
    col = lax.broadcasted_iota(jnp.int32, (chunk, chunk), 1)
    causal = col <= row
    for hd in range(nh):
        sl = slice(hd * dh, (hd + 1) * dh)
        b_c = bc_all[:, nh + hd:nh + hd + 1]
        i_c = gc[:, hd:hd + 1]
        b_r = br_all[nh + hd:nh + hd + 1, :]
        i_r = gr[hd:hd + 1, :]
        m_old = m_scr[hd:hd + 1, 0:1]
        qh = q_ref[:, sl]
        kh = k_ref[:, sl]
        vh = v_ref[:, sl]

        log_d = jnp.where(causal, b_c - b_r + i_r, NEG_BIG)
        m_inter = b_c + m_old
        m_t = jnp.maximum(m_inter, jnp.max(log_d, axis=-1, keepdims=True))
        w_intra = jnp.exp(log_d - m_t) * _dot_nt(qh, kh)
        w_inter = jnp.exp(m_inter - m_t)
        c_old = c_scr[hd]
        n_old = n_scr[hd:hd + 1, :]
        num = _dot(w_intra.astype(BF16), vh) + w_inter * _dot(qh, c_old.astype(BF16))
        den = (jnp.sum(w_intra, axis=-1, keepdims=True)
               + w_inter * jnp.sum(qh.astype(F32) * n_old, axis=-1, keepdims=True))
        o_ref[:, sl] = num / jnp.maximum(jnp.abs(den), jnp.exp(-m_t))

        b_end = b_c[chunk - 1:chunk, :]
        log_w = b_end - b_c + i_c
        m_new = jnp.maximum(b_end + m_old, jnp.max(log_w, axis=0, keepdims=True))
        w_s = jnp.exp(log_w - m_new)
        decay = jnp.exp(b_end + m_old - m_new)
        kw = kh.astype(F32) * w_s
        c_scr[hd] = decay * c_old + _dot_tn(kw.astype(BF16), vh)
        n_scr[hd:hd + 1, :] = decay * n_old + jnp.sum(kw, axis=0, keepdims=True)
        m_scr[hd:hd + 1, :] = jnp.broadcast_to(m_new, (1, m_scr.shape[1]))


def _mlstm(q, k, v, gates):
    s = q.shape[0]
    chunk = min(SCAN_CHUNK, s)
    nc = s // chunk
    ng = 2 * MLSTM_HEADS
    grow = gates[:, :ng].reshape(nc, chunk, ng).transpose(0, 2, 1)
    t = np.arange(chunk)
    tri = (t[None, :] <= t[:, None]).astype(np.float32)
    return pl.pallas_call(
        functools.partial(_mlstm_body, chunk=chunk),
        grid=(nc,),
        in_specs=[_rows(chunk, D_MODEL), _rows(chunk, D_MODEL), _rows(chunk, D_MODEL), _rows(chunk, V7X_LANES),
                  pl.BlockSpec((1, ng, chunk), lambda c: (c, 0, 0)),
                  _resident((chunk, chunk)), _resident((chunk, chunk))],
        out_specs=_rows(chunk, D_MODEL),
        out_shape=jax.ShapeDtypeStruct((s, D_MODEL), F32),
        scratch_shapes=[pltpu.VMEM((MLSTM_HEADS, MLSTM_DH, MLSTM_DH), F32),
                        pltpu.VMEM((8, MLSTM_DH), F32),
                        pltpu.VMEM((8, V7X_LANES), F32)],
        compiler_params=_cparams(("arbitrary",)),
        name="mlstm_scan",
    )(q, k, v, gates, grow, jnp.asarray(tri, BF16), jnp.asarray(tri.T, BF16))


def _odd_out_body(h_ref, hc_ref, xc_ref, z_ref, gh_ref, skip_ref, wo_ref, gpost_ref, out_ref):
    hc = hc_ref[...]
    dh = MLSTM_DH
    parts = []
    for hd in range(MLSTM_HEADS):
        seg = hc[:, hd * dh:(hd + 1) * dh]
        cen = seg - jnp.mean(seg, axis=-1, keepdims=True)
        parts.append(cen * lax.rsqrt(jnp.mean(cen * cen, axis=-1, keepdims=True) + EPS))
    hn = jnp.concatenate(parts, axis=1) * gh_ref[...]
    out = ((hn + skip_ref[...] * xc_ref[...]) * _silu(z_ref[...])).astype(BF16)
    out_ref[...] = h_ref[...] + _rms(_dot(out, wo_ref[...]), gpost_ref[...])


def _odd_out(h, hcell, x_c, z, g_hnorm, skip, w_out, g_post):
    s = h.shape[0]
    tm = min(ROW_TILE, s)
    return pl.pallas_call(
        _odd_out_body,
        grid=(s // tm,),
        in_specs=[_rows(tm, D_MODEL)] * 4 + [_resident(g_hnorm.shape), _resident(skip.shape),
                                              _resident(w_out.shape), _resident(g_post.shape)],
        out_specs=_rows(tm, D_MODEL),
        out_shape=jax.ShapeDtypeStruct((s, D_MODEL), F32),
        compiler_params=_cparams(("parallel",)),
        name="odd_out",
    )(h, hcell, x_c, z, g_hnorm, skip, w_out, g_post)


def _xattn_body(h_ref, mem_ref, gmem_ref, gpre_ref, wq_ref, wk_ref, wv_ref, wo_ref, gpost_ref, out_ref,
                k_scr, v_scr):
    @pl.when(pl.program_id(0) == 0)
    def _():
        mem_n = _rms(mem_ref[...], gmem_ref[...]).astype(BF16)
        k_scr[...] = _dot(mem_n, wk_ref[...]).astype(BF16)
        v_scr[...] = _dot(mem_n, wv_ref[...]).astype(BF16)

    h = h_ref[...]
    hn = _rms(h, gpre_ref[...]).astype(BF16)
    q = _dot(hn, wq_ref[...]).astype(BF16)
    dh = XATTN_DH
    scale = dh ** -0.5
    outs = []
    for hd in range(XATTN_HEADS):
        sl = slice(hd * dh, (hd + 1) * dh)
        s = _dot_nt(q[:, sl], k_scr[:, sl]) * scale
        p = jnp.exp(s - jnp.max(s, axis=-1, keepdims=True))
        inv_l = 1.0 / jnp.sum(p, axis=-1, keepdims=True)
        outs.append((_dot(p.astype(BF16), v_scr[:, sl]) * inv_l).astype(BF16))
    o = jnp.concatenate(outs, axis=1)
    out_ref[...] = h + _rms(_dot(o, wo_ref[...]), gpost_ref[...])


def _xattn(h, mem, g_mem, g_pre, w_q, w_k, w_v, w_o, g_post):
    s = h.shape[0]
    tm = min(ROW_TILE, s)
    ins = (mem, g_mem, g_pre, w_q, w_k, w_v, w_o, g_post)
    return pl.pallas_call(
        _xattn_body,
        grid=(s // tm,),
        in_specs=[_rows(tm, D_MODEL)] + [_resident(a.shape) for a in ins],
        out_specs=_rows(tm, D_MODEL),
        out_shape=jax.ShapeDtypeStruct((s, D_MODEL), F32),
        scratch_shapes=[pltpu.VMEM((mem.shape[0], D_MODEL), BF16), pltpu.VMEM((mem.shape[0], D_MODEL), BF16)],
        compiler_params=_cparams(("arbitrary",)),
        name="mem_xattn",
    )(h, *ins)


def _ffn_body(h_ref, gpre_ref, w1_ref, w2_ref, gpost_ref, out_ref):
    h = h_ref[...]
    xn = _rms(h, gpre_ref[...]).astype(BF16)
    acc = None
    for c in range(D_FF // FF_CHUNK):
        sl = slice(c * FF_CHUNK, (c + 1) * FF_CHUNK)
        a = jnp.square(jnp.maximum(_dot(xn, w1_ref[:, sl]), 0.0)).astype(BF16)
        part = _dot(a, w2_ref[sl, :])
        acc = part if acc is None else acc + part
    out_ref[...] = h + _rms(acc, gpost_ref[...])


def _ffn(h, g_pre, w1, w2, g_post):
    s = h.shape[0]
    tm = min(ROW_TILE, s)
    return pl.pallas_call(
        _ffn_body,
        grid=(s // tm,),
        in_specs=[_rows(tm, D_MODEL), _resident(g_pre.shape), _resident(w1.shape), _resident(w2.shape),
                  _resident(g_post.shape)],
        out_specs=_rows(tm, D_MODEL),
        out_shape=jax.ShapeDtypeStruct((s, D_MODEL), F32),
        compiler_params=_cparams(("parallel",)),
        name="sq_relu_mlp",
    )(h, g_pre, w1, w2, g_post)


def _row(g):
    return g.reshape(1, -1)


def kernel(x, mem, positions, g_mix_pre, g_mix_post, g_xattn_pre, g_xattn_post, g_mem, g_ffn_pre, g_ffn_post, ev_w_in, ev_g_q, ev_w_uq, ev_g_kv, ev_w_ukv, ev_w_gate, ev_b_gate, ev_g_gla, ev_w_out, od_w_in, od_conv_w, od_conv_b, od_w_q, od_w_k, od_w_v, od_w_gates, od_b_gates, od_g_hnorm, od_skip, od_w_out, xa_w_q, xa_w_k, xa_w_v, xa_w_o, ffn_w1, ffn_w2):
    batch, seq, d = x.shape
    assert batch == 1 and d == D_MODEL and seq % ROW_TILE == 0 and seq % (2 * ATTN_TK) == 0
    assert seq % (SCAN_CHUNK * GLA_CHUNKS_PER_STEP) == 0
    h = x.reshape(seq, d)
    mem2 = mem.reshape(mem.shape[1], d)
    cos_t, sin_t = _rope_tables(positions)
    n_a = MLA_HEADS * MLA_V

    for layer in range(DEPTH):
        j = layer // 2
        if layer % 2 == 0:
            w_in_p, w_uq_p, w_ukv_p, w_gate_p, b_gate = _prep_even_weights(
                ev_w_in[j], ev_w_uq[j], ev_w_ukv[j], ev_w_gate[j], ev_b_gate[j])
            q, k, v, gq, gk, gv, log_a, r = _even_proj(
                h, _row(g_mix_pre[layer]), w_in_p, _row(ev_g_q[j]), w_uq_p, _row(ev_g_kv[j]), w_ukv_p, w_gate_p,
                b_gate, cos_t, sin_t)
            a_t = _mla_attn(q, k, v)
            o_gla = _gla(gq, gk, gv, log_a)
            w_out = ev_w_out[j].astype(BF16)
            h = _even_out(h, a_t, o_gla, r, _row(ev_g_gla[j]), w_out[:n_a], w_out[n_a:], _row(g_mix_post[layer]))
        else:
            wg_q, wg_k, wg_v, b_g = _prep_odd_gate_weights(od_w_gates[j], od_b_gates[j])
            q, k, v, gates, x_c, z = _odd_proj(
                h, _row(g_mix_pre[layer]), od_w_in[j].astype(BF16), od_conv_w[j], _row(od_conv_b[j]),
                od_w_q[j].astype(BF16), od_w_k[j].astype(BF16), od_w_v[j].astype(BF16), wg_q, wg_k, wg_v, b_g)
            hcell = _mlstm(q, k, v, gates)
            h = _odd_out(h, hcell, x_c, z, _row(od_g_hnorm[j]), _row(od_skip[j]), od_w_out[j].astype(BF16),
                         _row(g_mix_post[layer]))
        h = _xattn(h, mem2, _row(g_mem[layer]), _row(g_xattn_pre[layer]), xa_w_q[layer].astype(BF16),
                   xa_w_k[layer].astype(BF16), xa_w_v[layer].astype(BF16), xa_w_o[layer].astype(BF16),
                   _row(g_xattn_post[layer]))
        h = _ffn(h, _row(g_ffn_pre[layer]), ffn_w1[layer].astype(BF16), ffn_w2[layer].astype(BF16),
                 _row(g_ffn_post[layer]))
    return h.reshape(batch, seq, d)
```

```python
import functools

import numpy as np
import jax
import jax.numpy as jnp
from jax import lax
from jax.experimental import pallas as pl
from jax.experimental.pallas import tpu as pltpu

F32 = jnp.float32
BF16 = jnp.bfloat16

D_MODEL = 1024
DEPTH = 4
EPS = 1e-6
MLA_HEADS = 8
MLA_NOPE = 64
MLA_ROPE = 32
MLA_V = 64
MLA_Q_RANK = 256
MLA_KV_RANK = 128
ROPE_THETA = 10000.0
GLA_HEADS = 4
GLA_DK = 64
GLA_DV = 128
GLA_GATE_RANK = 16
GLA_TAU = 16.0
MLSTM_HEADS = 4
MLSTM_DH = 256
MLSTM_CONV = 4
XATTN_HEADS = 4
XATTN_DH = 256
D_FF = 4096
EVEN_SPLITS = (256, 128, 32, 256, 256, 512, 16, 512)

V7X_LANES = 128
V7X_VMEM_BYTES = 64 * 1024 * 1024
VMEM_LIMIT = (V7X_VMEM_BYTES * 3) // 4

ROW_TILE = 512
FF_CHUNK = 1024
ATTN_TK = 512
ATTN_HEADS_PER_STEP = 2
SCAN_CHUNK = 128
GLA_CHUNKS_PER_STEP = 4
HEAD_PAD = V7X_LANES
ATTN_VROWS = MLA_V + 16

LOG2E = 1.4426950408889634
NEG_BIG = -1e30


def _cparams(sem):
    return pltpu.CompilerParams(dimension_semantics=sem, vmem_limit_bytes=VMEM_LIMIT)


def _resident(shape):
    nd = len(shape)
    return pl.BlockSpec(shape, lambda *_: (0,) * nd, pipeline_mode=pl.Buffered(1))


def _rows(tile, width):
    return pl.BlockSpec((tile, width), lambda i: (i, 0))


def _rms(x, g):
    return x * lax.rsqrt(jnp.mean(x * x, axis=-1, keepdims=True) + EPS) * g


def _silu(x):
    return x * (1.0 / (1.0 + jnp.exp(-x)))


def _log_sigmoid(x):
    return jnp.minimum(x, 0.0) - jnp.log(1.0 + jnp.exp(-jnp.abs(x)))


def _dot(a, b):
    return jnp.dot(a, b, preferred_element_type=F32)


def _dot_nt(a, b):
    return lax.dot_general(a, b, (((1,), (1,)), ((), ())), preferred_element_type=F32)


def _dot_tn(a, b):
    return lax.dot_general(a, b, (((0,), (0,)), ((), ())), preferred_element_type=F32)


def _split2(x):
    x1 = x.astype(BF16)
    return x1, (x - x1.astype(F32)).astype(BF16)


def _split3(x):
    x1 = x.astype(BF16)
    r = x - x1.astype(F32)
    x2 = r.astype(BF16)
    x3 = (r - x2.astype(F32)).astype(BF16)
    return x1, x2, x3


def _rope_body(pos_ref, freq_ref, cos_ref, sin_ref):
    ang = pos_ref[...] * freq_ref[...]
    lane = lax.broadcasted_iota(jnp.int32, ang.shape, 1)
    on = jnp.logical_and(lane >= MLA_NOPE, lane < MLA_NOPE + MLA_ROPE)
    cos_ref[...] = jnp.where(on, jnp.cos(ang), 0.0)
    sin_ref[...] = jnp.where(on, jnp.sin(ang), 0.0)


def _rope_tables(positions):
    s = positions.shape[-1]
    pos = positions.astype(F32).reshape(s, 1)
    inv_freq = ROPE_THETA ** (-jnp.arange(0, MLA_ROPE, 2, dtype=F32) / MLA_ROPE)
    freq = jnp.concatenate([jnp.zeros((MLA_NOPE,), F32), inv_freq, inv_freq,
                            jnp.zeros((HEAD_PAD - MLA_NOPE - MLA_ROPE,), F32)]).reshape(1, HEAD_PAD)
    tile = min(2048, s)
    return pl.pallas_call(
        _rope_body,
        grid=(s // tile,),
        in_specs=[_rows(tile, 1), _resident((1, HEAD_PAD))],
        out_specs=[_rows(tile, HEAD_PAD), _rows(tile, HEAD_PAD)],
        out_shape=[jax.ShapeDtypeStruct((s, HEAD_PAD), F32)] * 2,
        compiler_params=_cparams(("parallel",)),
        name="rope_tables",
    )(pos, freq)


_EV_OFF = np.cumsum((0, 256, 128, 128, 128, 256, 256, 512, 128, 512))


def _even_proj_body(h_ref, gpre_ref, win_ref, gq_ref, wuq_ref, gkv_ref, wuk_ref, wuvt_ref, wgate_ref, bgate_ref,
                    cos_ref, sin_ref,
                    q_ref, k_ref, vt_ref, gq_out, gk_out, gv_out, la_out, r_out):
    o = _EV_OFF
    hn = _rms(h_ref[...], gpre_ref[...]).astype(BF16)
    proj = _dot(hn, win_ref[...])
    c_q = proj[:, o[0]:o[1]]
    c_kv = proj[:, o[1]:o[2]]
    kpe_a = proj[:, o[2]:o[3]]
    kpe_b = proj[:, o[3]:o[4]]
    cosk = cos_ref[...]
    sink = sin_ref[...]
    nh = MLA_HEADS
    width = nh * HEAD_PAD

    qs = (MLA_NOPE + MLA_ROPE) ** -0.5 * LOG2E
    lane = lax.broadcasted_iota(jnp.int32, cosk.shape, 1)
    cosq = qs * jnp.where(lane < MLA_NOPE, 1.0, cosk)
    sinq = qs * sink
    cqn = _rms(c_q, gq_ref[...]).astype(BF16)
    qab = _dot(cqn, wuq_ref[...])
    q = qab[:, :width] * jnp.tile(cosq, (1, nh)) + qab[:, width:] * jnp.tile(sinq, (1, nh))
    q_ref[...] = q.astype(BF16)

    ckvn = _rms(c_kv, gkv_ref[...]).astype(BF16)
    kpe = kpe_a * cosk + kpe_b * sink
    k_ref[...] = (_dot(ckvn, wuk_ref[...]) + jnp.tile(kpe, (1, nh))).astype(BF16)
    vt = _dot_nt(wuvt_ref[...], ckvn).astype(BF16)
    row = lax.broadcasted_iota(jnp.int32, (ATTN_VROWS - MLA_V, vt.shape[1]), 0)
    extra = jnp.where(row == 0, 1.0, 0.0).astype(BF16)
    for hd in range(nh):
        vt_ref[hd, 0:MLA_V, :] = vt[hd * MLA_V:(hd + 1) * MLA_V]
        vt_ref[hd, MLA_V:ATTN_VROWS, :] = extra

    gq_out[...] = proj[:, o[4]:o[5]]
    gk_out[...] = proj[:, o[5]:o[6]]
    gv_out[...] = proj[:, o[6]:o[7]]
    glr = proj[:, o[7]:o[8]].astype(BF16)
    x = _dot(glr, wgate_ref[...]) + bgate_ref[...]
    la_out[...] = _log_sigmoid(x) * (1.0 / GLA_TAU)
    r_out[...] = proj[:, o[8]:o[9]]


def _even_proj(h, g_pre, w_in, g_q, w_uq, g_kv, w_uk, w_uvt, w_gate, b_gate, cos_t, sin_t):
    s = h.shape[0]
    tm = min(ROW_TILE, s)
    width = MLA_HEADS * HEAD_PAD
    gdk = GLA_HEADS * GLA_DK
    gdv = GLA_HEADS * GLA_DV
    rows = [(width, BF16), (width, BF16), None, (gdk, F32), (gdk, F32), (gdv, F32), (gdk, F32), (gdv, F32)]
    vt_spec = pl.BlockSpec((MLA_HEADS, None, ATTN_VROWS, tm), lambda i: (0, i, 0, 0))
    vt_shape = jax.ShapeDtypeStruct((MLA_HEADS, s // tm, ATTN_VROWS, tm), BF16)
    return pl.pallas_call(
        _even_proj_body,
        grid=(s // tm,),
        in_specs=[_rows(tm, D_MODEL), _resident(g_pre.shape), _resident(w_in.shape), _resident(g_q.shape),
                  _resident(w_uq.shape), _resident(g_kv.shape), _resident(w_uk.shape), _resident(w_uvt.shape),
                  _resident(w_gate.shape), _resident(b_gate.shape), _rows(tm, HEAD_PAD), _rows(tm, HEAD_PAD)],
        out_specs=[vt_spec if r is None else _rows(tm, r[0]) for r in rows],
        out_shape=[vt_shape if r is None else jax.ShapeDtypeStruct((s, r[0]), r[1]) for r in rows],
        compiler_params=_cparams(("parallel",)),
        name="even_proj",
    )(h, g_pre, w_in, g_q, w_uq, g_kv, w_uk, w_uvt, w_gate, b_gate, cos_t, sin_t)


def _prep_even_weights(w_in, w_uq, w_ukv, w_gate, b_gate):
    d = w_in.shape[0]
    off = np.cumsum((0,) + EVEN_SPLITS)
    seg = [w_in[:, off[i]:off[i + 1]] for i in range(len(EVEN_SPLITS))]
    c_q, c_kv, k_pe, gq, gk, gv, glr, r = seg
    half = MLA_ROPE // 2
    z = lambda n: jnp.zeros((d, n), F32)
    pad = HEAD_PAD - MLA_NOPE - MLA_ROPE
    kpe_a = jnp.concatenate([z(MLA_NOPE), k_pe, z(pad)], axis=1)
    kpe_b = jnp.concatenate([z(MLA_NOPE), -k_pe[:, half:], k_pe[:, :half], z(pad)], axis=1)
    glr_p = jnp.concatenate([glr, z(HEAD_PAD - GLA_GATE_RANK)], axis=1)
    w_in_p = jnp.concatenate([c_q, c_kv, kpe_a, kpe_b, gq, gk, gv, glr_p, r], axis=1).astype(BF16)

    nh = MLA_HEADS
    wq3 = w_uq.reshape(MLA_Q_RANK, nh, MLA_NOPE + MLA_ROPE)
    nope, rope = wq3[..., :MLA_NOPE], wq3[..., MLA_NOPE:]
    zq = lambda n: jnp.zeros((MLA_Q_RANK, nh, n), F32)
    wa = jnp.concatenate([nope, rope, zq(pad)], axis=-1).reshape(MLA_Q_RANK, nh * HEAD_PAD)
    wb = jnp.concatenate([zq(MLA_NOPE), -rope[..., half:], rope[..., :half], zq(pad)],
                         axis=-1).reshape(MLA_Q_RANK, nh * HEAD_PAD)
    w_uq_p = jnp.concatenate([wa, wb], axis=1).astype(BF16)

    wkv3 = w_ukv.reshape(MLA_KV_RANK, nh, MLA_NOPE + MLA_V)
    zk = jnp.zeros((MLA_KV_RANK, nh, HEAD_PAD - MLA_NOPE), F32)
    w_uk_p = jnp.concatenate([wkv3[..., :MLA_NOPE], zk], axis=-1).reshape(MLA_KV_RANK, nh * HEAD_PAD).astype(BF16)
    w_uvt_p = wkv3[..., MLA_NOPE:].reshape(MLA_KV_RANK, nh * MLA_V).T.astype(BF16)

    w_gate_p = jnp.concatenate([w_gate, jnp.zeros((HEAD_PAD - GLA_GATE_RANK, w_gate.shape[1]), F32)],
                               axis=0).astype(BF16)
    return w_in_p, w_uq_p, w_uk_p, w_uvt_p, w_gate_p, b_gate.reshape(1, -1)


def _mla_attn_body(q_ref, k_ref, vt_ref, o_ref, acc_scr, st_scr, p_scr, m_scr, alpha_scr, *, tq, tk, hg):
    i = pl.program_id(1)
    heads = range(hg)

    def scores(t, slot):
        start = pl.multiple_of(t * tk, tk)
        for hd in heads:
            lanes = slice(hd * HEAD_PAD, (hd + 1) * HEAD_PAD)
            st_scr[slot, hd] = _dot_nt(k_ref[pl.ds(start, tk), lanes], q_ref[:, lanes])

    def accumulate(t, slot):
        for hd in heads:
            acc_scr[hd] = alpha_scr[slot, hd] * acc_scr[hd] + _dot(vt_ref[hd, t], p_scr[slot, hd])

    def softmax(slot, mask_shift):
        for hd in heads:
            st = st_scr[slot, hd]
            if mask_shift is not None:
                key = lax.broadcasted_iota(jnp.int32, st.shape, 0) + mask_shift
                qry = lax.broadcasted_iota(jnp.int32, st.shape, 1)
                st = jnp.where(key <= qry, st, NEG_BIG)
            m_prev = m_scr[hd]
            m_next = jnp.maximum(m_prev, jnp.max(st, axis=0, keepdims=True))
            p_scr[slot, hd] = jnp.exp2(st - m_next).astype(BF16)
            alpha_scr[slot, hd] = jnp.exp2(m_prev - m_next)
            m_scr[hd] = m_next

    def step(t, slot, mask_shift, more):
        softmax(slot, mask_shift)
        if more:
            scores(t + 1, 1 - slot)
        accumulate(jnp.maximum(t - 1, 0), 1 - slot)

    acc_scr[...] = jnp.zeros(acc_scr.shape, F32)
    p_scr[1] = jnp.zeros(p_scr.shape[1:], BF16)
    alpha_scr[1] = jnp.ones(alpha_scr.shape[1:], F32)
    m_scr[...] = jnp.full(m_scr.shape, NEG_BIG, F32)
    scores(0, 0)

    def pair(u, c):
        step(2 * u, 0, None, True)
        step(2 * u + 1, 1, None, True)
        return c

    lax.fori_loop(0, i, pair, 0)
    step(2 * i, 0, 0, True)
    step(2 * i + 1, 1, tk, False)
    accumulate(2 * i + 1, 1)
    for hd in heads:
        acc = acc_scr[hd]
        o_ref[hd] = (acc[:MLA_V] * (1.0 / acc[MLA_V:MLA_V + 1])).astype(o_ref.dtype)


def _mla_attn(q, k, vt):
    s = q.shape[0]
    nk, tk = vt.shape[1], vt.shape[3]
    tq = 2 * tk
    hg = ATTN_HEADS_PER_STEP
    out = pl.pallas_call(
        functools.partial(_mla_attn_body, tq=tq, tk=tk, hg=hg),
        grid=(MLA_HEADS // hg, s // tq),
        in_specs=[pl.BlockSpec((tq, hg * HEAD_PAD), lambda g, i: (i, g)),
                  pl.BlockSpec((s, hg * HEAD_PAD), lambda g, i: (0, g), pipeline_mode=pl.Buffered(1)),
                  pl.BlockSpec((hg, nk, ATTN_VROWS, tk), lambda g, i: (g, 0, 0, 0), pipeline_mode=pl.Buffered(1))],
        out_specs=pl.BlockSpec((hg, MLA_V, tq), lambda g, i: (g, 0, i)),
        out_shape=jax.ShapeDtypeStruct((MLA_HEADS, MLA_V, s), BF16),
        scratch_shapes=[pltpu.VMEM((hg, ATTN_VROWS, tq), F32),
                        pltpu.VMEM((2, hg, tk, tq), F32),
                        pltpu.VMEM((2, hg, tk, tq), BF16),
                        pltpu.VMEM((hg, 1, tq), F32),
                        pltpu.VMEM((2, hg, 1, tq), F32)],
        compiler_params=_cparams(("parallel", "arbitrary")),
        name="mla_attn",
    )(q, k, vt)
    return out.reshape(MLA_HEADS * MLA_V, s)


def _gla_levels(chunk):
    n = 0
    while (1 << n) < chunk:
        n += 1
    return n


def _gla_constants(chunk):
    t = np.arange(chunk)
    mats = [t[None, :] <= t[:, None]]
    masks = []
    b = 1
    while b < chunk:
        blk = t // (2 * b)
        upper = (t % (2 * b)) >= b
        e = blk * 2 * b + b - 1
        up_rows = upper[:, None] & (t[None, :] > e[:, None]) & (t[None, :] <= t[:, None])
        lo_rows = (~upper)[:, None] & (t[None, :] > t[:, None]) & (t[None, :] <= e[:, None])
        mats.append(up_rows | lo_rows)
        masks.append(upper[:, None] & (~upper)[None, :] & (blk[:, None] == blk[None, :]))
        b *= 2
    masks.append(t[:, None] == t[None, :])
    mats = np.concatenate(mats, axis=0).astype(np.float32)
    masks = np.stack(masks).astype(np.float32)
    masks = np.tile(masks, (1, GLA_HEADS, 1))
    return jnp.asarray(mats, BF16), jnp.asarray(masks, F32)


def _gla_body(q_ref, k_ref, v_ref, g_ref, mats_ref, masks_ref, o_ref, st_ref, *, chunk, n_sub):
    nlev = _gla_levels(chunk)
    nh, dk, dv = GLA_HEADS, GLA_DK, GLA_DV
    w = nh * dk

    @pl.when(pl.program_id(0) == 0)
    def _():
        st_ref[...] = jnp.zeros(st_ref.shape, F32)

    head_of_lane = lax.broadcasted_iota(jnp.int32, (chunk, w), 1) // dk

    def stack_heads(x):
        return jnp.concatenate([jnp.where(head_of_lane == hd, x, 0.0) for hd in range(nh)], axis=0).astype(BF16)

    pending = []
    for c in range(n_sub):
        rows = slice(c * chunk, (c + 1) * chunk)
        q = q_ref[rows, :] * (dk ** -0.5)
        k = k_ref[rows, :]
        g = g_ref[rows, :]
        e_all = _dot(mats_ref[...], jnp.concatenate(_split2(g), axis=1))
        e_all = e_all[:, :w] + e_all[:, w:]
        b = e_all[:chunk]
        b_end = b[chunk - 1:chunk]
        attn = _dot_nt(stack_heads(q), k.astype(BF16)) * masks_ref[nlev]
        for lv in range(nlev):
            dec = jnp.exp(e_all[(1 + lv) * chunk:(2 + lv) * chunk])
            attn = attn + _dot_nt(stack_heads(q * dec), (k * dec).astype(BF16)) * masks_ref[lv]
        attn = attn.astype(BF16)
        for hd in range(nh):
            vh = v_ref[rows, hd * dv:(hd + 1) * dv].astype(BF16)
            o_ref[rows, hd * dv:(hd + 1) * dv] = _dot(attn[hd * chunk:(hd + 1) * chunk], vh)
        pending.append((stack_heads(q * jnp.exp(b)), (k * jnp.exp(b_end - b)).astype(BF16), jnp.exp(b_end)))

    st = st_ref[...]
    head_of_state_lane = lax.broadcasted_iota(jnp.int32, st.shape, 1) // dk
    for c in range(n_sub):
        rows = slice(c * chunk, (c + 1) * chunk)
        qg, kg, dec_end = pending[c]
        inter = _dot_nt(qg, st.astype(BF16))
        new = st * dec_end
        for hd in range(nh):
            cols = slice(hd * dv, (hd + 1) * dv)
            o_ref[rows, cols] = o_ref[rows, cols] + inter[hd * chunk:(hd + 1) * chunk]
            upd = _dot_tn(v_ref[rows, cols].astype(BF16), kg)
            new = new + jnp.where(head_of_state_lane == hd, upd, 0.0)
        st = new
    st_ref[...] = st


def _gla(gq, gk, gv, log_a):
    s = gq.shape[0]
    chunk = min(SCAN_CHUNK, s)
    n_sub = min(GLA_CHUNKS_PER_STEP, s // chunk)
    rows = chunk * n_sub
    mats, masks = _gla_constants(chunk)
    gdk = GLA_HEADS * GLA_DK
    gdv = GLA_HEADS * GLA_DV
    return pl.pallas_call(
        functools.partial(_gla_body, chunk=chunk, n_sub=n_sub),
        grid=(s // rows,),
        in_specs=[_rows(rows, gdk), _rows(rows, gdk), _rows(rows, gdv), _rows(rows, gdk),
                  _resident(mats.shape), _resident(masks.shape)],
        out_specs=_rows(rows, gdv),
        out_shape=jax.ShapeDtypeStruct((s, gdv), F32),
        scratch_shapes=[pltpu.VMEM((GLA_DV, gdk), F32)],
        compiler_params=_cparams(("arbitrary",)),
        name="gla_scan",
    )(gq, gk, gv, log_a, mats, masks)


def _even_out_body(h_ref, at_ref, o_ref, r_ref, ggla_ref, wa_ref, wg_ref, gpost_ref, out_ref):
    og = o_ref[...]
    gn = jnp.concatenate([_rms(og[:, hd * GLA_DV:(hd + 1) * GLA_DV], ggla_ref[...]) for hd in range(GLA_HEADS)],
                         axis=1)
    g = (gn * _silu(r_ref[...])).astype(BF16)
    mix = _dot_tn(at_ref[...], wa_ref[...]) + _dot(g, wg_ref[...])
    out_ref[...] = h_ref[...] + _rms(mix, gpost_ref[...])


def _even_out(h, a_t, o_gla, r, g_gla, w_a, w_g, g_post):
    s = h.shape[0]
    tm = min(ROW_TILE, s)
    return pl.pallas_call(
        _even_out_body,
        grid=(s // tm,),
        in_specs=[_rows(tm, D_MODEL), pl.BlockSpec((a_t.shape[0], tm), lambda i: (0, i)),
                  _rows(tm, o_gla.shape[1]), _rows(tm, r.shape[1]),
                  _resident(g_gla.shape), _resident(w_a.shape), _resident(w_g.shape), _resident(g_post.shape)],
        out_specs=_rows(tm, D_MODEL),
        out_shape=jax.ShapeDtypeStruct((s, D_MODEL), F32),
        compiler_params=_cparams(("parallel",)),
        name="even_out",
    )(h, a_t, o_gla, r, g_gla, w_a, w_g, g_post)


CONV_HALO = 8


def _odd_proj_body(h_ref, gpre_ref, win_ref, cw_ref, cb_ref, wq_ref, wk_ref, wkt_ref, wv_ref, wgq_ref, wgk_ref,
                   wgv_ref, bg_ref, q_out, k_out, kt_out, v_out, gates_out, xc_out, z_out, xbuf):
    tm = h_ref.shape[0]
    width = D_MODEL

    @pl.when(pl.program_id(0) == 0)
    def _():
        xbuf[0:CONV_HALO, :] = jnp.zeros((CONV_HALO, width), F32)

    hn = _rms(h_ref[...], gpre_ref[...]).astype(BF16)
    xz = _dot(hn, win_ref[...])
    x_m = xz[:, :width]
    z_out[...] = xz[:, width:].astype(z_out.dtype)
    xbuf[CONV_HALO:CONV_HALO + tm, :] = x_m
    conv = cb_ref[...] + cw_ref[MLSTM_CONV - 1:MLSTM_CONV, :] * x_m
    for j in range(MLSTM_CONV - 1):
        back = MLSTM_CONV - 1 - j
        conv = conv + cw_ref[j:j + 1, :] * xbuf[CONV_HALO - back:CONV_HALO - back + tm, :]
    xbuf[0:CONV_HALO, :] = x_m[tm - CONV_HALO:, :]
    x_c = _silu(conv)
    xcb = x_c.astype(BF16)
    xc_out[...] = xcb

    dh = MLSTM_DH
    xmb = x_m.astype(BF16)
    qs, ks, vs = [], [], []
    for hd in range(MLSTM_HEADS):
        sl = slice(hd * dh, (hd + 1) * dh)
        qs.append(_dot(xcb[:, sl], wq_ref[hd]))
        ks.append(_dot(xcb[:, sl], wk_ref[hd]))
        vs.append(_dot(xmb[:, sl], wv_ref[hd]))
        kt_out[sl, :] = _dot_nt(wkt_ref[hd], xcb[:, sl]).astype(BF16)
    q = jnp.concatenate(qs, axis=1).astype(BF16)
    k = jnp.concatenate(ks, axis=1).astype(BF16)
    v = jnp.concatenate(vs, axis=1).astype(BF16)
    gates_out[...] = _dot(q, wgq_ref[...]) + _dot(k, wgk_ref[...]) + _dot(v, wgv_ref[...]) + bg_ref[...]
    q_out[...] = (q.astype(F32) * (dh ** -0.5)).astype(BF16)
    k_out[...] = k
    v_out[...] = v


def _odd_proj(h, g_pre, w_in, conv_w, conv_b, w_q, w_k, w_kt, w_v, wg_q, wg_k, wg_v, b_g):
    s = h.shape[0]
    tm = min(ROW_TILE, s)
    rows = [(D_MODEL, BF16), (D_MODEL, BF16), None, (D_MODEL, BF16), (2 * V7X_LANES, F32), (D_MODEL, BF16),
            (D_MODEL, BF16)]
    ins = (g_pre, w_in, conv_w, conv_b, w_q, w_k, w_kt, w_v, wg_q, wg_k, wg_v, b_g)
    kt_spec = pl.BlockSpec((D_MODEL, tm), lambda i: (0, i))
    kt_shape = jax.ShapeDtypeStruct((D_MODEL, s), BF16)
    return pl.pallas_call(
        _odd_proj_body,
        grid=(s // tm,),
        in_specs=[_rows(tm, D_MODEL)] + [_resident(a.shape) for a in ins],
        out_specs=[kt_spec if r is None else _rows(tm, r[0]) for r in rows],
        out_shape=[kt_shape if r is None else jax.ShapeDtypeStruct((s, r[0]), r[1]) for r in rows],
        scratch_shapes=[pltpu.VMEM((CONV_HALO + tm, D_MODEL), F32)],
        compiler_params=_cparams(("arbitrary",)),
        name="odd_proj",
    )(h, *ins)


def _prep_odd_gate_weights(w_gates, b_gates):
    nh, dh = MLSTM_HEADS, MLSTM_DH
    w4 = w_gates.reshape(nh, 3, dh, 2 * nh)
    pad = jnp.zeros((nh * dh, V7X_LANES - nh), F32)

    def spread(w):
        return jnp.concatenate([w[:, :nh], pad[:w.shape[0]], w[:, nh:], pad[:w.shape[0]]], axis=1)

    parts = [spread(w4[:, j].reshape(nh * dh, 2 * nh)).astype(BF16) for j in range(3)]
    b = spread(b_gates.reshape(1, 2 * nh))
    return parts[0], parts[1], parts[2], b


MLSTM_EXT = MLSTM_DH + V7X_LANES


def _mlstm_body(q_ref, k_ref, kt_ref, v_ref, gcol_ref, grow_ref, tri_ref, trit_ref, o_ref, c_scr, m_scr, *, chunk):
    nh, dh = MLSTM_HEADS, MLSTM_DH
    lanes = V7X_LANES

    @pl.when(pl.program_id(0) == 0)
    def _():
        c_scr[...] = jnp.zeros(c_scr.shape, F32)
        m_scr[...] = jnp.zeros(m_scr.shape, F32)

    gc = gcol_ref[...]
    b_col = _dot(tri_ref[...], jnp.concatenate(_split3(_log_sigmoid(gc[:, lanes:])), axis=1))
    b_col = b_col[:, :lanes] + b_col[:, lanes:2 * lanes] + b_col[:, 2 * lanes:]
    gr = grow_ref[0]
    nr = gr.shape[0] // 2
    b_row = _dot(jnp.concatenate(_split3(_log_sigmoid(gr[nr:])), axis=0), trit_ref[...])
    b_row = b_row[:nr] + b_row[nr:2 * nr] + b_row[2 * nr:]
    r_row = gr[:nr] - b_row

    row = lax.broadcasted_iota(jnp.int32, (chunk, chunk), 0)
    col = lax.broadcasted_iota(jnp.int32, (chunk, chunk), 1)
    causal = col <= row
    ones_col = jnp.where(lax.broadcasted_iota(jnp.int32, (chunk, lanes), 1) == 0, 1.0, 0.0).astype(BF16)
    for hd in range(nh):
        sl = slice(hd * dh, (hd + 1) * dh)
        r_h = r_row[hd:hd + 1, :]
        b_h = b_col[:, hd:hd + 1]
        m_old = m_scr[hd:hd + 1, 0:1]
        qh = q_ref[:, sl]
        vext = jnp.concatenate([v_ref[:, sl], ones_col], axis=1)
        cext = c_scr[hd]

        logw = jnp.where(causal, r_h, NEG_BIG)
        big_m = jnp.maximum(jnp.max(logw, axis=-1, keepdims=True), m_old)
        w_intra = jnp.exp(logw - big_m) * _dot_nt(qh, k_ref[:, sl])
        w_inter = jnp.exp(m_old - big_m)
        out = (_dot(w_intra.astype(BF16), vext)
               + _dot((qh.astype(F32) * w_inter).astype(BF16), cext.astype(BF16)))
        den = out[:, dh:dh + 1]
        o_ref[:, sl] = out[:, :dh] / jnp.maximum(jnp.abs(den), jnp.exp(-(b_h + big_m)))

        m_last = big_m[chunk - 1:chunk, :]
        w_s = jnp.exp(r_h - m_last)
        ktw = (kt_ref[sl, :].astype(F32) * w_s).astype(BF16)
        c_scr[hd] = jnp.exp(m_old - m_last) * cext + _dot(ktw, vext)
        m_scr[hd:hd + 1, :] = jnp.broadcast_to(b_h[chunk - 1:chunk, :] + m_last, (1, m_scr.shape[1]))


def _mlstm(q, k, kt, v, gates):
    s = q.shape[0]
    chunk = min(SCAN_CHUNK, s)
    nc = s // chunk
    lanes = V7X_LANES
    grow = jnp.concatenate([gates[:, :8], gates[:, lanes:lanes + 8]], axis=1)
    grow = grow.reshape(nc, chunk, 16).transpose(0, 2, 1)
    t = np.arange(chunk)
    tri = (t[None, :] <= t[:, None]).astype(np.float32)
    return pl.pallas_call(
        functools.partial(_mlstm_body, chunk=chunk),
        grid=(nc,),
        in_specs=[_rows(chunk, D_MODEL), _rows(chunk, D_MODEL), pl.BlockSpec((D_MODEL, chunk), lambda c: (0, c)),
                  _rows(chunk, D_MODEL), _rows(chunk, 2 * lanes),
                  pl.BlockSpec((1, 16, chunk), lambda c: (c, 0, 0)),
                  _resident((chunk, chunk)), _resident((chunk, chunk))],
        out_specs=_rows(chunk, D_MODEL),
        out_shape=jax.ShapeDtypeStruct((s, D_MODEL), F32),
        scratch_shapes=[pltpu.VMEM((MLSTM_HEADS, MLSTM_DH, MLSTM_EXT), F32),
                        pltpu.VMEM((8, lanes), F32)],
        compiler_params=_cparams(("arbitrary",)),
        name="mlstm_scan",
    )(q, k, kt, v, gates, grow, jnp.asarray(tri, BF16), jnp.asarray(tri.T, BF16))


def _odd_out_body(h_ref, hc_ref, xc_ref, z_ref, gh_ref, skip_ref, wo_ref, gpost_ref, out_ref):
    hc = hc_ref[...]
    dh = MLSTM_DH
    parts = []
    for hd in range(MLSTM_HEADS):
        seg = hc[:, hd * dh:(hd + 1) * dh]
        cen = seg - jnp.mean(seg, axis=-1, keepdims=True)
        parts.append(cen * lax.rsqrt(jnp.mean(cen * cen, axis=-1, keepdims=True) + EPS))
    hn = jnp.concatenate(parts, axis=1) * gh_ref[...]
    out = ((hn + skip_ref[...] * xc_ref[...].astype(F32)) * _silu(z_ref[...].astype(F32))).astype(BF16)
    out_ref[...] = h_ref[...] + _rms(_dot(out, wo_ref[...]), gpost_ref[...])


def _odd_out(h, hcell, x_c, z, g_hnorm, skip, w_out, g_post):
    s = h.shape[0]
    tm = min(ROW_TILE, s)
    return pl.pallas_call(
        _odd_out_body,
        grid=(s // tm,),
        in_specs=[_rows(tm, D_MODEL)] * 4 + [_resident(g_hnorm.shape), _resident(skip.shape),
                                              _resident(w_out.shape), _resident(g_post.shape)],
        out_specs=_rows(tm, D_MODEL),
        out_shape=jax.ShapeDtypeStruct((s, D_MODEL), F32),
        compiler_params=_cparams(("parallel",)),
        name="odd_out",
    )(h, hcell, x_c, z, g_hnorm, skip, w_out, g_post)


def _xattn_body(h_ref, mem_ref, gmem_ref, gpre_ref, wq_ref, wk_ref, wv_ref, wo_ref, gpost_ref, out_ref,
                k_scr, v_scr):
    @pl.when(pl.program_id(0) == 0)
    def _():
        mem_n = _rms(mem_ref[...], gmem_ref[...]).astype(BF16)
        k_scr[...] = _dot(mem_n, wk_ref[...]).astype(BF16)
        v_scr[...] = _dot(mem_n, wv_ref[...]).astype(BF16)

    h = h_ref[...]
    hn = _rms(h, gpre_ref[...]).astype(BF16)
    q = _dot(hn, wq_ref[...]).astype(BF16)
    dh = XATTN_DH
    scale = dh ** -0.5
    outs = []
    for hd in range(XATTN_HEADS):
        sl = slice(hd * dh, (hd + 1) * dh)
        s = _dot_nt(q[:, sl], k_scr[:, sl]) * scale
        p = jnp.exp(s - jnp.max(s, axis=-1, keepdims=True))
        inv_l = 1.0 / jnp.sum(p, axis=-1, keepdims=True)
        outs.append((_dot(p.astype(BF16), v_scr[:, sl]) * inv_l).astype(BF16))
    o = jnp.concatenate(outs, axis=1)
    out_ref[...] = h + _rms(_dot(o, wo_ref[...]), gpost_ref[...])


def _xattn(h, mem, g_mem, g_pre, w_q, w_k, w_v, w_o, g_post):
    s = h.shape[0]
    tm = min(ROW_TILE, s)
    ins = (mem, g_mem, g_pre, w_q, w_k, w_v, w_o, g_post)
    return pl.pallas_call(
        _xattn_body,
        grid=(s // tm,),
        in_specs=[_rows(tm, D_MODEL)] + [_resident(a.shape) for a in ins],
        out_specs=_rows(tm, D_MODEL),
        out_shape=jax.ShapeDtypeStruct((s, D_MODEL), F32),
        scratch_shapes=[pltpu.VMEM((mem.shape[0], D_MODEL), BF16), pltpu.VMEM((mem.shape[0], D_MODEL), BF16)],
        compiler_params=_cparams(("arbitrary",)),
        name="mem_xattn",
    )(h, *ins)


def _ffn_body(h_ref, gpre_ref, w1_ref, w2_ref, gpost_ref, out_ref):
    h = h_ref[...]
    xn = _rms(h, gpre_ref[...]).astype(BF16)
    acc = None
    for c in range(D_FF // FF_CHUNK):
        sl = slice(c * FF_CHUNK, (c + 1) * FF_CHUNK)
        a = jnp.square(jnp.maximum(_dot(xn, w1_ref[:, sl]), 0.0)).astype(BF16)
        part = _dot(a, w2_ref[sl, :])
        acc = part if acc is None else acc + part
    out_ref[...] = h + _rms(acc, gpost_ref[...])


def _ffn(h, g_pre, w1, w2, g_post):
    s = h.shape[0]
    tm = min(ROW_TILE, s)
    return pl.pallas_call(
        _ffn_body,
        grid=(s // tm,),
        in_specs=[_rows(tm, D_MODEL), _resident(g_pre.shape), _resident(w1.shape), _resident(w2.shape),
                  _resident(g_post.shape)],
        out_specs=_rows(tm, D_MODEL),
        out_shape=jax.ShapeDtypeStruct((s, D_MODEL), F32),
        compiler_params=_cparams(("parallel",)),
        name="sq_relu_mlp",
    )(h, g_pre, w1, w2, g_post)


def _row(g):
    return g.reshape(1, -1)


def kernel(x, mem, positions, g_mix_pre, g_mix_post, g_xattn_pre, g_xattn_post, g_mem, g_ffn_pre, g_ffn_post, ev_w_in, ev_g_q, ev_w_uq, ev_g_kv, ev_w_ukv, ev_w_gate, ev_b_gate, ev_g_gla, ev_w_out, od_w_in, od_conv_w, od_conv_b, od_w_q, od_w_k, od_w_v, od_w_gates, od_b_gates, od_g_hnorm, od_skip, od_w_out, xa_w_q, xa_w_k, xa_w_v, xa_w_o, ffn_w1, ffn_w2):
    batch, seq, d = x.shape
    assert batch == 1 and d == D_MODEL and seq % ROW_TILE == 0 and seq % (2 * ATTN_TK) == 0
    assert seq % (SCAN_CHUNK * GLA_CHUNKS_PER_STEP) == 0 and ROW_TILE == ATTN_TK
    h = x.reshape(seq, d)
    mem2 = mem.reshape(mem.shape[1], d)
    cos_t, sin_t = _rope_tables(positions)
    n_a = MLA_HEADS * MLA_V

    for layer in range(DEPTH):
        j = layer // 2
        if layer % 2 == 0:
            w_in_p, w_uq_p, w_uk_p, w_uvt_p, w_gate_p, b_gate = _prep_even_weights(
                ev_w_in[j], ev_w_uq[j], ev_w_ukv[j], ev_w_gate[j], ev_b_gate[j])
            q, k, vt, gq, gk, gv, log_a, r = _even_proj(
                h, _row(g_mix_pre[layer]), w_in_p, _row(ev_g_q[j]), w_uq_p, _row(ev_g_kv[j]), w_uk_p, w_uvt_p,
                w_gate_p, b_gate, cos_t, sin_t)
            a_t = _mla_attn(q, k, vt)
            o_gla = _gla(gq, gk, gv, log_a)
            w_out = ev_w_out[j].astype(BF16)
            h = _even_out(h, a_t, o_gla, r, _row(ev_g_gla[j]), w_out[:n_a], w_out[n_a:], _row(g_mix_post[layer]))
        else:
            wg_q, wg_k, wg_v, b_g = _prep_odd_gate_weights(od_w_gates[j], od_b_gates[j])
            w_k = od_w_k[j].astype(BF16)
            q, k, kt, v, gates, x_c, z = _odd_proj(
                h, _row(g_mix_pre[layer]), od_w_in[j].astype(BF16), od_conv_w[j], _row(od_conv_b[j]),
                od_w_q[j].astype(BF16), w_k, w_k.transpose(0, 2, 1), od_w_v[j].astype(BF16), wg_q, wg_k, wg_v, b_g)
            hcell = _mlstm(q, k, kt, v, gates)
            h = _odd_out(h, hcell, x_c, z, _row(od_g_hnorm[j]), _row(od_skip[j]), od_w_out[j].astype(BF16),
                         _row(g_mix_post[layer]))
        h = _xattn(h, mem2, _row(g_mem[layer]), _row(g_xattn_pre[layer]), xa_w_q[layer].astype(BF16),
                   xa_w_k[layer].astype(BF16), xa_w_v[layer].astype(BF16), xa_w_o[layer].astype(BF16),
                   _row(g_xattn_post[layer]))
        h = _ffn(h, _row(g_ffn_pre[layer]), ffn_w1[layer].astype(BF16), ffn_w2[layer].astype(BF16),
                 _row(g_ffn_post[layer]))
    return h.reshape(batch, seq, d)
```

```python
import functools

import numpy as np
import jax
import jax.numpy as jnp
from jax import lax
from jax.experimental import pallas as pl
from jax.experimental.pallas import tpu as pltpu

F32 = jnp.float32
BF16 = jnp.bfloat16

D_MODEL = 1024
DEPTH = 4
EPS = 1e-6
MLA_HEADS = 8
MLA_NOPE = 64
MLA_ROPE = 32
MLA_V = 64
MLA_Q_RANK = 256
MLA_KV_RANK = 128
ROPE_THETA = 10000.0
GLA_HEADS = 4
GLA_DK = 64
GLA_DV = 128
GLA_GATE_RANK = 16
GLA_TAU = 16.0
MLSTM_HEADS = 4
MLSTM_DH = 256
MLSTM_CONV = 4
XATTN_HEADS = 4
XATTN_DH = 256
D_FF = 4096
EVEN_SPLITS = (256, 128, 32, 256, 256, 512, 16, 512)

V7X_LANES = 128
V7X_VMEM_BYTES = 64 * 1024 * 1024
VMEM_LIMIT = (V7X_VMEM_BYTES * 7) // 8

ROW_TILE = 512
FF_CHUNK = 1024
ATTN_TK = 512
ATTN_HEADS_PER_STEP = 2
SCAN_CHUNK = 128
GLA_CHUNKS_PER_STEP = 4
HEAD_PAD = V7X_LANES
ATTN_VROWS = MLA_V + 16

LOG2E = 1.4426950408889634
NEG_BIG = -1e30


def _cparams(sem):
    return pltpu.CompilerParams(dimension_semantics=sem, vmem_limit_bytes=VMEM_LIMIT)


def _resident(shape):
    nd = len(shape)
    return pl.BlockSpec(shape, lambda *_: (0,) * nd, pipeline_mode=pl.Buffered(1))


def _rows(tile, width):
    return pl.BlockSpec((tile, width), lambda i: (i, 0))


def _rms(x, g):
    return x * lax.rsqrt(jnp.mean(x * x, axis=-1, keepdims=True) + EPS) * g


def _silu(x):
    return x * (1.0 / (1.0 + jnp.exp(-x)))


def _log_sigmoid(x):
    return jnp.minimum(x, 0.0) - jnp.log(1.0 + jnp.exp(-jnp.abs(x)))


def _dot(a, b):
    return jnp.dot(a, b, preferred_element_type=F32)


def _dot_nt(a, b):
    return lax.dot_general(a, b, (((1,), (1,)), ((), ())), preferred_element_type=F32)


def _dot_tn(a, b):
    return lax.dot_general(a, b, (((0,), (0,)), ((), ())), preferred_element_type=F32)


def _split2(x):
    x1 = x.astype(BF16)
    return x1, (x - x1.astype(F32)).astype(BF16)


def _split3(x):
    x1 = x.astype(BF16)
    r = x - x1.astype(F32)
    x2 = r.astype(BF16)
    x3 = (r - x2.astype(F32)).astype(BF16)
    return x1, x2, x3


def _rope_body(pos_ref, freq_ref, cos_ref, sin_ref):
    ang = pos_ref[...] * freq_ref[...]
    lane = lax.broadcasted_iota(jnp.int32, ang.shape, 1)
    on = jnp.logical_and(lane >= MLA_NOPE, lane < MLA_NOPE + MLA_ROPE)
    cos_ref[...] = jnp.where(on, jnp.cos(ang), 0.0)
    sin_ref[...] = jnp.where(on, jnp.sin(ang), 0.0)


def _rope_tables(positions):
    s = positions.shape[-1]
    pos = positions.astype(F32).reshape(s, 1)
    inv_freq = ROPE_THETA ** (-jnp.arange(0, MLA_ROPE, 2, dtype=F32) / MLA_ROPE)
    freq = jnp.concatenate([jnp.zeros((MLA_NOPE,), F32), inv_freq, inv_freq,
                            jnp.zeros((HEAD_PAD - MLA_NOPE - MLA_ROPE,), F32)]).reshape(1, HEAD_PAD)
    tile = min(2048, s)
    return pl.pallas_call(
        _rope_body,
        grid=(s // tile,),
        in_specs=[_rows(tile, 1), _resident((1, HEAD_PAD))],
        out_specs=[_rows(tile, HEAD_PAD), _rows(tile, HEAD_PAD)],
        out_shape=[jax.ShapeDtypeStruct((s, HEAD_PAD), F32)] * 2,
        compiler_params=_cparams(("parallel",)),
        name="rope_tables",
    )(pos, freq)


_EV_OFF = np.cumsum((0, 256, 128, 128, 128, 256, 256, 512, 128, 512))


def _even_proj_body(h_ref, gpre_ref, win_ref, gq_ref, wuq_ref, gkv_ref, wuk_ref, wuvt_ref, wgate_ref, bgate_ref,
                    cos_ref, sin_ref,
                    q_ref, k_ref, vt_ref, gq_out, gk_out, gv_out, la_out, r_out):
    o = _EV_OFF
    hn = _rms(h_ref[...], gpre_ref[...]).astype(BF16)
    proj = _dot(hn, win_ref[...])
    c_q = proj[:, o[0]:o[1]]
    c_kv = proj[:, o[1]:o[2]]
    kpe_a = proj[:, o[2]:o[3]]
    kpe_b = proj[:, o[3]:o[4]]
    cosk = cos_ref[...]
    sink = sin_ref[...]
    nh = MLA_HEADS
    width = nh * HEAD_PAD

    qs = (MLA_NOPE + MLA_ROPE) ** -0.5 * LOG2E
    lane = lax.broadcasted_iota(jnp.int32, cosk.shape, 1)
    cosq = qs * jnp.where(lane < MLA_NOPE, 1.0, cosk)
    sinq = qs * sink
    cqn = _rms(c_q, gq_ref[...]).astype(BF16)
    qab = _dot(cqn, wuq_ref[...])
    q = qab[:, :width] * jnp.tile(cosq, (1, nh)) + qab[:, width:] * jnp.tile(sinq, (1, nh))
    q_ref[...] = q.astype(BF16)

    ckvn = _rms(c_kv, gkv_ref[...]).astype(BF16)
    kpe = kpe_a * cosk + kpe_b * sink
    k_ref[...] = (_dot(ckvn, wuk_ref[...]) + jnp.tile(kpe, (1, nh))).astype(BF16)
    vt = _dot_nt(wuvt_ref[...], ckvn).astype(BF16)
    row = lax.broadcasted_iota(jnp.int32, (ATTN_VROWS - MLA_V, vt.shape[1]), 0)
    extra = jnp.where(row == 0, 1.0, 0.0).astype(BF16)
    for hd in range(nh):
        vt_ref[hd, 0:MLA_V, :] = vt[hd * MLA_V:(hd + 1) * MLA_V]
        vt_ref[hd, MLA_V:ATTN_VROWS, :] = extra

    gq_out[...] = proj[:, o[4]:o[5]]
    gk_out[...] = proj[:, o[5]:o[6]]
    gv_out[...] = proj[:, o[6]:o[7]]
    glr = proj[:, o[7]:o[8]].astype(BF16)
    x = _dot(glr, wgate_ref[...]) + bgate_ref[...]
    la_out[...] = _log_sigmoid(x) * (1.0 / GLA_TAU)
    r_out[...] = proj[:, o[8]:o[9]]


def _even_proj(h, g_pre, w_in, g_q, w_uq, g_kv, w_uk, w_uvt, w_gate, b_gate, cos_t, sin_t):
    s = h.shape[0]
    tm = min(ROW_TILE, s)
    width = MLA_HEADS * HEAD_PAD
    gdk = GLA_HEADS * GLA_DK
    gdv = GLA_HEADS * GLA_DV
    rows = [(width, BF16), (width, BF16), None, (gdk, F32), (gdk, F32), (gdv, F32), (gdk, F32), (gdv, F32)]
    vt_spec = pl.BlockSpec((MLA_HEADS, None, ATTN_VROWS, tm), lambda i: (0, i, 0, 0))
    vt_shape = jax.ShapeDtypeStruct((MLA_HEADS, s // tm, ATTN_VROWS, tm), BF16)
    return pl.pallas_call(
        _even_proj_body,
        grid=(s // tm,),
        in_specs=[_rows(tm, D_MODEL), _resident(g_pre.shape), _resident(w_in.shape), _resident(g_q.shape),
                  _resident(w_uq.shape), _resident(g_kv.shape), _resident(w_uk.shape), _resident(w_uvt.shape),
                  _resident(w_gate.shape), _resident(b_gate.shape), _rows(tm, HEAD_PAD), _rows(tm, HEAD_PAD)],
        out_specs=[vt_spec if r is None else _rows(tm, r[0]) for r in rows],
        out_shape=[vt_shape if r is None else jax.ShapeDtypeStruct((s, r[0]), r[1]) for r in rows],
        compiler_params=_cparams(("parallel",)),
        name="even_proj",
    )(h, g_pre, w_in, g_q, w_uq, g_kv, w_uk, w_uvt, w_gate, b_gate, cos_t, sin_t)


def _prep_even_weights(w_in, w_uq, w_ukv, w_gate, b_gate):
    d = w_in.shape[0]
    off = np.cumsum((0,) + EVEN_SPLITS)
    seg = [w_in[:, off[i]:off[i + 1]] for i in range(len(EVEN_SPLITS))]
    c_q, c_kv, k_pe, gq, gk, gv, glr, r = seg
    half = MLA_ROPE // 2
    z = lambda n: jnp.zeros((d, n), F32)
    pad = HEAD_PAD - MLA_NOPE - MLA_ROPE
    kpe_a = jnp.concatenate([z(MLA_NOPE), k_pe, z(pad)], axis=1)
    kpe_b = jnp.concatenate([z(MLA_NOPE), -k_pe[:, half:], k_pe[:, :half], z(pad)], axis=1)
    glr_p = jnp.concatenate([glr, z(HEAD_PAD - GLA_GATE_RANK)], axis=1)
    w_in_p = jnp.concatenate([c_q, c_kv, kpe_a, kpe_b, gq, gk, gv, glr_p, r], axis=1).astype(BF16)

    nh = MLA_HEADS
    wq3 = w_uq.reshape(MLA_Q_RANK, nh, MLA_NOPE + MLA_ROPE)
    nope, rope = wq3[..., :MLA_NOPE], wq3[..., MLA_NOPE:]
    zq = lambda n: jnp.zeros((MLA_Q_RANK, nh, n), F32)
    wa = jnp.concatenate([nope, rope, zq(pad)], axis=-1).reshape(MLA_Q_RANK, nh * HEAD_PAD)
    wb = jnp.concatenate([zq(MLA_NOPE), -rope[..., half:], rope[..., :half], zq(pad)],
                         axis=-1).reshape(MLA_Q_RANK, nh * HEAD_PAD)
    w_uq_p = jnp.concatenate([wa, wb], axis=1).astype(BF16)

    wkv3 = w_ukv.reshape(MLA_KV_RANK, nh, MLA_NOPE + MLA_V)
    zk = jnp.zeros((MLA_KV_RANK, nh, HEAD_PAD - MLA_NOPE), F32)
    w_uk_p = jnp.concatenate([wkv3[..., :MLA_NOPE], zk], axis=-1).reshape(MLA_KV_RANK, nh * HEAD_PAD).astype(BF16)
    w_uvt_p = wkv3[..., MLA_NOPE:].reshape(MLA_KV_RANK, nh * MLA_V).T.astype(BF16)

    w_gate_p = jnp.concatenate([w_gate, jnp.zeros((HEAD_PAD - GLA_GATE_RANK, w_gate.shape[1]), F32)],
                               axis=0).astype(BF16)
    return w_in_p, w_uq_p, w_uk_p, w_uvt_p, w_gate_p, b_gate.reshape(1, -1)


def _mla_attn_body(q_ref, k_ref, vt_ref, o_ref, acc_scr, st_scr, p_scr, m_scr, alpha_scr, *, tq, tk, hg):
    i = pl.program_id(1)
    heads = range(hg)

    def scores(t, slot):
        start = pl.multiple_of(t * tk, tk)
        for hd in heads:
            lanes = slice(hd * HEAD_PAD, (hd + 1) * HEAD_PAD)
            st_scr[slot, hd] = _dot_nt(k_ref[pl.ds(start, tk), lanes], q_ref[:, lanes])

    def accumulate(t, slot):
        for hd in heads:
            acc_scr[hd] = alpha_scr[slot, hd] * acc_scr[hd] + _dot(vt_ref[hd, t], p_scr[slot, hd])

    def softmax(slot, mask_shift):
        for hd in heads:
            st = st_scr[slot, hd]
            if mask_shift is not None:
                key = lax.broadcasted_iota(jnp.int32, st.shape, 0) + mask_shift
                qry = lax.broadcasted_iota(jnp.int32, st.shape, 1)
                st = jnp.where(key <= qry, st, NEG_BIG)
            m_prev = m_scr[hd]
            m_next = jnp.maximum(m_prev, jnp.max(st, axis=0, keepdims=True))
            p_scr[slot, hd] = jnp.exp2(st - m_next).astype(BF16)
            alpha_scr[slot, hd] = jnp.exp2(m_prev - m_next)
            m_scr[hd] = m_next

    def step(t, slot, mask_shift, more):
        softmax(slot, mask_shift)
        if more:
            scores(t + 1, 1 - slot)
        accumulate(jnp.maximum(t - 1, 0), 1 - slot)

    acc_scr[...] = jnp.zeros(acc_scr.shape, F32)
    p_scr[1] = jnp.zeros(p_scr.shape[1:], BF16)
    alpha_scr[1] = jnp.ones(alpha_scr.shape[1:], F32)
    m_scr[...] = jnp.full(m_scr.shape, NEG_BIG, F32)
    scores(0, 0)

    def pair(u, c):
        step(2 * u, 0, None, True)
        step(2 * u + 1, 1, None, True)
        return c

    lax.fori_loop(0, i, pair, 0)
    step(2 * i, 0, 0, True)
    step(2 * i + 1, 1, tk, False)
    accumulate(2 * i + 1, 1)
    for hd in heads:
        acc = acc_scr[hd]
        o_ref[hd] = (acc[:MLA_V] * (1.0 / acc[MLA_V:MLA_V + 1])).astype(o_ref.dtype)


def _mla_attn(q, k, vt):
    s = q.shape[0]
    nk, tk = vt.shape[1], vt.shape[3]
    tq = 2 * tk
    hg = ATTN_HEADS_PER_STEP
    out = pl.pallas_call(
        functools.partial(_mla_attn_body, tq=tq, tk=tk, hg=hg),
        grid=(MLA_HEADS // hg, s // tq),
        in_specs=[pl.BlockSpec((tq, hg * HEAD_PAD), lambda g, i: (i, g)),
                  pl.BlockSpec((s, hg * HEAD_PAD), lambda g, i: (0, g), pipeline_mode=pl.Buffered(1)),
                  pl.BlockSpec((hg, nk, ATTN_VROWS, tk), lambda g, i: (g, 0, 0, 0), pipeline_mode=pl.Buffered(1))],
        out_specs=pl.BlockSpec((hg, MLA_V, tq), lambda g, i: (g, 0, i)),
        out_shape=jax.ShapeDtypeStruct((MLA_HEADS, MLA_V, s), BF16),
        scratch_shapes=[pltpu.VMEM((hg, ATTN_VROWS, tq), F32),
                        pltpu.VMEM((2, hg, tk, tq), F32),
                        pltpu.VMEM((2, hg, tk, tq), BF16),
                        pltpu.VMEM((hg, 1, tq), F32),
                        pltpu.VMEM((2, hg, 1, tq), F32)],
        compiler_params=_cparams(("parallel", "arbitrary")),
        name="mla_attn",
    )(q, k, vt)
    return out.reshape(MLA_HEADS * MLA_V, s)


def _gla_levels(chunk):
    n = 0
    while (1 << n) < chunk:
        n += 1
    return n


def _gla_constants(chunk):
    t = np.arange(chunk)
    mats = [t[None, :] <= t[:, None]]
    masks = []
    b = 1
    while b < chunk:
        blk = t // (2 * b)
        upper = (t % (2 * b)) >= b
        e = blk * 2 * b + b - 1
        up_rows = upper[:, None] & (t[None, :] > e[:, None]) & (t[None, :] <= t[:, None])
        lo_rows = (~upper)[:, None] & (t[None, :] > t[:, None]) & (t[None, :] <= e[:, None])
        mats.append(up_rows | lo_rows)
        masks.append(upper[:, None] & (~upper)[None, :] & (blk[:, None] == blk[None, :]))
        b *= 2
    masks.append(t[:, None] == t[None, :])
    mats = np.concatenate(mats, axis=0).astype(np.float32)
    masks = np.stack(masks).astype(np.float32)
    masks = np.tile(masks, (1, GLA_HEADS, 1))
    return jnp.asarray(mats, BF16), jnp.asarray(masks, F32)


def _gla_body(q_ref, k_ref, v_ref, g_ref, mats_ref, masks_ref, o_ref, st_ref, *, chunk, n_sub):
    nlev = _gla_levels(chunk)
    nh, dk, dv = GLA_HEADS, GLA_DK, GLA_DV
    w = nh * dk

    @pl.when(pl.program_id(0) == 0)
    def _():
        st_ref[...] = jnp.zeros(st_ref.shape, F32)

    head_of_lane = lax.broadcasted_iota(jnp.int32, (chunk, w), 1) // dk

    def stack_heads(x):
        return jnp.concatenate([jnp.where(head_of_lane == hd, x, 0.0) for hd in range(nh)], axis=0).astype(BF16)

    pending = []
    for c in range(n_sub):
        rows = slice(c * chunk, (c + 1) * chunk)
        q = q_ref[rows, :] * (dk ** -0.5)
        k = k_ref[rows, :]
        g = g_ref[rows, :]
        e_all = _dot(mats_ref[...], jnp.concatenate(_split2(g), axis=1))
        e_all = e_all[:, :w] + e_all[:, w:]
        b = e_all[:chunk]
        b_end = b[chunk - 1:chunk]
        attn = _dot_nt(stack_heads(q), k.astype(BF16)) * masks_ref[nlev]
        for lv in range(nlev):
            dec = jnp.exp(e_all[(1 + lv) * chunk:(2 + lv) * chunk])
            attn = attn + _dot_nt(stack_heads(q * dec), (k * dec).astype(BF16)) * masks_ref[lv]
        attn = attn.astype(BF16)
        for hd in range(nh):
            vh = v_ref[rows, hd * dv:(hd + 1) * dv].astype(BF16)
            o_ref[rows, hd * dv:(hd + 1) * dv] = _dot(attn[hd * chunk:(hd + 1) * chunk], vh)
        pending.append((stack_heads(q * jnp.exp(b)), (k * jnp.exp(b_end - b)).astype(BF16), jnp.exp(b_end)))

    st = st_ref[...]
    head_of_state_lane = lax.broadcasted_iota(jnp.int32, st.shape, 1) // dk
    for c in range(n_sub):
        rows = slice(c * chunk, (c + 1) * chunk)
        qg, kg, dec_end = pending[c]
        inter = _dot_nt(qg, st.astype(BF16))
        new = st * dec_end
        for hd in range(nh):
            cols = slice(hd * dv, (hd + 1) * dv)
            o_ref[rows, cols] = o_ref[rows, cols] + inter[hd * chunk:(hd + 1) * chunk]
            upd = _dot_tn(v_ref[rows, cols].astype(BF16), kg)
            new = new + jnp.where(head_of_state_lane == hd, upd, 0.0)
        st = new
    st_ref[...] = st


def _gla(gq, gk, gv, log_a):
    s = gq.shape[0]
    chunk = min(SCAN_CHUNK, s)
    n_sub = min(GLA_CHUNKS_PER_STEP, s // chunk)
    rows = chunk * n_sub
    mats, masks = _gla_constants(chunk)
    gdk = GLA_HEADS * GLA_DK
    gdv = GLA_HEADS * GLA_DV
    return pl.pallas_call(
        functools.partial(_gla_body, chunk=chunk, n_sub=n_sub),
        grid=(s // rows,),
        in_specs=[_rows(rows, gdk), _rows(rows, gdk), _rows(rows, gdv), _rows(rows, gdk),
                  _resident(mats.shape), _resident(masks.shape)],
        out_specs=_rows(rows, gdv),
        out_shape=jax.ShapeDtypeStruct((s, gdv), F32),
        scratch_shapes=[pltpu.VMEM((GLA_DV, gdk), F32)],
        compiler_params=_cparams(("arbitrary",)),
        name="gla_scan",
    )(gq, gk, gv, log_a, mats, masks)


def _even_out_body(h_ref, at_ref, o_ref, r_ref, ggla_ref, wa_ref, wg_ref, gpost_ref, out_ref):
    og = o_ref[...]
    gn = jnp.concatenate([_rms(og[:, hd * GLA_DV:(hd + 1) * GLA_DV], ggla_ref[...]) for hd in range(GLA_HEADS)],
                         axis=1)
    g = (gn * _silu(r_ref[...])).astype(BF16)
    mix = _dot_tn(at_ref[...], wa_ref[...]) + _dot(g, wg_ref[...])
    out_ref[...] = h_ref[...] + _rms(mix, gpost_ref[...])


def _even_out(h, a_t, o_gla, r, g_gla, w_a, w_g, g_post):
    s = h.shape[0]
    tm = min(ROW_TILE, s)
    return pl.pallas_call(
        _even_out_body,
        grid=(s // tm,),
        in_specs=[_rows(tm, D_MODEL), pl.BlockSpec((a_t.shape[0], tm), lambda i: (0, i)),
                  _rows(tm, o_gla.shape[1]), _rows(tm, r.shape[1]),
                  _resident(g_gla.shape), _resident(w_a.shape), _resident(w_g.shape), _resident(g_post.shape)],
        out_specs=_rows(tm, D_MODEL),
        out_shape=jax.ShapeDtypeStruct((s, D_MODEL), F32),
        compiler_params=_cparams(("parallel",)),
        name="even_out",
    )(h, a_t, o_gla, r, g_gla, w_a, w_g, g_post)


CONV_HALO = 8


def _odd_proj_body(h_ref, gpre_ref, win_ref, cw_ref, cb_ref, wq_ref, wk_ref, wkt_ref, wv_ref, wgq_ref, wgk_ref,
                   wgv_ref, bg_ref, q_out, k_out, kt_out, v_out, gates_out, xc_out, z_out, xbuf):
    tm = h_ref.shape[0]
    width = D_MODEL

    @pl.when(pl.program_id(0) == 0)
    def _():
        xbuf[0:CONV_HALO, :] = jnp.zeros((CONV_HALO, width), F32)

    hn = _rms(h_ref[...], gpre_ref[...]).astype(BF16)
    xz = _dot(hn, win_ref[...])
    x_m = xz[:, :width]
    z_out[...] = xz[:, width:]
    xbuf[CONV_HALO:CONV_HALO + tm, :] = x_m
    conv = cb_ref[...] + cw_ref[MLSTM_CONV - 1:MLSTM_CONV, :] * x_m
    for j in range(MLSTM_CONV - 1):
        back = MLSTM_CONV - 1 - j
        conv = conv + cw_ref[j:j + 1, :] * xbuf[CONV_HALO - back:CONV_HALO - back + tm, :]
    xbuf[0:CONV_HALO, :] = x_m[tm - CONV_HALO:, :]
    x_c = _silu(conv)
    xc_out[...] = x_c
    xcb = x_c.astype(BF16)

    dh = MLSTM_DH
    xmb = x_m.astype(BF16)
    qs, ks, vs = [], [], []
    for hd in range(MLSTM_HEADS):
        sl = slice(hd * dh, (hd + 1) * dh)
        qs.append(_dot(xcb[:, sl], wq_ref[hd]))
        ks.append(_dot(xcb[:, sl], wk_ref[hd]))
        vs.append(_dot(xmb[:, sl], wv_ref[hd]))
        kt_out[sl, :] = _dot_nt(wkt_ref[hd], xcb[:, sl]).astype(BF16)
    q = jnp.concatenate(qs, axis=1).astype(BF16)
    k = jnp.concatenate(ks, axis=1).astype(BF16)
    v = jnp.concatenate(vs, axis=1).astype(BF16)
    gates_out[...] = _dot(q, wgq_ref[...]) + _dot(k, wgk_ref[...]) + _dot(v, wgv_ref[...]) + bg_ref[...]
    q_out[...] = (q.astype(F32) * (dh ** -0.5)).astype(BF16)
    k_out[...] = k
    v_out[...] = v


def _odd_proj(h, g_pre, w_in, conv_w, conv_b, w_q, w_k, w_kt, w_v, wg_q, wg_k, wg_v, b_g):
    s = h.shape[0]
    tm = min(ROW_TILE, s)
    rows = [(D_MODEL, BF16), (D_MODEL, BF16), None, (D_MODEL, BF16), (V7X_LANES, F32), (D_MODEL, F32),
            (D_MODEL, F32)]
    ins = (g_pre, w_in, conv_w, conv_b, w_q, w_k, w_kt, w_v, wg_q, wg_k, wg_v, b_g)
    kt_spec = pl.BlockSpec((D_MODEL, tm), lambda i: (0, i))
    kt_shape = jax.ShapeDtypeStruct((D_MODEL, s), BF16)
    return pl.pallas_call(
        _odd_proj_body,
        grid=(s // tm,),
        in_specs=[_rows(tm, D_MODEL)] + [_resident(a.shape) for a in ins],
        out_specs=[kt_spec if r is None else _rows(tm, r[0]) for r in rows],
        out_shape=[kt_shape if r is None else jax.ShapeDtypeStruct((s, r[0]), r[1]) for r in rows],
        scratch_shapes=[pltpu.VMEM((CONV_HALO + tm, D_MODEL), F32)],
        compiler_params=_cparams(("arbitrary",)),
        name="odd_proj",
    )(h, *ins)


def _prep_odd_gate_weights(w_gates, b_gates):
    nh, dh = MLSTM_HEADS, MLSTM_DH
    w4 = w_gates.reshape(nh, 3, dh, 2 * nh)
    pad = jnp.zeros((nh * dh, V7X_LANES - 2 * nh), F32)
    parts = [jnp.concatenate([w4[:, j].reshape(nh * dh, 2 * nh), pad], axis=1).astype(BF16) for j in range(3)]
    b = jnp.concatenate([b_gates, jnp.zeros((V7X_LANES - 2 * nh,), F32)]).reshape(1, V7X_LANES)
    return parts[0], parts[1], parts[2], b


MLSTM_EXT = MLSTM_DH + V7X_LANES


def _mlstm_body(q_ref, k_ref, kt_ref, v_ref, gcol_ref, grow_ref, tri_ref, trit_ref, o_ref, c_scr, m_scr, *, chunk):
    nh, dh = MLSTM_HEADS, MLSTM_DH
    lanes = V7X_LANES

    @pl.when(pl.program_id(0) == 0)
    def _():
        c_scr[...] = jnp.zeros(c_scr.shape, F32)
        m_scr[...] = jnp.zeros(m_scr.shape, F32)

    gc = gcol_ref[...]
    b_col = _dot(tri_ref[...], jnp.concatenate(_split3(_log_sigmoid(gc)), axis=1))
    b_col = b_col[:, :lanes] + b_col[:, lanes:2 * lanes] + b_col[:, 2 * lanes:]
    gr = grow_ref[0]
    nr = gr.shape[0]
    b_row = _dot(jnp.concatenate(_split3(_log_sigmoid(gr)), axis=0), trit_ref[...])
    b_row = b_row[:nr] + b_row[nr:2 * nr] + b_row[2 * nr:]

    row = lax.broadcasted_iota(jnp.int32, (chunk, chunk), 0)
    col = lax.broadcasted_iota(jnp.int32, (chunk, chunk), 1)
    causal = col <= row
    ones_col = jnp.where(lax.broadcasted_iota(jnp.int32, (chunk, lanes), 1) == 0, 1.0, 0.0).astype(BF16)
    for hd in range(nh):
        sl = slice(hd * dh, (hd + 1) * dh)
        r_h = gr[hd:hd + 1, :] - b_row[nh + hd:nh + hd + 1, :]
        b_h = b_col[:, nh + hd:nh + hd + 1]
        m_old = m_scr[hd:hd + 1, 0:1]
        qh = q_ref[:, sl]
        vext = jnp.concatenate([v_ref[:, sl], ones_col], axis=1)
        cext = c_scr[hd]

        logw = jnp.where(causal, r_h, NEG_BIG)
        big_m = jnp.maximum(jnp.max(logw, axis=-1, keepdims=True), m_old)
        w_intra = jnp.exp(logw - big_m) * _dot_nt(qh, k_ref[:, sl])
        w_inter = jnp.exp(m_old - big_m)
        out = (_dot(w_intra.astype(BF16), vext)
               + _dot((qh.astype(F32) * w_inter).astype(BF16), cext.astype(BF16)))
        den = out[:, dh:dh + 1]
        o_ref[:, sl] = out[:, :dh] / jnp.maximum(jnp.abs(den), jnp.exp(-(b_h + big_m)))

        m_last = big_m[chunk - 1:chunk, :]
        w_s = jnp.exp(r_h - m_last)
        ktw = (kt_ref[sl, :].astype(F32) * w_s).astype(BF16)
        c_scr[hd] = jnp.exp(m_old - m_last) * cext + _dot(ktw, vext)
        m_scr[hd:hd + 1, :] = jnp.broadcast_to(b_h[chunk - 1:chunk, :] + m_last, (1, m_scr.shape[1]))


def _mlstm(q, k, kt, v, gates):
    s = q.shape[0]
    chunk = min(SCAN_CHUNK, s)
    nc = s // chunk
    lanes = V7X_LANES
    ng = 2 * MLSTM_HEADS
    grow = gates[:, :ng].reshape(nc, chunk, ng).transpose(0, 2, 1)
    t = np.arange(chunk)
    tri = (t[None, :] <= t[:, None]).astype(np.float32)
    return pl.pallas_call(
        functools.partial(_mlstm_body, chunk=chunk),
        grid=(nc,),
        in_specs=[_rows(chunk, D_MODEL), _rows(chunk, D_MODEL), pl.BlockSpec((D_MODEL, chunk), lambda c: (0, c)),
                  _rows(chunk, D_MODEL), _rows(chunk, lanes),
                  pl.BlockSpec((1, ng, chunk), lambda c: (c, 0, 0)),
                  _resident((chunk, chunk)), _resident((chunk, chunk))],
        out_specs=_rows(chunk, D_MODEL),
        out_shape=jax.ShapeDtypeStruct((s, D_MODEL), F32),
        scratch_shapes=[pltpu.VMEM((MLSTM_HEADS, MLSTM_DH, MLSTM_EXT), F32),
                        pltpu.VMEM((8, lanes), F32)],
        compiler_params=_cparams(("arbitrary",)),
        name="mlstm_scan",
    )(q, k, kt, v, gates, grow, jnp.asarray(tri, BF16), jnp.asarray(tri.T, BF16))


def _odd_out_body(h_ref, hc_ref, xc_ref, z_ref, gh_ref, skip_ref, wo_ref, gpost_ref, out_ref):
    hc = hc_ref[...]
    dh = MLSTM_DH
    parts = []
    for hd in range(MLSTM_HEADS):
        seg = hc[:, hd * dh:(hd + 1) * dh]
        cen = seg - jnp.mean(seg, axis=-1, keepdims=True)
        parts.append(cen * lax.rsqrt(jnp.mean(cen * cen, axis=-1, keepdims=True) + EPS))
    hn = jnp.concatenate(parts, axis=1) * gh_ref[...]
    out = ((hn + skip_ref[...] * xc_ref[...].astype(F32)) * _silu(z_ref[...].astype(F32))).astype(BF16)
    out_ref[...] = h_ref[...] + _rms(_dot(out, wo_ref[...]), gpost_ref[...])


def _odd_out(h, hcell, x_c, z, g_hnorm, skip, w_out, g_post):
    s = h.shape[0]
    tm = min(ROW_TILE, s)
    return pl.pallas_call(
        _odd_out_body,
        grid=(s // tm,),
        in_specs=[_rows(tm, D_MODEL)] * 4 + [_resident(g_hnorm.shape), _resident(skip.shape),
                                              _resident(w_out.shape), _resident(g_post.shape)],
        out_specs=_rows(tm, D_MODEL),
        out_shape=jax.ShapeDtypeStruct((s, D_MODEL), F32),
        compiler_params=_cparams(("parallel",)),
        name="odd_out",
    )(h, hcell, x_c, z, g_hnorm, skip, w_out, g_post)


def _xattn_body(h_ref, mem_ref, gmem_ref, gpre_ref, wq_ref, wk_ref, wv_ref, wo_ref, gpost_ref, out_ref,
                k_scr, v_scr):
    @pl.when(pl.program_id(0) == 0)
    def _():
        mem_n = _rms(mem_ref[...], gmem_ref[...]).astype(BF16)
        k_scr[...] = _dot(mem_n, wk_ref[...].astype(BF16)).astype(BF16)
        v_scr[...] = _dot(mem_n, wv_ref[...].astype(BF16)).astype(BF16)

    h = h_ref[...]
    hn = _rms(h, gpre_ref[...]).astype(BF16)
    q = _dot(hn, wq_ref[...].astype(BF16)).astype(BF16)
    dh = XATTN_DH
    scale = dh ** -0.5
    outs = []
    for hd in range(XATTN_HEADS):
        sl = slice(hd * dh, (hd + 1) * dh)
        s = _dot_nt(q[:, sl], k_scr[:, sl]) * scale
        p = jnp.exp(s - jnp.max(s, axis=-1, keepdims=True))
        inv_l = 1.0 / jnp.sum(p, axis=-1, keepdims=True)
        outs.append((_dot(p.astype(BF16), v_scr[:, sl]) * inv_l).astype(BF16))
    o = jnp.concatenate(outs, axis=1)
    out_ref[...] = h + _rms(_dot(o, wo_ref[...].astype(BF16)), gpost_ref[...])


def _xattn(h, mem, g_mem, g_pre, w_q, w_k, w_v, w_o, g_post):
    s = h.shape[0]
    tm = min(ROW_TILE, s)
    ins = (mem, g_mem, g_pre, w_q, w_k, w_v, w_o, g_post)
    return pl.pallas_call(
        _xattn_body,
        grid=(s // tm,),
        in_specs=[_rows(tm, D_MODEL)] + [_resident(a.shape) for a in ins],
        out_specs=_rows(tm, D_MODEL),
        out_shape=jax.ShapeDtypeStruct((s, D_MODEL), F32),
        scratch_shapes=[pltpu.VMEM((mem.shape[0], D_MODEL), BF16), pltpu.VMEM((mem.shape[0], D_MODEL), BF16)],
        compiler_params=_cparams(("arbitrary",)),
        name="mem_xattn",
    )(h, *ins)


def _ffn_body(h_ref, gpre_ref, w1_ref, w2_ref, gpost_ref, out_ref):
    h = h_ref[...]
    xn = _rms(h, gpre_ref[...]).astype(BF16)
    acc = None
    for c in range(D_FF // FF_CHUNK):
        sl = slice(c * FF_CHUNK, (c + 1) * FF_CHUNK)
        a = jnp.square(jnp.maximum(_dot(xn, w1_ref[:, sl].astype(BF16)), 0.0)).astype(BF16)
        part = _dot(a, w2_ref[sl, :].astype(BF16))
        acc = part if acc is None else acc + part
    out_ref[...] = h + _rms(acc, gpost_ref[...])


def _ffn(h, g_pre, w1, w2, g_post):
    s = h.shape[0]
    tm = min(ROW_TILE, s)
    return pl.pallas_call(
        _ffn_body,
        grid=(s // tm,),
        in_specs=[_rows(tm, D_MODEL), _resident(g_pre.shape), _resident(w1.shape), _resident(w2.shape),
                  _resident(g_post.shape)],
        out_specs=_rows(tm, D_MODEL),
        out_shape=jax.ShapeDtypeStruct((s, D_MODEL), F32),
        compiler_params=_cparams(("parallel",)),
        name="sq_relu_mlp",
    )(h, g_pre, w1, w2, g_post)


def _row(g):
    return g.reshape(1, -1)


def kernel(x, mem, positions, g_mix_pre, g_mix_post, g_xattn_pre, g_xattn_post, g_mem, g_ffn_pre, g_ffn_post, ev_w_in, ev_g_q, ev_w_uq, ev_g_kv, ev_w_ukv, ev_w_gate, ev_b_gate, ev_g_gla, ev_w_out, od_w_in, od_conv_w, od_conv_b, od_w_q, od_w_k, od_w_v, od_w_gates, od_b_gates, od_g_hnorm, od_skip, od_w_out, xa_w_q, xa_w_k, xa_w_v, xa_w_o, ffn_w1, ffn_w2):
    batch, seq, d = x.shape
    assert batch == 1 and d == D_MODEL and seq % ROW_TILE == 0 and seq % (2 * ATTN_TK) == 0
    assert seq % (SCAN_CHUNK * GLA_CHUNKS_PER_STEP) == 0 and ROW_TILE == ATTN_TK
    h = x.reshape(seq, d)
    mem2 = mem.reshape(mem.shape[1], d)
    cos_t, sin_t = _rope_tables(positions)
    n_a = MLA_HEADS * MLA_V

    for layer in range(DEPTH):
        j = layer // 2
        if layer % 2 == 0:
            w_in_p, w_uq_p, w_uk_p, w_uvt_p, w_gate_p, b_gate = _prep_even_weights(
                ev_w_in[j], ev_w_uq[j], ev_w_ukv[j], ev_w_gate[j], ev_b_gate[j])
            q, k, vt, gq, gk, gv, log_a, r = _even_proj(
                h, _row(g_mix_pre[layer]), w_in_p, _row(ev_g_q[j]), w_uq_p, _row(ev_g_kv[j]), w_uk_p, w_uvt_p,
                w_gate_p, b_gate, cos_t, sin_t)
            a_t = _mla_attn(q, k, vt)
            o_gla = _gla(gq, gk, gv, log_a)
            w_out = ev_w_out[j].astype(BF16)
            h = _even_out(h, a_t, o_gla, r, _row(ev_g_gla[j]), w_out[:n_a], w_out[n_a:], _row(g_mix_post[layer]))
        else:
            wg_q, wg_k, wg_v, b_g = _prep_odd_gate_weights(od_w_gates[j], od_b_gates[j])
            w_k = od_w_k[j].astype(BF16)
            q, k, kt, v, gates, x_c, z = _odd_proj(
                h, _row(g_mix_pre[layer]), od_w_in[j].astype(BF16), od_conv_w[j], _row(od_conv_b[j]),
                od_w_q[j].astype(BF16), w_k, w_k.transpose(0, 2, 1), od_w_v[j].astype(BF16), wg_q, wg_k, wg_v, b_g)
            hcell = _mlstm(q, k, kt, v, gates)
            h = _odd_out(h, hcell, x_c, z, _row(od_g_hnorm[j]), _row(od_skip[j]), od_w_out[j].astype(BF16),
                         _row(g_mix_post[layer]))
        h = _xattn(h, mem2, _row(g_mem[layer]), _row(g_xattn_pre[layer]), xa_w_q[layer], xa_w_k[layer],
                   xa_w_v[layer], xa_w_o[layer], _row(g_xattn_post[layer]))
        h = _ffn(h, _row(g_ffn_pre[layer]), ffn_w1[layer], ffn_w2[layer], _row(g_ffn_post[layer]))
    return h.reshape(batch, seq, d)
```

```python
import functools

import numpy as np
import jax
import jax.numpy as jnp
from jax import lax
from jax.experimental import pallas as pl
from jax.experimental.pallas import tpu as pltpu

F32 = jnp.float32
BF16 = jnp.bfloat16

D_MODEL = 1024
DEPTH = 4
EPS = 1e-6
MLA_HEADS = 8
MLA_NOPE = 64
MLA_ROPE = 32
MLA_V = 64
MLA_Q_RANK = 256
MLA_KV_RANK = 128
ROPE_THETA = 10000.0
GLA_HEADS = 4
GLA_DK = 64
GLA_DV = 128
GLA_GATE_RANK = 16
GLA_TAU = 16.0
MLSTM_HEADS = 4
MLSTM_DH = 256
MLSTM_CONV = 4
XATTN_HEADS = 4
XATTN_DH = 256
D_FF = 4096
EVEN_SPLITS = (256, 128, 32, 256, 256, 512, 16, 512)

V7X_LANES = 128
V7X_VMEM_BYTES = 64 * 1024 * 1024
VMEM_LIMIT = (V7X_VMEM_BYTES * 7) // 8

ROW_TILE = 512
FF_CHUNK = 1024
ATTN_TK = 512
ATTN_HEADS_PER_STEP = 2
SCAN_CHUNK = 128
GLA_CHUNKS_PER_STEP = 4
HEAD_PAD = V7X_LANES
ATTN_VROWS = MLA_V + 16

LOG2E = 1.4426950408889634
NEG_BIG = -1e30


def _cparams(sem):
    return pltpu.CompilerParams(dimension_semantics=sem, vmem_limit_bytes=VMEM_LIMIT)


def _resident(shape):
    nd = len(shape)
    return pl.BlockSpec(shape, lambda *_: (0,) * nd, pipeline_mode=pl.Buffered(1))


def _rows(tile, width):
    return pl.BlockSpec((tile, width), lambda i: (i, 0))


def _rms(x, g):
    return x * lax.rsqrt(jnp.mean(x * x, axis=-1, keepdims=True) + EPS) * g


def _silu(x):
    return x * (1.0 / (1.0 + jnp.exp(-x)))


def _log_sigmoid(x):
    return jnp.minimum(x, 0.0) - jnp.log(1.0 + jnp.exp(-jnp.abs(x)))


def _dot(a, b):
    return jnp.dot(a, b, preferred_element_type=F32)


def _dot_nt(a, b):
    return lax.dot_general(a, b, (((1,), (1,)), ((), ())), preferred_element_type=F32)


def _dot_tn(a, b):
    return lax.dot_general(a, b, (((0,), (0,)), ((), ())), preferred_element_type=F32)


def _split2(x):
    x1 = x.astype(BF16)
    return x1, (x - x1.astype(F32)).astype(BF16)


def _split3(x):
    x1 = x.astype(BF16)
    r = x - x1.astype(F32)
    x2 = r.astype(BF16)
    x3 = (r - x2.astype(F32)).astype(BF16)
    return x1, x2, x3


def _rope_body(pos_ref, freq_ref, cos_ref, sin_ref):
    ang = pos_ref[...] * freq_ref[...]
    lane = lax.broadcasted_iota(jnp.int32, ang.shape, 1)
    on = jnp.logical_and(lane >= MLA_NOPE, lane < MLA_NOPE + MLA_ROPE)
    cos_ref[...] = jnp.where(on, jnp.cos(ang), 0.0)
    sin_ref[...] = jnp.where(on, jnp.sin(ang), 0.0)


def _rope_tables(positions):
    s = positions.shape[-1]
    pos = positions.astype(F32).reshape(s, 1)
    inv_freq = ROPE_THETA ** (-jnp.arange(0, MLA_ROPE, 2, dtype=F32) / MLA_ROPE)
    freq = jnp.concatenate([jnp.zeros((MLA_NOPE,), F32), inv_freq, inv_freq,
                            jnp.zeros((HEAD_PAD - MLA_NOPE - MLA_ROPE,), F32)]).reshape(1, HEAD_PAD)
    tile = min(2048, s)
    return pl.pallas_call(
        _rope_body,
        grid=(s // tile,),
        in_specs=[_rows(tile, 1), _resident((1, HEAD_PAD))],
        out_specs=[_rows(tile, HEAD_PAD), _rows(tile, HEAD_PAD)],
        out_shape=[jax.ShapeDtypeStruct((s, HEAD_PAD), F32)] * 2,
        compiler_params=_cparams(("parallel",)),
        name="rope_tables",
    )(pos, freq)


_EV_OFF = np.cumsum((0, 256, 128, 128, 128, 256, 256, 512, 128, 512))


def _even_proj_body(h_ref, gpre_ref, win_ref, gq_ref, wuq_ref, gkv_ref, wuk_ref, wuvt_ref, wgate_ref, bgate_ref,
                    cos_ref, sin_ref,
                    q_ref, k_ref, vt_ref, gq_out, gk_out, gv_out, la_out, r_out):
    o = _EV_OFF
    hn = _rms(h_ref[...], gpre_ref[...]).astype(BF16)
    proj = _dot(hn, win_ref[...])
    c_q = proj[:, o[0]:o[1]]
    c_kv = proj[:, o[1]:o[2]]
    kpe_a = proj[:, o[2]:o[3]]
    kpe_b = proj[:, o[3]:o[4]]
    cosk = cos_ref[...]
    sink = sin_ref[...]
    nh = MLA_HEADS
    width = nh * HEAD_PAD

    qs = (MLA_NOPE + MLA_ROPE) ** -0.5 * LOG2E
    lane = lax.broadcasted_iota(jnp.int32, cosk.shape, 1)
    cosq = qs * jnp.where(lane < MLA_NOPE, 1.0, cosk)
    sinq = qs * sink
    cqn = _rms(c_q, gq_ref[...]).astype(BF16)
    qab = _dot(cqn, wuq_ref[...])
    q = qab[:, :width] * jnp.tile(cosq, (1, nh)) + qab[:, width:] * jnp.tile(sinq, (1, nh))
    q_ref[...] = q.astype(BF16)

    ckvn = _rms(c_kv, gkv_ref[...]).astype(BF16)
    kpe = kpe_a * cosk + kpe_b * sink
    k_ref[...] = (_dot(ckvn, wuk_ref[...]) + jnp.tile(kpe, (1, nh))).astype(BF16)
    vt = _dot_nt(wuvt_ref[...], ckvn).astype(BF16)
    row = lax.broadcasted_iota(jnp.int32, (ATTN_VROWS - MLA_V, vt.shape[1]), 0)
    extra = jnp.where(row == 0, 1.0, 0.0).astype(BF16)
    for hd in range(nh):
        vt_ref[hd, 0:MLA_V, :] = vt[hd * MLA_V:(hd + 1) * MLA_V]
        vt_ref[hd, MLA_V:ATTN_VROWS, :] = extra

    gq_out[...] = proj[:, o[4]:o[5]]
    gk_out[...] = proj[:, o[5]:o[6]]
    gv_out[...] = proj[:, o[6]:o[7]]
    glr = proj[:, o[7]:o[8]].astype(BF16)
    x = _dot(glr, wgate_ref[...]) + bgate_ref[...]
    la_out[...] = _log_sigmoid(x) * (1.0 / GLA_TAU)
    r_out[...] = proj[:, o[8]:o[9]]


def _even_proj(h, g_pre, w_in, g_q, w_uq, g_kv, w_uk, w_uvt, w_gate, b_gate, cos_t, sin_t):
    s = h.shape[0]
    tm = min(ROW_TILE, s)
    width = MLA_HEADS * HEAD_PAD
    gdk = GLA_HEADS * GLA_DK
    gdv = GLA_HEADS * GLA_DV
    rows = [(width, BF16), (width, BF16), None, (gdk, F32), (gdk, F32), (gdv, F32), (gdk, F32), (gdv, F32)]
    vt_spec = pl.BlockSpec((MLA_HEADS, None, ATTN_VROWS, tm), lambda i: (0, i, 0, 0))
    vt_shape = jax.ShapeDtypeStruct((MLA_HEADS, s // tm, ATTN_VROWS, tm), BF16)
    return pl.pallas_call(
        _even_proj_body,
        grid=(s // tm,),
        in_specs=[_rows(tm, D_MODEL), _resident(g_pre.shape), _resident(w_in.shape), _resident(g_q.shape),
                  _resident(w_uq.shape), _resident(g_kv.shape), _resident(w_uk.shape), _resident(w_uvt.shape),
                  _resident(w_gate.shape), _resident(b_gate.shape), _rows(tm, HEAD_PAD), _rows(tm, HEAD_PAD)],
        out_specs=[vt_spec if r is None else _rows(tm, r[0]) for r in rows],
        out_shape=[vt_shape if r is None else jax.ShapeDtypeStruct((s, r[0]), r[1]) for r in rows],
        compiler_params=_cparams(("parallel",)),
        name="even_proj",
    )(h, g_pre, w_in, g_q, w_uq, g_kv, w_uk, w_uvt, w_gate, b_gate, cos_t, sin_t)


def _prep_even_weights(w_in, w_uq, w_ukv, w_gate, b_gate):
    d = w_in.shape[0]
    off = np.cumsum((0,) + EVEN_SPLITS)
    seg = [w_in[:, off[i]:off[i + 1]] for i in range(len(EVEN_SPLITS))]
    c_q, c_kv, k_pe, gq, gk, gv, glr, r = seg
    half = MLA_ROPE // 2
    z = lambda n: jnp.zeros((d, n), F32)
    pad = HEAD_PAD - MLA_NOPE - MLA_ROPE
    kpe_a = jnp.concatenate([z(MLA_NOPE), k_pe, z(pad)], axis=1)
    kpe_b = jnp.concatenate([z(MLA_NOPE), -k_pe[:, half:], k_pe[:, :half], z(pad)], axis=1)
    glr_p = jnp.concatenate([glr, z(HEAD_PAD - GLA_GATE_RANK)], axis=1)
    w_in_p = jnp.concatenate([c_q, c_kv, kpe_a, kpe_b, gq, gk, gv, glr_p, r], axis=1).astype(BF16)

    nh = MLA_HEADS
    wq3 = w_uq.reshape(MLA_Q_RANK, nh, MLA_NOPE + MLA_ROPE)
    nope, rope = wq3[..., :MLA_NOPE], wq3[..., MLA_NOPE:]
    zq = lambda n: jnp.zeros((MLA_Q_RANK, nh, n), F32)
    wa = jnp.concatenate([nope, rope, zq(pad)], axis=-1).reshape(MLA_Q_RANK, nh * HEAD_PAD)
    wb = jnp.concatenate([zq(MLA_NOPE), -rope[..., half:], rope[..., :half], zq(pad)],
                         axis=-1).reshape(MLA_Q_RANK, nh * HEAD_PAD)
    w_uq_p = jnp.concatenate([wa, wb], axis=1).astype(BF16)

    wkv3 = w_ukv.reshape(MLA_KV_RANK, nh, MLA_NOPE + MLA_V)
    zk = jnp.zeros((MLA_KV_RANK, nh, HEAD_PAD - MLA_NOPE), F32)
    w_uk_p = jnp.concatenate([wkv3[..., :MLA_NOPE], zk], axis=-1).reshape(MLA_KV_RANK, nh * HEAD_PAD).astype(BF16)
    w_uvt_p = wkv3[..., MLA_NOPE:].reshape(MLA_KV_RANK, nh * MLA_V).T.astype(BF16)

    w_gate_p = jnp.concatenate([w_gate, jnp.zeros((HEAD_PAD - GLA_GATE_RANK, w_gate.shape[1]), F32)],
                               axis=0).astype(BF16)
    return w_in_p, w_uq_p, w_uk_p, w_uvt_p, w_gate_p, b_gate.reshape(1, -1)


def _mla_attn_body(q_ref, k_ref, vt_ref, o_ref, acc_scr, st_scr, p_scr, m_scr, alpha_scr, *, tq, tk, hg):
    i = pl.program_id(1)
    heads = range(hg)

    def scores(t, slot):
        start = pl.multiple_of(t * tk, tk)
        for hd in heads:
            lanes = slice(hd * HEAD_PAD, (hd + 1) * HEAD_PAD)
            st_scr[slot, hd] = _dot_nt(k_ref[pl.ds(start, tk), lanes], q_ref[:, lanes])

    def accumulate(t, slot):
        for hd in heads:
            acc_scr[hd] = alpha_scr[slot, hd] * acc_scr[hd] + _dot(vt_ref[hd, t], p_scr[slot, hd])

    def softmax(slot, mask_shift):
        for hd in heads:
            st = st_scr[slot, hd]
            if mask_shift is not None:
                key = lax.broadcasted_iota(jnp.int32, st.shape, 0) + mask_shift
                qry = lax.broadcasted_iota(jnp.int32, st.shape, 1)
                st = jnp.where(key <= qry, st, NEG_BIG)
            m_prev = m_scr[hd]
            m_next = jnp.maximum(m_prev, jnp.max(st, axis=0, keepdims=True))
            p_scr[slot, hd] = jnp.exp2(st - m_next).astype(BF16)
            alpha_scr[slot, hd] = jnp.exp2(m_prev - m_next)
            m_scr[hd] = m_next

    def step(t, slot, mask_shift, more):
        softmax(slot, mask_shift)
        if more:
            scores(t + 1, 1 - slot)
        accumulate(jnp.maximum(t - 1, 0), 1 - slot)

    acc_scr[...] = jnp.zeros(acc_scr.shape, F32)
    p_scr[1] = jnp.zeros(p_scr.shape[1:], BF16)
    alpha_scr[1] = jnp.ones(alpha_scr.shape[1:], F32)
    m_scr[...] = jnp.full(m_scr.shape, NEG_BIG, F32)
    scores(0, 0)

    def pair(u, c):
        step(2 * u, 0, None, True)
        step(2 * u + 1, 1, None, True)
        return c

    lax.fori_loop(0, i, pair, 0)
    step(2 * i, 0, 0, True)
    step(2 * i + 1, 1, tk, False)
    accumulate(2 * i + 1, 1)
    for hd in heads:
        acc = acc_scr[hd]
        o_ref[hd] = (acc[:MLA_V] * (1.0 / acc[MLA_V:MLA_V + 1])).astype(o_ref.dtype)


def _mla_attn(q, k, vt):
    s = q.shape[0]
    nk, tk = vt.shape[1], vt.shape[3]
    tq = 2 * tk
    hg = ATTN_HEADS_PER_STEP
    out = pl.pallas_call(
        functools.partial(_mla_attn_body, tq=tq, tk=tk, hg=hg),
        grid=(MLA_HEADS // hg, s // tq),
        in_specs=[pl.BlockSpec((tq, hg * HEAD_PAD), lambda g, i: (i, g)),
                  pl.BlockSpec((s, hg * HEAD_PAD), lambda g, i: (0, g), pipeline_mode=pl.Buffered(1)),
                  pl.BlockSpec((hg, nk, ATTN_VROWS, tk), lambda g, i: (g, 0, 0, 0), pipeline_mode=pl.Buffered(1))],
        out_specs=pl.BlockSpec((hg, MLA_V, tq), lambda g, i: (g, 0, i)),
        out_shape=jax.ShapeDtypeStruct((MLA_HEADS, MLA_V, s), BF16),
        scratch_shapes=[pltpu.VMEM((hg, ATTN_VROWS, tq), F32),
                        pltpu.VMEM((2, hg, tk, tq), F32),
                        pltpu.VMEM((2, hg, tk, tq), BF16),
                        pltpu.VMEM((hg, 1, tq), F32),
                        pltpu.VMEM((2, hg, 1, tq), F32)],
        compiler_params=_cparams(("parallel", "arbitrary")),
        name="mla_attn",
    )(q, k, vt)
    return out.reshape(MLA_HEADS * MLA_V, s)


def _gla_levels(chunk):
    n = 0
    while (1 << n) < chunk:
        n += 1
    return n


def _gla_constants(chunk):
    t = np.arange(chunk)
    mats = [t[None, :] <= t[:, None]]
    masks = []
    b = 1
    while b < chunk:
        blk = t // (2 * b)
        upper = (t % (2 * b)) >= b
        e = blk * 2 * b + b - 1
        up_rows = upper[:, None] & (t[None, :] > e[:, None]) & (t[None, :] <= t[:, None])
        lo_rows = (~upper)[:, None] & (t[None, :] > t[:, None]) & (t[None, :] <= e[:, None])
        mats.append(up_rows | lo_rows)
        masks.append(upper[:, None] & (~upper)[None, :] & (blk[:, None] == blk[None, :]))
        b *= 2
    masks.append(t[:, None] == t[None, :])
    mats = np.concatenate(mats, axis=0).astype(np.float32)
    masks = np.stack(masks).astype(np.float32)
    masks = np.tile(masks, (1, GLA_HEADS, 1))
    return jnp.asarray(mats, BF16), jnp.asarray(masks, F32)


def _gla_body(q_ref, k_ref, v_ref, g_ref, mats_ref, masks_ref, o_ref, st_ref, *, chunk, n_sub):
    nlev = _gla_levels(chunk)
    nh, dk, dv = GLA_HEADS, GLA_DK, GLA_DV
    w = nh * dk

    @pl.when(pl.program_id(0) == 0)
    def _():
        st_ref[...] = jnp.zeros(st_ref.shape, F32)

    head_of_lane = lax.broadcasted_iota(jnp.int32, (chunk, w), 1) // dk

    def stack_heads(x):
        return jnp.concatenate([jnp.where(head_of_lane == hd, x, 0.0) for hd in range(nh)], axis=0).astype(BF16)

    pending = []
    for c in range(n_sub):
        rows = slice(c * chunk, (c + 1) * chunk)
        q = q_ref[rows, :] * (dk ** -0.5)
        k = k_ref[rows, :]
        g = g_ref[rows, :]
        e_all = _dot(mats_ref[...], jnp.concatenate(_split2(g), axis=1))
        e_all = e_all[:, :w] + e_all[:, w:]
        b = e_all[:chunk]
        b_end = b[chunk - 1:chunk]
        attn = _dot_nt(stack_heads(q), k.astype(BF16)) * masks_ref[nlev]
        for lv in range(nlev):
            dec = jnp.exp(e_all[(1 + lv) * chunk:(2 + lv) * chunk])
            attn = attn + _dot_nt(stack_heads(q * dec), (k * dec).astype(BF16)) * masks_ref[lv]
        attn = attn.astype(BF16)
        for hd in range(nh):
            vh = v_ref[rows, hd * dv:(hd + 1) * dv].astype(BF16)
            o_ref[rows, hd * dv:(hd + 1) * dv] = _dot(attn[hd * chunk:(hd + 1) * chunk], vh)
        pending.append((stack_heads(q * jnp.exp(b)), (k * jnp.exp(b_end - b)).astype(BF16), jnp.exp(b_end)))

    st = st_ref[...]
    head_of_state_lane = lax.broadcasted_iota(jnp.int32, st.shape, 1) // dk
    for c in range(n_sub):
        rows = slice(c * chunk, (c + 1) * chunk)
        qg, kg, dec_end = pending[c]
        inter = _dot_nt(qg, st.astype(BF16))
        new = st * dec_end
        for hd in range(nh):
            cols = slice(hd * dv, (hd + 1) * dv)
            o_ref[rows, cols] = o_ref[rows, cols] + inter[hd * chunk:(hd + 1) * chunk]
            upd = _dot_tn(v_ref[rows, cols].astype(BF16), kg)
            new = new + jnp.where(head_of_state_lane == hd, upd, 0.0)
        st = new
    st_ref[...] = st


def _gla(gq, gk, gv, log_a):
    s = gq.shape[0]
    chunk = min(SCAN_CHUNK, s)
    n_sub = min(GLA_CHUNKS_PER_STEP, s // chunk)
    rows = chunk * n_sub
    mats, masks = _gla_constants(chunk)
    gdk = GLA_HEADS * GLA_DK
    gdv = GLA_HEADS * GLA_DV
    return pl.pallas_call(
        functools.partial(_gla_body, chunk=chunk, n_sub=n_sub),
        grid=(s // rows,),
        in_specs=[_rows(rows, gdk), _rows(rows, gdk), _rows(rows, gdv), _rows(rows, gdk),
                  _resident(mats.shape), _resident(masks.shape)],
        out_specs=_rows(rows, gdv),
        out_shape=jax.ShapeDtypeStruct((s, gdv), F32),
        scratch_shapes=[pltpu.VMEM((GLA_DV, gdk), F32)],
        compiler_params=_cparams(("arbitrary",)),
        name="gla_scan",
    )(gq, gk, gv, log_a, mats, masks)


def _even_out_body(h_ref, at_ref, o_ref, r_ref, ggla_ref, wa_ref, wg_ref, gpost_ref, out_ref):
    og = o_ref[...]
    gn = jnp.concatenate([_rms(og[:, hd * GLA_DV:(hd + 1) * GLA_DV], ggla_ref[...]) for hd in range(GLA_HEADS)],
                         axis=1)
    g = (gn * _silu(r_ref[...])).astype(BF16)
    mix = _dot_tn(at_ref[...], wa_ref[...]) + _dot(g, wg_ref[...])
    out_ref[...] = h_ref[...] + _rms(mix, gpost_ref[...])


def _even_out(h, a_t, o_gla, r, g_gla, w_a, w_g, g_post):
    s = h.shape[0]
    tm = min(ROW_TILE, s)
    return pl.pallas_call(
        _even_out_body,
        grid=(s // tm,),
        in_specs=[_rows(tm, D_MODEL), pl.BlockSpec((a_t.shape[0], tm), lambda i: (0, i)),
                  _rows(tm, o_gla.shape[1]), _rows(tm, r.shape[1]),
                  _resident(g_gla.shape), _resident(w_a.shape), _resident(w_g.shape), _resident(g_post.shape)],
        out_specs=_rows(tm, D_MODEL),
        out_shape=jax.ShapeDtypeStruct((s, D_MODEL), F32),
        compiler_params=_cparams(("parallel",)),
        name="even_out",
    )(h, a_t, o_gla, r, g_gla, w_a, w_g, g_post)


CONV_HALO = 8


def _odd_proj_body(h_ref, gpre_ref, win_ref, cw_ref, cb_ref, wq_ref, wk_ref, wkt_ref, wv_ref, wgq_ref, wgk_ref,
                   wgv_ref, bg_ref, q_out, k_out, kt_out, v_out, gates_out, xc_out, z_out, xbuf):
    tm = h_ref.shape[0]
    width = D_MODEL

    @pl.when(pl.program_id(0) == 0)
    def _():
        xbuf[0:CONV_HALO, :] = jnp.zeros((CONV_HALO, width), F32)

    hn = _rms(h_ref[...], gpre_ref[...]).astype(BF16)
    xz = _dot(hn, win_ref[...])
    x_m = xz[:, :width]
    z_out[...] = xz[:, width:]
    xbuf[CONV_HALO:CONV_HALO + tm, :] = x_m
    conv = cb_ref[...] + cw_ref[MLSTM_CONV - 1:MLSTM_CONV, :] * x_m
    for j in range(MLSTM_CONV - 1):
        back = MLSTM_CONV - 1 - j
        conv = conv + cw_ref[j:j + 1, :] * xbuf[CONV_HALO - back:CONV_HALO - back + tm, :]
    xbuf[0:CONV_HALO, :] = x_m[tm - CONV_HALO:, :]
    x_c = _silu(conv)
    xc_out[...] = x_c
    xcb = x_c.astype(BF16)

    dh = MLSTM_DH
    xmb = x_m.astype(BF16)
    qs, ks, vs = [], [], []
    for hd in range(MLSTM_HEADS):
        sl = slice(hd * dh, (hd + 1) * dh)
        qs.append(_dot(xcb[:, sl], wq_ref[hd]))
        ks.append(_dot(xcb[:, sl], wk_ref[hd]))
        vs.append(_dot(xmb[:, sl], wv_ref[hd]))
        kt_out[sl, :] = _dot_nt(wkt_ref[hd], xcb[:, sl]).astype(BF16)
    q = jnp.concatenate(qs, axis=1).astype(BF16)
    k = jnp.concatenate(ks, axis=1).astype(BF16)
    v = jnp.concatenate(vs, axis=1).astype(BF16)
    gates_out[...] = _dot(q, wgq_ref[...]) + _dot(k, wgk_ref[...]) + _dot(v, wgv_ref[...]) + bg_ref[...]
    q_out[...] = (q.astype(F32) * (dh ** -0.5)).astype(BF16)
    k_out[...] = k
    v_out[...] = v


def _odd_proj(h, g_pre, w_in, conv_w, conv_b, w_q, w_k, w_kt, w_v, wg_q, wg_k, wg_v, b_g):
    s = h.shape[0]
    tm = min(ROW_TILE, s)
    rows = [(D_MODEL, BF16), (D_MODEL, BF16), None, (D_MODEL, BF16), (V7X_LANES, F32), (D_MODEL, F32),
            (D_MODEL, F32)]
    ins = (g_pre, w_in, conv_w, conv_b, w_q, w_k, w_kt, w_v, wg_q, wg_k, wg_v, b_g)
    kt_spec = pl.BlockSpec((D_MODEL, tm), lambda i: (0, i))
    kt_shape = jax.ShapeDtypeStruct((D_MODEL, s), BF16)
    return pl.pallas_call(
        _odd_proj_body,
        grid=(s // tm,),
        in_specs=[_rows(tm, D_MODEL)] + [_resident(a.shape) for a in ins],
        out_specs=[kt_spec if r is None else _rows(tm, r[0]) for r in rows],
        out_shape=[kt_shape if r is None else jax.ShapeDtypeStruct((s, r[0]), r[1]) for r in rows],
        scratch_shapes=[pltpu.VMEM((CONV_HALO + tm, D_MODEL), F32)],
        compiler_params=_cparams(("arbitrary",)),
        name="odd_proj",
    )(h, *ins)


def _prep_odd_gate_weights(w_gates, b_gates):
    nh, dh = MLSTM_HEADS, MLSTM_DH
    w4 = w_gates.reshape(nh, 3, dh, 2 * nh)
    pad = jnp.zeros((nh * dh, V7X_LANES - 2 * nh), F32)
    parts = [jnp.concatenate([w4[:, j].reshape(nh * dh, 2 * nh), pad], axis=1).astype(BF16) for j in range(3)]
    b = jnp.concatenate([b_gates, jnp.zeros((V7X_LANES - 2 * nh,), F32)]).reshape(1, V7X_LANES)
    return parts[0], parts[1], parts[2], b


MLSTM_EXT = MLSTM_DH + V7X_LANES


def _mlstm_body(q_ref, k_ref, kt_ref, v_ref, gcol_ref, grow_ref, tri_ref, trit_ref, o_ref, c_scr, m_scr, *, chunk):
    nh, dh = MLSTM_HEADS, MLSTM_DH
    lanes = V7X_LANES

    @pl.when(pl.program_id(0) == 0)
    def _():
        c_scr[...] = jnp.zeros(c_scr.shape, F32)
        m_scr[...] = jnp.zeros(m_scr.shape, F32)

    gc = gcol_ref[...]
    b_col = _dot(tri_ref[...], jnp.concatenate(_split3(_log_sigmoid(gc)), axis=1))
    b_col = b_col[:, :lanes] + b_col[:, lanes:2 * lanes] + b_col[:, 2 * lanes:]
    gr = grow_ref[0]
    nr = gr.shape[0]
    b_row = _dot(jnp.concatenate(_split3(_log_sigmoid(gr)), axis=0), trit_ref[...])
    b_row = b_row[:nr] + b_row[nr:2 * nr] + b_row[2 * nr:]

    row = lax.broadcasted_iota(jnp.int32, (chunk, chunk), 0)
    col = lax.broadcasted_iota(jnp.int32, (chunk, chunk), 1)
    causal = col <= row
    ones_col = jnp.where(lax.broadcasted_iota(jnp.int32, (chunk, lanes), 1) == 0, 1.0, 0.0).astype(BF16)
    for hd in range(nh):
        sl = slice(hd * dh, (hd + 1) * dh)
        r_h = gr[hd:hd + 1, :] - b_row[nh + hd:nh + hd + 1, :]
        b_h = b_col[:, nh + hd:nh + hd + 1]
        m_old = m_scr[hd:hd + 1, 0:1]
        qh = q_ref[:, sl]
        vext = jnp.concatenate([v_ref[:, sl], ones_col], axis=1)
        cext = c_scr[hd]

        logw = jnp.where(causal, r_h, NEG_BIG)
        big_m = jnp.maximum(jnp.max(logw, axis=-1, keepdims=True), m_old)
        w_intra = jnp.exp(logw - big_m) * _dot_nt(qh, k_ref[:, sl])
        w_inter = jnp.exp(m_old - big_m)
        out = (_dot(w_intra.astype(BF16), vext)
               + _dot((qh.astype(F32) * w_inter).astype(BF16), cext.astype(BF16)))
        den = out[:, dh:dh + 1]
        o_ref[:, sl] = out[:, :dh] / jnp.maximum(jnp.abs(den), jnp.exp(-(b_h + big_m)))

        m_last = big_m[chunk - 1:chunk, :]
        w_s = jnp.exp(r_h - m_last)
        ktw = (kt_ref[sl, :].astype(F32) * w_s).astype(BF16)
        c_scr[hd] = jnp.exp(m_old - m_last) * cext + _dot(ktw, vext)
        m_scr[hd:hd + 1, :] = jnp.broadcast_to(b_h[chunk - 1:chunk, :] + m_last, (1, m_scr.shape[1]))


def _mlstm(q, k, kt, v, gates):
    s = q.shape[0]
    chunk = min(SCAN_CHUNK, s)
    nc = s // chunk
    lanes = V7X_LANES
    ng = 2 * MLSTM_HEADS
    grow = gates[:, :ng].reshape(nc, chunk, ng).transpose(0, 2, 1)
    t = np.arange(chunk)
    tri = (t[None, :] <= t[:, None]).astype(np.float32)
    return pl.pallas_call(
        functools.partial(_mlstm_body, chunk=chunk),
        grid=(nc,),
        in_specs=[_rows(chunk, D_MODEL), _rows(chunk, D_MODEL), pl.BlockSpec((D_MODEL, chunk), lambda c: (0, c)),
                  _rows(chunk, D_MODEL), _rows(chunk, lanes),
                  pl.BlockSpec((1, ng, chunk), lambda c: (c, 0, 0)),
                  _resident((chunk, chunk)), _resident((chunk, chunk))],
        out_specs=_rows(chunk, D_MODEL),
        out_shape=jax.ShapeDtypeStruct((s, D_MODEL), F32),
        scratch_shapes=[pltpu.VMEM((MLSTM_HEADS, MLSTM_DH, MLSTM_EXT), F32),
                        pltpu.VMEM((8, lanes), F32)],
        compiler_params=_cparams(("arbitrary",)),
        name="mlstm_scan",
    )(q, k, kt, v, gates, grow, jnp.asarray(tri, BF16), jnp.asarray(tri.T, BF16))


def _odd_out_body(h_ref, hc_ref, xc_ref, z_ref, gh_ref, skip_ref, wo_ref, gpost_ref, out_ref):
    hc = hc_ref[...]
    dh = MLSTM_DH
    parts = []
    for hd in range(MLSTM_HEADS):
        seg = hc[:, hd * dh:(hd + 1) * dh]
        cen = seg - jnp.mean(seg, axis=-1, keepdims=True)
        parts.append(cen * lax.rsqrt(jnp.mean(cen * cen, axis=-1, keepdims=True) + EPS))
    hn = jnp.concatenate(parts, axis=1) * gh_ref[...]
    out = ((hn + skip_ref[...] * xc_ref[...].astype(F32)) * _silu(z_ref[...].astype(F32))).astype(BF16)
    out_ref[...] = h_ref[...] + _rms(_dot(out, wo_ref[...]), gpost_ref[...])


def _odd_out(h, hcell, x_c, z, g_hnorm, skip, w_out, g_post):
    s = h.shape[0]
    tm = min(ROW_TILE, s)
    return pl.pallas_call(
        _odd_out_body,
        grid=(s // tm,),
        in_specs=[_rows(tm, D_MODEL)] * 4 + [_resident(g_hnorm.shape), _resident(skip.shape),
                                              _resident(w_out.shape), _resident(g_post.shape)],
        out_specs=_rows(tm, D_MODEL),
        out_shape=jax.ShapeDtypeStruct((s, D_MODEL), F32),
        compiler_params=_cparams(("parallel",)),
        name="odd_out",
    )(h, hcell, x_c, z, g_hnorm, skip, w_out, g_post)


def _xattn_body(h_ref, mem_ref, gmem_ref, gpre_ref, wq_ref, wk_ref, wv_ref, wo_ref, gpost_ref, out_ref,
                k_scr, v_scr):
    @pl.when(pl.program_id(0) == 0)
    def _():
        mem_n = _rms(mem_ref[...], gmem_ref[...]).astype(BF16)
        k_scr[...] = _dot(mem_n, wk_ref[...].astype(BF16)).astype(BF16)
        v_scr[...] = _dot(mem_n, wv_ref[...].astype(BF16)).astype(BF16)

    h = h_ref[...]
    hn = _rms(h, gpre_ref[...]).astype(BF16)
    q = _dot(hn, wq_ref[...].astype(BF16)).astype(BF16)
    dh = XATTN_DH
    scale = dh ** -0.5
    outs = []
    for hd in range(XATTN_HEADS):
        sl = slice(hd * dh, (hd + 1) * dh)
        s = _dot_nt(q[:, sl], k_scr[:, sl]) * scale
        p = jnp.exp(s - jnp.max(s, axis=-1, keepdims=True))
        inv_l = 1.0 / jnp.sum(p, axis=-1, keepdims=True)
        outs.append((_dot(p.astype(BF16), v_scr[:, sl]) * inv_l).astype(BF16))
    o = jnp.concatenate(outs, axis=1)
    out_ref[...] = h + _rms(_dot(o, wo_ref[...].astype(BF16)), gpost_ref[...])


def _layer_resident(shape, layer):
    return pl.BlockSpec((None,) + tuple(shape[1:]), lambda *_: (layer, 0, 0), pipeline_mode=pl.Buffered(1))


def _xattn(h, mem, g_mem, g_pre, w_q, w_k, w_v, w_o, g_post, layer):
    s = h.shape[0]
    tm = min(ROW_TILE, s)
    small = (mem, g_mem, g_pre)
    stacks = (w_q, w_k, w_v, w_o)
    return pl.pallas_call(
        _xattn_body,
        grid=(s // tm,),
        in_specs=([_rows(tm, D_MODEL)] + [_resident(a.shape) for a in small]
                  + [_layer_resident(w.shape, layer) for w in stacks] + [_resident(g_post.shape)]),
        out_specs=_rows(tm, D_MODEL),
        out_shape=jax.ShapeDtypeStruct((s, D_MODEL), F32),
        scratch_shapes=[pltpu.VMEM((mem.shape[0], D_MODEL), BF16), pltpu.VMEM((mem.shape[0], D_MODEL), BF16)],
        compiler_params=_cparams(("arbitrary",)),
        name="mem_xattn",
    )(h, *small, *stacks, g_post)


def _ffn_body(h_ref, gpre_ref, w1_ref, w2_ref, gpost_ref, out_ref):
    h = h_ref[...]
    xn = _rms(h, gpre_ref[...]).astype(BF16)
    acc = None
    for c in range(D_FF // FF_CHUNK):
        sl = slice(c * FF_CHUNK, (c + 1) * FF_CHUNK)
        a = jnp.square(jnp.maximum(_dot(xn, w1_ref[:, sl].astype(BF16)), 0.0)).astype(BF16)
        part = _dot(a, w2_ref[sl, :].astype(BF16))
        acc = part if acc is None else acc + part
    out_ref[...] = h + _rms(acc, gpost_ref[...])


def _ffn(h, g_pre, w1, w2, g_post, layer):
    s = h.shape[0]
    tm = min(ROW_TILE, s)
    return pl.pallas_call(
        _ffn_body,
        grid=(s // tm,),
        in_specs=[_rows(tm, D_MODEL), _resident(g_pre.shape), _layer_resident(w1.shape, layer),
                  _layer_resident(w2.shape, layer), _resident(g_post.shape)],
        out_specs=_rows(tm, D_MODEL),
        out_shape=jax.ShapeDtypeStruct((s, D_MODEL), F32),
        compiler_params=_cparams(("parallel",)),
        name="sq_relu_mlp",
    )(h, g_pre, w1, w2, g_post)


def _row(g):
    return g.reshape(1, -1)


def kernel(x, mem, positions, g_mix_pre, g_mix_post, g_xattn_pre, g_xattn_post, g_mem, g_ffn_pre, g_ffn_post, ev_w_in, ev_g_q, ev_w_uq, ev_g_kv, ev_w_ukv, ev_w_gate, ev_b_gate, ev_g_gla, ev_w_out, od_w_in, od_conv_w, od_conv_b, od_w_q, od_w_k, od_w_v, od_w_gates, od_b_gates, od_g_hnorm, od_skip, od_w_out, xa_w_q, xa_w_k, xa_w_v, xa_w_o, ffn_w1, ffn_w2):
    batch, seq, d = x.shape
    assert batch == 1 and d == D_MODEL and seq % ROW_TILE == 0 and seq % (2 * ATTN_TK) == 0
    assert seq % (SCAN_CHUNK * GLA_CHUNKS_PER_STEP) == 0 and ROW_TILE == ATTN_TK
    h = x.reshape(seq, d)
    mem2 = mem.reshape(mem.shape[1], d)
    cos_t, sin_t = _rope_tables(positions)
    n_a = MLA_HEADS * MLA_V

    for layer in range(DEPTH):
        j = layer // 2
        if layer % 2 == 0:
            w_in_p, w_uq_p, w_uk_p, w_uvt_p, w_gate_p, b_gate = _prep_even_weights(
                ev_w_in[j], ev_w_uq[j], ev_w_ukv[j], ev_w_gate[j], ev_b_gate[j])
            q, k, vt, gq, gk, gv, log_a, r = _even_proj(
                h, _row(g_mix_pre[layer]), w_in_p, _row(ev_g_q[j]), w_uq_p, _row(ev_g_kv[j]), w_uk_p, w_uvt_p,
                w_gate_p, b_gate, cos_t, sin_t)
            a_t = _mla_attn(q, k, vt)
            o_gla = _gla(gq, gk, gv, log_a)
            w_out = ev_w_out[j].astype(BF16)
            h = _even_out(h, a_t, o_gla, r, _row(ev_g_gla[j]), w_out[:n_a], w_out[n_a:], _row(g_mix_post[layer]))
        else:
            wg_q, wg_k, wg_v, b_g = _prep_odd_gate_weights(od_w_gates[j], od_b_gates[j])
            w_k = od_w_k[j].astype(BF16)
            q, k, kt, v, gates, x_c, z = _odd_proj(
                h, _row(g_mix_pre[layer]), od_w_in[j].astype(BF16), od_conv_w[j], _row(od_conv_b[j]),
                od_w_q[j].astype(BF16), w_k, w_k.transpose(0, 2, 1), od_w_v[j].astype(BF16), wg_q, wg_k, wg_v, b_g)
            hcell = _mlstm(q, k, kt, v, gates)
            h = _odd_out(h, hcell, x_c, z, _row(od_g_hnorm[j]), _row(od_skip[j]), od_w_out[j].astype(BF16),
                         _row(g_mix_post[layer]))
        h = _xattn(h, mem2, _row(g_mem[layer]), _row(g_xattn_pre[layer]), xa_w_q, xa_w_k, xa_w_v, xa_w_o,
                   _row(g_xattn_post[layer]), layer)
        h = _ffn(h, _row(g_ffn_pre[layer]), ffn_w1, ffn_w2, _row(g_ffn_post[layer]), layer)
    return h.reshape(batch, seq, d)
```

```python
import functools

import numpy as np
import jax
import jax.numpy as jnp
from jax import lax
from jax.experimental import pallas as pl
from jax.experimental.pallas import tpu as pltpu

F32 = jnp.float32
BF16 = jnp.bfloat16

D_MODEL = 1024
DEPTH = 4
EPS = 1e-6
MLA_HEADS = 8
MLA_NOPE = 64
MLA_ROPE = 32
MLA_V = 64
MLA_Q_RANK = 256
MLA_KV_RANK = 128
ROPE_THETA = 10000.0
GLA_HEADS = 4
GLA_DK = 64
GLA_DV = 128
GLA_GATE_RANK = 16
GLA_TAU = 16.0
MLSTM_HEADS = 4
MLSTM_DH = 256
MLSTM_CONV = 4
XATTN_HEADS = 4
XATTN_DH = 256
D_FF = 4096
EVEN_SPLITS = (256, 128, 32, 256, 256, 512, 16, 512)

V7X_LANES = 128
V7X_VMEM_BYTES = 64 * 1024 * 1024
VMEM_LIMIT = (V7X_VMEM_BYTES * 7) // 8

ROW_TILE = 512
FF_CHUNK = 1024
ATTN_TK = 512
ATTN_HEADS_PER_STEP = 2
SCAN_CHUNK = 128
GLA_CHUNKS_PER_STEP = 4
HEAD_PAD = V7X_LANES
ATTN_VROWS = MLA_V + 16

LOG2E = 1.4426950408889634
NEG_BIG = -1e30


def _cparams(sem):
    return pltpu.CompilerParams(dimension_semantics=sem, vmem_limit_bytes=VMEM_LIMIT)


def _resident(shape):
    nd = len(shape)
    return pl.BlockSpec(shape, lambda *_: (0,) * nd, pipeline_mode=pl.Buffered(1))


def _rows(tile, width):
    return pl.BlockSpec((tile, width), lambda i: (i, 0))


def _rms(x, g):
    return x * lax.rsqrt(jnp.mean(x * x, axis=-1, keepdims=True) + EPS) * g


def _silu(x):
    return x * (1.0 / (1.0 + jnp.exp(-x)))


def _log_sigmoid(x):
    return jnp.minimum(x, 0.0) - jnp.log(1.0 + jnp.exp(-jnp.abs(x)))


def _dot(a, b):
    return jnp.dot(a, b, preferred_element_type=F32)


def _dot_nt(a, b):
    return lax.dot_general(a, b, (((1,), (1,)), ((), ())), preferred_element_type=F32)


def _dot_tn(a, b):
    return lax.dot_general(a, b, (((0,), (0,)), ((), ())), preferred_element_type=F32)


def _split2(x):
    x1 = x.astype(BF16)
    return x1, (x - x1.astype(F32)).astype(BF16)


def _split3(x):
    x1 = x.astype(BF16)
    r = x - x1.astype(F32)
    x2 = r.astype(BF16)
    x3 = (r - x2.astype(F32)).astype(BF16)
    return x1, x2, x3


def _rope_body(pos_ref, freq_ref, cos_ref, sin_ref):
    ang = pos_ref[...] * freq_ref[...]
    lane = lax.broadcasted_iota(jnp.int32, ang.shape, 1)
    on = jnp.logical_and(lane >= MLA_NOPE, lane < MLA_NOPE + MLA_ROPE)
    cos_ref[...] = jnp.where(on, jnp.cos(ang), 0.0)
    sin_ref[...] = jnp.where(on, jnp.sin(ang), 0.0)


def _rope_tables(positions):
    s = positions.shape[-1]
    pos = positions.astype(F32).reshape(s, 1)
    inv_freq = ROPE_THETA ** (-jnp.arange(0, MLA_ROPE, 2, dtype=F32) / MLA_ROPE)
    freq = jnp.concatenate([jnp.zeros((MLA_NOPE,), F32), inv_freq, inv_freq,
                            jnp.zeros((HEAD_PAD - MLA_NOPE - MLA_ROPE,), F32)]).reshape(1, HEAD_PAD)
    tile = min(2048, s)
    return pl.pallas_call(
        _rope_body,
        grid=(s // tile,),
        in_specs=[_rows(tile, 1), _resident((1, HEAD_PAD))],
        out_specs=[_rows(tile, HEAD_PAD), _rows(tile, HEAD_PAD)],
        out_shape=[jax.ShapeDtypeStruct((s, HEAD_PAD), F32)] * 2,
        compiler_params=_cparams(("parallel",)),
        name="rope_tables",
    )(pos, freq)


_EV_OFF = np.cumsum((0, 256, 128, 128, 128, 256, 256, 512, 128, 512))


def _even_proj_body(h_ref, gpre_ref, win_ref, gq_ref, wuq_ref, gkv_ref, wuk_ref, wuvt_ref, wgate_ref, bgate_ref,
                    cos_ref, sin_ref,
                    q_ref, k_ref, vt_ref, gq_out, gk_out, gv_out, la_out, r_out):
    o = _EV_OFF
    hn = _rms(h_ref[...], gpre_ref[...]).astype(BF16)
    proj = _dot(hn, win_ref[...])
    c_q = proj[:, o[0]:o[1]]
    c_kv = proj[:, o[1]:o[2]]
    kpe_a = proj[:, o[2]:o[3]]
    kpe_b = proj[:, o[3]:o[4]]
    cosk = cos_ref[...]
    sink = sin_ref[...]
    nh = MLA_HEADS
    width = nh * HEAD_PAD

    qs = (MLA_NOPE + MLA_ROPE) ** -0.5 * LOG2E
    lane = lax.broadcasted_iota(jnp.int32, cosk.shape, 1)
    cosq = qs * jnp.where(lane < MLA_NOPE, 1.0, cosk)
    sinq = qs * sink
    cqn = _rms(c_q, gq_ref[...]).astype(BF16)
    qab = _dot(cqn, wuq_ref[...])
    q = qab[:, :width] * jnp.tile(cosq, (1, nh)) + qab[:, width:] * jnp.tile(sinq, (1, nh))
    q_ref[...] = q.astype(BF16)

    ckvn = _rms(c_kv, gkv_ref[...]).astype(BF16)
    kpe = kpe_a * cosk + kpe_b * sink
    k_ref[...] = (_dot(ckvn, wuk_ref[...]) + jnp.tile(kpe, (1, nh))).astype(BF16)
    vt = _dot_nt(wuvt_ref[...], ckvn).astype(BF16)
    row = lax.broadcasted_iota(jnp.int32, (ATTN_VROWS - MLA_V, vt.shape[1]), 0)
    extra = jnp.where(row == 0, 1.0, 0.0).astype(BF16)
    for hd in range(nh):
        vt_ref[hd, 0:MLA_V, :] = vt[hd * MLA_V:(hd + 1) * MLA_V]
        vt_ref[hd, MLA_V:ATTN_VROWS, :] = extra

    gq_out[...] = proj[:, o[4]:o[5]]
    gk_out[...] = proj[:, o[5]:o[6]]
    gv_out[...] = proj[:, o[6]:o[7]]
    glr = proj[:, o[7]:o[8]].astype(BF16)
    x = _dot(glr, wgate_ref[...]) + bgate_ref[...]
    la_out[...] = _log_sigmoid(x) * (1.0 / GLA_TAU)
    r_out[...] = proj[:, o[8]:o[9]]


def _even_proj(h, g_pre, w_in, g_q, w_uq, g_kv, w_uk, w_uvt, w_gate, b_gate, cos_t, sin_t):
    s = h.shape[0]
    tm = min(ROW_TILE, s)
    width = MLA_HEADS * HEAD_PAD
    gdk = GLA_HEADS * GLA_DK
    gdv = GLA_HEADS * GLA_DV
    rows = [(width, BF16), (width, BF16), None, (gdk, F32), (gdk, F32), (gdv, F32), (gdk, F32), (gdv, F32)]
    vt_spec = pl.BlockSpec((MLA_HEADS, None, ATTN_VROWS, tm), lambda i: (0, i, 0, 0))
    vt_shape = jax.ShapeDtypeStruct((MLA_HEADS, s // tm, ATTN_VROWS, tm), BF16)
    return pl.pallas_call(
        _even_proj_body,
        grid=(s // tm,),
        in_specs=[_rows(tm, D_MODEL), _resident(g_pre.shape), _resident(w_in.shape), _resident(g_q.shape),
                  _resident(w_uq.shape), _resident(g_kv.shape), _resident(w_uk.shape), _resident(w_uvt.shape),
                  _resident(w_gate.shape), _resident(b_gate.shape), _rows(tm, HEAD_PAD), _rows(tm, HEAD_PAD)],
        out_specs=[vt_spec if r is None else _rows(tm, r[0]) for r in rows],
        out_shape=[vt_shape if r is None else jax.ShapeDtypeStruct((s, r[0]), r[1]) for r in rows],
        compiler_params=_cparams(("parallel",)),
        name="even_proj",
    )(h, g_pre, w_in, g_q, w_uq, g_kv, w_uk, w_uvt, w_gate, b_gate, cos_t, sin_t)


def _prep_even_weights(w_in, w_uq, w_ukv, w_gate, b_gate):
    d = w_in.shape[0]
    off = np.cumsum((0,) + EVEN_SPLITS)
    seg = [w_in[:, off[i]:off[i + 1]] for i in range(len(EVEN_SPLITS))]
    c_q, c_kv, k_pe, gq, gk, gv, glr, r = seg
    half = MLA_ROPE // 2
    z = lambda n: jnp.zeros((d, n), F32)
    pad = HEAD_PAD - MLA_NOPE - MLA_ROPE
    kpe_a = jnp.concatenate([z(MLA_NOPE), k_pe, z(pad)], axis=1)
    kpe_b = jnp.concatenate([z(MLA_NOPE), -k_pe[:, half:], k_pe[:, :half], z(pad)], axis=1)
    glr_p = jnp.concatenate([glr, z(HEAD_PAD - GLA_GATE_RANK)], axis=1)
    w_in_p = jnp.concatenate([c_q, c_kv, kpe_a, kpe_b, gq, gk, gv, glr_p, r], axis=1).astype(BF16)

    nh = MLA_HEADS
    wq3 = w_uq.reshape(MLA_Q_RANK, nh, MLA_NOPE + MLA_ROPE)
    nope, rope = wq3[..., :MLA_NOPE], wq3[..., MLA_NOPE:]
    zq = lambda n: jnp.zeros((MLA_Q_RANK, nh, n), F32)
    wa = jnp.concatenate([nope, rope, zq(pad)], axis=-1).reshape(MLA_Q_RANK, nh * HEAD_PAD)
    wb = jnp.concatenate([zq(MLA_NOPE), -rope[..., half:], rope[..., :half], zq(pad)],
                         axis=-1).reshape(MLA_Q_RANK, nh * HEAD_PAD)
    w_uq_p = jnp.concatenate([wa, wb], axis=1).astype(BF16)

    wkv3 = w_ukv.reshape(MLA_KV_RANK, nh, MLA_NOPE + MLA_V)
    zk = jnp.zeros((MLA_KV_RANK, nh, HEAD_PAD - MLA_NOPE), F32)
    w_uk_p = jnp.concatenate([wkv3[..., :MLA_NOPE], zk], axis=-1).reshape(MLA_KV_RANK, nh * HEAD_PAD).astype(BF16)
    w_uvt_p = wkv3[..., MLA_NOPE:].reshape(MLA_KV_RANK, nh * MLA_V).T.astype(BF16)

    w_gate_p = jnp.concatenate([w_gate, jnp.zeros((HEAD_PAD - GLA_GATE_RANK, w_gate.shape[1]), F32)],
                               axis=0).astype(BF16)
    return w_in_p, w_uq_p, w_uk_p, w_uvt_p, w_gate_p, b_gate.reshape(1, -1)


def _mla_attn_body(q_ref, k_ref, vt_ref, o_ref, acc_scr, st_scr, p_scr, m_scr, alpha_scr, *, tq, tk, hg):
    i = pl.program_id(1)
    heads = range(hg)

    def scores(t, slot):
        start = pl.multiple_of(t * tk, tk)
        for hd in heads:
            lanes = slice(hd * HEAD_PAD, (hd + 1) * HEAD_PAD)
            st_scr[slot, hd] = _dot_nt(k_ref[pl.ds(start, tk), lanes], q_ref[:, lanes])

    def accumulate(t, slot):
        for hd in heads:
            acc_scr[hd] = alpha_scr[slot, hd] * acc_scr[hd] + _dot(vt_ref[hd, t], p_scr[slot, hd])

    def softmax(slot, mask_shift):
        for hd in heads:
            st = st_scr[slot, hd]
            if mask_shift is not None:
                key = lax.broadcasted_iota(jnp.int32, st.shape, 0) + mask_shift
                qry = lax.broadcasted_iota(jnp.int32, st.shape, 1)
                st = jnp.where(key <= qry, st, NEG_BIG)
            m_prev = m_scr[hd]
            m_next = jnp.maximum(m_prev, jnp.max(st, axis=0, keepdims=True))
            p_scr[slot, hd] = jnp.exp2(st - m_next).astype(BF16)
            alpha_scr[slot, hd] = jnp.exp2(m_prev - m_next)
            m_scr[hd] = m_next

    def step(t, slot, mask_shift, more):
        softmax(slot, mask_shift)
        if more:
            scores(t + 1, 1 - slot)
        accumulate(jnp.maximum(t - 1, 0), 1 - slot)

    acc_scr[...] = jnp.zeros(acc_scr.shape, F32)
    p_scr[1] = jnp.zeros(p_scr.shape[1:], BF16)
    alpha_scr[1] = jnp.ones(alpha_scr.shape[1:], F32)
    m_scr[...] = jnp.full(m_scr.shape, NEG_BIG, F32)
    scores(0, 0)

    def pair(u, c):
        step(2 * u, 0, None, True)
        step(2 * u + 1, 1, None, True)
        return c

    lax.fori_loop(0, i, pair, 0)
    step(2 * i, 0, 0, True)
    step(2 * i + 1, 1, tk, False)
    accumulate(2 * i + 1, 1)
    for hd in heads:
        acc = acc_scr[hd]
        o_ref[hd] = (acc[:MLA_V] * (1.0 / acc[MLA_V:MLA_V + 1])).astype(o_ref.dtype)


def _mla_attn(q, k, vt):
    s = q.shape[0]
    nk, tk = vt.shape[1], vt.shape[3]
    tq = 2 * tk
    hg = ATTN_HEADS_PER_STEP
    out = pl.pallas_call(
        functools.partial(_mla_attn_body, tq=tq, tk=tk, hg=hg),
        grid=(MLA_HEADS // hg, s // tq),
        in_specs=[pl.BlockSpec((tq, hg * HEAD_PAD), lambda g, i: (i, g)),
                  pl.BlockSpec((s, hg * HEAD_PAD), lambda g, i: (0, g), pipeline_mode=pl.Buffered(1)),
                  pl.BlockSpec((hg, nk, ATTN_VROWS, tk), lambda g, i: (g, 0, 0, 0), pipeline_mode=pl.Buffered(1))],
        out_specs=pl.BlockSpec((hg, MLA_V, tq), lambda g, i: (g, 0, i)),
        out_shape=jax.ShapeDtypeStruct((MLA_HEADS, MLA_V, s), BF16),
        scratch_shapes=[pltpu.VMEM((hg, ATTN_VROWS, tq), F32),
                        pltpu.VMEM((2, hg, tk, tq), F32),
                        pltpu.VMEM((2, hg, tk, tq), BF16),
                        pltpu.VMEM((hg, 1, tq), F32),
                        pltpu.VMEM((2, hg, 1, tq), F32)],
        compiler_params=_cparams(("parallel", "arbitrary")),
        name="mla_attn",
    )(q, k, vt)
    return out.reshape(MLA_HEADS * MLA_V, s)


def _gla_levels(chunk):
    n = 0
    while (1 << n) < chunk:
        n += 1
    return n


def _gla_constants(chunk):
    t = np.arange(chunk)
    mats = [t[None, :] <= t[:, None]]
    masks = []
    b = 1
    while b < chunk:
        blk = t // (2 * b)
        upper = (t % (2 * b)) >= b
        e = blk * 2 * b + b - 1
        up_rows = upper[:, None] & (t[None, :] > e[:, None]) & (t[None, :] <= t[:, None])
        lo_rows = (~upper)[:, None] & (t[None, :] > t[:, None]) & (t[None, :] <= e[:, None])
        mats.append(up_rows | lo_rows)
        masks.append(upper[:, None] & (~upper)[None, :] & (blk[:, None] == blk[None, :]))
        b *= 2
    masks.append(t[:, None] == t[None, :])
    mats = np.concatenate(mats, axis=0).astype(np.float32)
    masks = np.stack(masks).astype(np.float32)
    masks = np.tile(masks, (1, GLA_HEADS, 1))
    return jnp.asarray(mats, BF16), jnp.asarray(masks, F32)


def _gla_body(q_ref, k_ref, v_ref, g_ref, mats_ref, masks_ref, o_ref, st_ref, *, chunk, n_sub):
    nlev = _gla_levels(chunk)
    nh, dk, dv = GLA_HEADS, GLA_DK, GLA_DV
    w = nh * dk

    @pl.when(pl.program_id(0) == 0)
    def _():
        st_ref[...] = jnp.zeros(st_ref.shape, F32)

    head_of_lane = lax.broadcasted_iota(jnp.int32, (chunk, w), 1) // dk

    def stack_heads(x):
        return jnp.concatenate([jnp.where(head_of_lane == hd, x, 0.0) for hd in range(nh)], axis=0).astype(BF16)

    pending = []
    for c in range(n_sub):
        rows = slice(c * chunk, (c + 1) * chunk)
        q = q_ref[rows, :] * (dk ** -0.5)
        k = k_ref[rows, :]
        g = g_ref[rows, :]
        e_all = _dot(mats_ref[...], jnp.concatenate(_split2(g), axis=1))
        e_all = e_all[:, :w] + e_all[:, w:]
        b = e_all[:chunk]
        b_end = b[chunk - 1:chunk]
        attn = _dot_nt(stack_heads(q), k.astype(BF16)) * masks_ref[nlev]
        for lv in range(nlev):
            dec = jnp.exp(e_all[(1 + lv) * chunk:(2 + lv) * chunk])
            attn = attn + _dot_nt(stack_heads(q * dec), (k * dec).astype(BF16)) * masks_ref[lv]
        attn = attn.astype(BF16)
        for hd in range(nh):
            vh = v_ref[rows, hd * dv:(hd + 1) * dv].astype(BF16)
            o_ref[rows, hd * dv:(hd + 1) * dv] = _dot(attn[hd * chunk:(hd + 1) * chunk], vh)
        pending.append((stack_heads(q * jnp.exp(b)), (k * jnp.exp(b_end - b)).astype(BF16), jnp.exp(b_end)))

    st = st_ref[...]
    head_of_state_lane = lax.broadcasted_iota(jnp.int32, st.shape, 1) // dk
    for c in range(n_sub):
        rows = slice(c * chunk, (c + 1) * chunk)
        qg, kg, dec_end = pending[c]
        inter = _dot_nt(qg, st.astype(BF16))
        new = st * dec_end
        for hd in range(nh):
            cols = slice(hd * dv, (hd + 1) * dv)
            o_ref[rows, cols] = o_ref[rows, cols] + inter[hd * chunk:(hd + 1) * chunk]
            upd = _dot_tn(v_ref[rows, cols].astype(BF16), kg)
            new = new + jnp.where(head_of_state_lane == hd, upd, 0.0)
        st = new
    st_ref[...] = st


def _gla(gq, gk, gv, log_a):
    s = gq.shape[0]
    chunk = min(SCAN_CHUNK, s)
    n_sub = min(GLA_CHUNKS_PER_STEP, s // chunk)
    rows = chunk * n_sub
    mats, masks = _gla_constants(chunk)
    gdk = GLA_HEADS * GLA_DK
    gdv = GLA_HEADS * GLA_DV
    return pl.pallas_call(
        functools.partial(_gla_body, chunk=chunk, n_sub=n_sub),
        grid=(s // rows,),
        in_specs=[_rows(rows, gdk), _rows(rows, gdk), _rows(rows, gdv), _rows(rows, gdk),
                  _resident(mats.shape), _resident(masks.shape)],
        out_specs=_rows(rows, gdv),
        out_shape=jax.ShapeDtypeStruct((s, gdv), F32),
        scratch_shapes=[pltpu.VMEM((GLA_DV, gdk), F32)],
        compiler_params=_cparams(("arbitrary",)),
        name="gla_scan",
    )(gq, gk, gv, log_a, mats, masks)


def _xattn_memory(mem_ref, gmem_ref, wk_ref, wv_ref, k_scr, v_scr):
    @pl.when(pl.program_id(0) == 0)
    def _():
        mem_n = _rms(mem_ref[...], gmem_ref[...]).astype(BF16)
        k_scr[...] = _dot(mem_n, wk_ref[...].astype(BF16)).astype(BF16)
        v_scr[...] = _dot(mem_n, wv_ref[...].astype(BF16)).astype(BF16)


def _xattn_rows(h, gpre_ref, wq_ref, wo_ref, gpost_ref, k_scr, v_scr):
    hn = _rms(h, gpre_ref[...]).astype(BF16)
    q = _dot(hn, wq_ref[...].astype(BF16)).astype(BF16)
    dh = XATTN_DH
    scale = dh ** -0.5
    outs = []
    for hd in range(XATTN_HEADS):
        sl = slice(hd * dh, (hd + 1) * dh)
        s = _dot_nt(q[:, sl], k_scr[:, sl]) * scale
        p = jnp.exp(s - jnp.max(s, axis=-1, keepdims=True))
        inv_l = 1.0 / jnp.sum(p, axis=-1, keepdims=True)
        outs.append((_dot(p.astype(BF16), v_scr[:, sl]) * inv_l).astype(BF16))
    o = jnp.concatenate(outs, axis=1)
    return h + _rms(_dot(o, wo_ref[...].astype(BF16)), gpost_ref[...])


def _layer_resident(shape, layer):
    return pl.BlockSpec((None,) + tuple(shape[1:]), lambda *_: (layer, 0, 0), pipeline_mode=pl.Buffered(1))


def _xattn_operands(xattn_args, layer):
    mem, g_mem, g_pre, w_q, w_k, w_v, w_o, g_post = xattn_args
    ins = (mem, g_mem, g_pre, w_q, w_k, w_v, w_o, g_post)
    specs = ([_resident(a.shape) for a in (mem, g_mem, g_pre)]
             + [_layer_resident(w.shape, layer) for w in (w_q, w_k, w_v, w_o)] + [_resident(g_post.shape)])
    scratch = [pltpu.VMEM((mem.shape[0], D_MODEL), BF16), pltpu.VMEM((mem.shape[0], D_MODEL), BF16)]
    return ins, specs, scratch


def _even_out_body(h_ref, at_ref, o_ref, r_ref, ggla_ref, wa_ref, wg_ref, gpost_ref,
                   mem_ref, gmem_ref, xpre_ref, wq_ref, wk_ref, wv_ref, wo_ref, xpost_ref, out_ref, k_scr, v_scr):
    _xattn_memory(mem_ref, gmem_ref, wk_ref, wv_ref, k_scr, v_scr)
    og = o_ref[...]
    gn = jnp.concatenate([_rms(og[:, hd * GLA_DV:(hd + 1) * GLA_DV], ggla_ref[...]) for hd in range(GLA_HEADS)],
                         axis=1)
    g = (gn * _silu(r_ref[...])).astype(BF16)
    mix = _dot_tn(at_ref[...], wa_ref[...]) + _dot(g, wg_ref[...])
    h = h_ref[...] + _rms(mix, gpost_ref[...])
    out_ref[...] = _xattn_rows(h, xpre_ref, wq_ref, wo_ref, xpost_ref, k_scr, v_scr)


def _even_out_xattn(h, a_t, o_gla, r, g_gla, w_a, w_g, g_post, xattn_args, layer):
    s = h.shape[0]
    tm = min(ROW_TILE, s)
    x_ins, x_specs, x_scratch = _xattn_operands(xattn_args, layer)
    return pl.pallas_call(
        _even_out_body,
        grid=(s // tm,),
        in_specs=[_rows(tm, D_MODEL), pl.BlockSpec((a_t.shape[0], tm), lambda i: (0, i)),
                  _rows(tm, o_gla.shape[1]), _rows(tm, r.shape[1]),
                  _resident(g_gla.shape), _resident(w_a.shape), _resident(w_g.shape), _resident(g_post.shape)] + x_specs,
        out_specs=_rows(tm, D_MODEL),
        out_shape=jax.ShapeDtypeStruct((s, D_MODEL), F32),
        scratch_shapes=x_scratch,
        compiler_params=_cparams(("arbitrary",)),
        name="even_out_xattn",
    )(h, a_t, o_gla, r, g_gla, w_a, w_g, g_post, *x_ins)


CONV_HALO = 8


def _odd_proj_body(h_ref, gpre_ref, win_ref, cw_ref, cb_ref, wq_ref, wk_ref, wkt_ref, wv_ref, wgq_ref, wgk_ref,
                   wgv_ref, bg_ref, q_out, k_out, kt_out, v_out, gates_out, xc_out, z_out, xbuf):
    tm = h_ref.shape[0]
    width = D_MODEL

    @pl.when(pl.program_id(0) == 0)
    def _():
        xbuf[0:CONV_HALO, :] = jnp.zeros((CONV_HALO, width), F32)

    hn = _rms(h_ref[...], gpre_ref[...]).astype(BF16)
    xz = _dot(hn, win_ref[...])
    x_m = xz[:, :width]
    z_out[...] = xz[:, width:]
    xbuf[CONV_HALO:CONV_HALO + tm, :] = x_m
    conv = cb_ref[...] + cw_ref[MLSTM_CONV - 1:MLSTM_CONV, :] * x_m
    for j in range(MLSTM_CONV - 1):
        back = MLSTM_CONV - 1 - j
        conv = conv + cw_ref[j:j + 1, :] * xbuf[CONV_HALO - back:CONV_HALO - back + tm, :]
    xbuf[0:CONV_HALO, :] = x_m[tm - CONV_HALO:, :]
    x_c = _silu(conv)
    xc_out[...] = x_c
    xcb = x_c.astype(BF16)

    dh = MLSTM_DH
    xmb = x_m.astype(BF16)
    qs, ks, vs = [], [], []
    for hd in range(MLSTM_HEADS):
        sl = slice(hd * dh, (hd + 1) * dh)
        qs.append(_dot(xcb[:, sl], wq_ref[hd]))
        ks.append(_dot(xcb[:, sl], wk_ref[hd]))
        vs.append(_dot(xmb[:, sl], wv_ref[hd]))
        kt_out[sl, :] = _dot_nt(wkt_ref[hd], xcb[:, sl]).astype(BF16)
    q = jnp.concatenate(qs, axis=1).astype(BF16)
    k = jnp.concatenate(ks, axis=1).astype(BF16)
    v = jnp.concatenate(vs, axis=1).astype(BF16)
    gates_out[...] = _dot(q, wgq_ref[...]) + _dot(k, wgk_ref[...]) + _dot(v, wgv_ref[...]) + bg_ref[...]
    q_out[...] = (q.astype(F32) * (dh ** -0.5)).astype(BF16)
    k_out[...] = k
    v_out[...] = v


def _odd_proj(h, g_pre, w_in, conv_w, conv_b, w_q, w_k, w_kt, w_v, wg_q, wg_k, wg_v, b_g):
    s = h.shape[0]
    tm = min(ROW_TILE, s)
    rows = [(D_MODEL, BF16), (D_MODEL, BF16), None, (D_MODEL, BF16), (V7X_LANES, F32), (D_MODEL, F32),
            (D_MODEL, F32)]
    ins = (g_pre, w_in, conv_w, conv_b, w_q, w_k, w_kt, w_v, wg_q, wg_k, wg_v, b_g)
    kt_spec = pl.BlockSpec((D_MODEL, tm), lambda i: (0, i))
    kt_shape = jax.ShapeDtypeStruct((D_MODEL, s), BF16)
    return pl.pallas_call(
        _odd_proj_body,
        grid=(s // tm,),
        in_specs=[_rows(tm, D_MODEL)] + [_resident(a.shape) for a in ins],
        out_specs=[kt_spec if r is None else _rows(tm, r[0]) for r in rows],
        out_shape=[kt_shape if r is None else jax.ShapeDtypeStruct((s, r[0]), r[1]) for r in rows],
        scratch_shapes=[pltpu.VMEM((CONV_HALO + tm, D_MODEL), F32)],
        compiler_params=_cparams(("arbitrary",)),
        name="odd_proj",
    )(h, *ins)


def _prep_odd_gate_weights(w_gates, b_gates):
    nh, dh = MLSTM_HEADS, MLSTM_DH
    w4 = w_gates.reshape(nh, 3, dh, 2 * nh)
    pad = jnp.zeros((nh * dh, V7X_LANES - 2 * nh), F32)
    parts = [jnp.concatenate([w4[:, j].reshape(nh * dh, 2 * nh), pad], axis=1).astype(BF16) for j in range(3)]
    b = jnp.concatenate([b_gates, jnp.zeros((V7X_LANES - 2 * nh,), F32)]).reshape(1, V7X_LANES)
    return parts[0], parts[1], parts[2], b


MLSTM_EXT = MLSTM_DH + V7X_LANES


def _mlstm_body(q_ref, k_ref, kt_ref, v_ref, gcol_ref, grow_ref, tri_ref, trit_ref, o_ref, c_scr, m_scr, *, chunk):
    nh, dh = MLSTM_HEADS, MLSTM_DH
    lanes = V7X_LANES

    @pl.when(pl.program_id(0) == 0)
    def _():
        c_scr[...] = jnp.zeros(c_scr.shape, F32)
        m_scr[...] = jnp.zeros(m_scr.shape, F32)

    gc = gcol_ref[...]
    b_col = _dot(tri_ref[...], jnp.concatenate(_split3(_log_sigmoid(gc)), axis=1))
    b_col = b_col[:, :lanes] + b_col[:, lanes:2 * lanes] + b_col[:, 2 * lanes:]
    gr = grow_ref[0]
    nr = gr.shape[0]
    b_row = _dot(jnp.concatenate(_split3(_log_sigmoid(gr)), axis=0), trit_ref[...])
    b_row = b_row[:nr] + b_row[nr:2 * nr] + b_row[2 * nr:]

    row = lax.broadcasted_iota(jnp.int32, (chunk, chunk), 0)
    col = lax.broadcasted_iota(jnp.int32, (chunk, chunk), 1)
    causal = col <= row
    ones_col = jnp.where(lax.broadcasted_iota(jnp.int32, (chunk, lanes), 1) == 0, 1.0, 0.0).astype(BF16)
    for hd in range(nh):
        sl = slice(hd * dh, (hd + 1) * dh)
        r_h = gr[hd:hd + 1, :] - b_row[nh + hd:nh + hd + 1, :]
        b_h = b_col[:, nh + hd:nh + hd + 1]
        m_old = m_scr[hd:hd + 1, 0:1]
        qh = q_ref[:, sl]
        vext = jnp.concatenate([v_ref[:, sl], ones_col], axis=1)
        cext = c_scr[hd]

        logw = jnp.where(causal, r_h, NEG_BIG)
        big_m = jnp.maximum(jnp.max(logw, axis=-1, keepdims=True), m_old)
        w_intra = jnp.exp(logw - big_m) * _dot_nt(qh, k_ref[:, sl])
        w_inter = jnp.exp(m_old - big_m)
        out = (_dot(w_intra.astype(BF16), vext)
               + _dot((qh.astype(F32) * w_inter).astype(BF16), cext.astype(BF16)))
        den = out[:, dh:dh + 1]
        o_ref[:, sl] = out[:, :dh] / jnp.maximum(jnp.abs(den), jnp.exp(-(b_h + big_m)))

        m_last = big_m[chunk - 1:chunk, :]
        w_s = jnp.exp(r_h - m_last)
        ktw = (kt_ref[sl, :].astype(F32) * w_s).astype(BF16)
        c_scr[hd] = jnp.exp(m_old - m_last) * cext + _dot(ktw, vext)
        m_scr[hd:hd + 1, :] = jnp.broadcast_to(b_h[chunk - 1:chunk, :] + m_last, (1, m_scr.shape[1]))


def _mlstm(q, k, kt, v, gates):
    s = q.shape[0]
    chunk = min(SCAN_CHUNK, s)
    nc = s // chunk
    lanes = V7X_LANES
    ng = 2 * MLSTM_HEADS
    grow = gates[:, :ng].reshape(nc, chunk, ng).transpose(0, 2, 1)
    t = np.arange(chunk)
    tri = (t[None, :] <= t[:, None]).astype(np.float32)
    return pl.pallas_call(
        functools.partial(_mlstm_body, chunk=chunk),
        grid=(nc,),
        in_specs=[_rows(chunk, D_MODEL), _rows(chunk, D_MODEL), pl.BlockSpec((D_MODEL, chunk), lambda c: (0, c)),
                  _rows(chunk, D_MODEL), _rows(chunk, lanes),
                  pl.BlockSpec((1, ng, chunk), lambda c: (c, 0, 0)),
                  _resident((chunk, chunk)), _resident((chunk, chunk))],
        out_specs=_rows(chunk, D_MODEL),
        out_shape=jax.ShapeDtypeStruct((s, D_MODEL), F32),
        scratch_shapes=[pltpu.VMEM((MLSTM_HEADS, MLSTM_DH, MLSTM_EXT), F32),
                        pltpu.VMEM((8, lanes), F32)],
        compiler_params=_cparams(("arbitrary",)),
        name="mlstm_scan",
    )(q, k, kt, v, gates, grow, jnp.asarray(tri, BF16), jnp.asarray(tri.T, BF16))


def _odd_out_body(h_ref, hc_ref, xc_ref, z_ref, gh_ref, skip_ref, wo_ref, gpost_ref,
                  mem_ref, gmem_ref, xpre_ref, wq_ref, wk_ref, wv_ref, xwo_ref, xpost_ref, out_ref, k_scr, v_scr):
    _xattn_memory(mem_ref, gmem_ref, wk_ref, wv_ref, k_scr, v_scr)
    hc = hc_ref[...]
    dh = MLSTM_DH
    parts = []
    for hd in range(MLSTM_HEADS):
        seg = hc[:, hd * dh:(hd + 1) * dh]
        cen = seg - jnp.mean(seg, axis=-1, keepdims=True)
        parts.append(cen * lax.rsqrt(jnp.mean(cen * cen, axis=-1, keepdims=True) + EPS))
    hn = jnp.concatenate(parts, axis=1) * gh_ref[...]
    out = ((hn + skip_ref[...] * xc_ref[...]) * _silu(z_ref[...])).astype(BF16)
    h = h_ref[...] + _rms(_dot(out, wo_ref[...]), gpost_ref[...])
    out_ref[...] = _xattn_rows(h, xpre_ref, wq_ref, xwo_ref, xpost_ref, k_scr, v_scr)


def _odd_out_xattn(h, hcell, x_c, z, g_hnorm, skip, w_out, g_post, xattn_args, layer):
    s = h.shape[0]
    tm = min(ROW_TILE, s)
    x_ins, x_specs, x_scratch = _xattn_operands(xattn_args, layer)
    return pl.pallas_call(
        _odd_out_body,
        grid=(s // tm,),
        in_specs=[_rows(tm, D_MODEL)] * 4 + [_resident(g_hnorm.shape), _resident(skip.shape),
                                              _resident(w_out.shape), _resident(g_post.shape)] + x_specs,
        out_specs=_rows(tm, D_MODEL),
        out_shape=jax.ShapeDtypeStruct((s, D_MODEL), F32),
        scratch_shapes=x_scratch,
        compiler_params=_cparams(("arbitrary",)),
        name="odd_out_xattn",
    )(h, hcell, x_c, z, g_hnorm, skip, w_out, g_post, *x_ins)


def _ffn_body(h_ref, gpre_ref, w1_ref, w2_ref, gpost_ref, out_ref):
    h = h_ref[...]
    xn = _rms(h, gpre_ref[...]).astype(BF16)
    acc = None
    for c in range(D_FF // FF_CHUNK):
        sl = slice(c * FF_CHUNK, (c + 1) * FF_CHUNK)
        a = jnp.square(jnp.maximum(_dot(xn, w1_ref[:, sl].astype(BF16)), 0.0)).astype(BF16)
        part = _dot(a, w2_ref[sl, :].astype(BF16))
        acc = part if acc is None else acc + part
    out_ref[...] = h + _rms(acc, gpost_ref[...])


def _ffn(h, g_pre, w1, w2, g_post, layer):
    s = h.shape[0]
    tm = min(ROW_TILE, s)
    return pl.pallas_call(
        _ffn_body,
        grid=(s // tm,),
        in_specs=[_rows(tm, D_MODEL), _resident(g_pre.shape), _layer_resident(w1.shape, layer),
                  _layer_resident(w2.shape, layer), _resident(g_post.shape)],
        out_specs=_rows(tm, D_MODEL),
        out_shape=jax.ShapeDtypeStruct((s, D_MODEL), F32),
        compiler_params=_cparams(("parallel",)),
        name="sq_relu_mlp",
    )(h, g_pre, w1, w2, g_post)


def _row(g):
    return g.reshape(1, -1)


def kernel(x, mem, positions, g_mix_pre, g_mix_post, g_xattn_pre, g_xattn_post, g_mem, g_ffn_pre, g_ffn_post, ev_w_in, ev_g_q, ev_w_uq, ev_g_kv, ev_w_ukv, ev_w_gate, ev_b_gate, ev_g_gla, ev_w_out, od_w_in, od_conv_w, od_conv_b, od_w_q, od_w_k, od_w_v, od_w_gates, od_b_gates, od_g_hnorm, od_skip, od_w_out, xa_w_q, xa_w_k, xa_w_v, xa_w_o, ffn_w1, ffn_w2):
    batch, seq, d = x.shape
    assert batch == 1 and d == D_MODEL and seq % ROW_TILE == 0 and seq % (2 * ATTN_TK) == 0
    assert seq % (SCAN_CHUNK * GLA_CHUNKS_PER_STEP) == 0 and ROW_TILE == ATTN_TK
    h = x.reshape(seq, d)
    mem2 = mem.reshape(mem.shape[1], d)
    cos_t, sin_t = _rope_tables(positions)
    n_a = MLA_HEADS * MLA_V

    for layer in range(DEPTH):
        j = layer // 2
        xattn_args = (mem2, _row(g_mem[layer]), _row(g_xattn_pre[layer]), xa_w_q, xa_w_k, xa_w_v, xa_w_o,
                      _row(g_xattn_post[layer]))
        if layer % 2 == 0:
            w_in_p, w_uq_p, w_uk_p, w_uvt_p, w_gate_p, b_gate = _prep_even_weights(
                ev_w_in[j], ev_w_uq[j], ev_w_ukv[j], ev_w_gate[j], ev_b_gate[j])
            q, k, vt, gq, gk, gv, log_a, r = _even_proj(
                h, _row(g_mix_pre[layer]), w_in_p, _row(ev_g_q[j]), w_uq_p, _row(ev_g_kv[j]), w_uk_p, w_uvt_p,
                w_gate_p, b_gate, cos_t, sin_t)
            a_t = _mla_attn(q, k, vt)
            o_gla = _gla(gq, gk, gv, log_a)
            w_out = ev_w_out[j].astype(BF16)
            h = _even_out_xattn(h, a_t, o_gla, r, _row(ev_g_gla[j]), w_out[:n_a], w_out[n_a:],
                                _row(g_mix_post[layer]), xattn_args, layer)
        else:
            wg_q, wg_k, wg_v, b_g = _prep_odd_gate_weights(od_w_gates[j], od_b_gates[j])
            w_k = od_w_k[j].astype(BF16)
            q, k, kt, v, gates, x_c, z = _odd_proj(
                h, _row(g_mix_pre[layer]), od_w_in[j].astype(BF16), od_conv_w[j], _row(od_conv_b[j]),
                od_w_q[j].astype(BF16), w_k, w_k.transpose(0, 2, 1), od_w_v[j].astype(BF16), wg_q, wg_k, wg_v, b_g)
            hcell = _mlstm(q, k, kt, v, gates)
            h = _odd_out_xattn(h, hcell, x_c, z, _row(od_g_hnorm[j]), _row(od_skip[j]), od_w_out[j].astype(BF16),
                               _row(g_mix_post[layer]), xattn_args, layer)
        h = _ffn(h, _row(g_ffn_pre[layer]), ffn_w1, ffn_w2, _row(g_ffn_post[layer]), layer)
    return h.reshape(batch, seq, d)
```

```python
import functools

import numpy as np
import jax
import jax.numpy as jnp
from jax import lax
from jax.experimental import pallas as pl
from jax.experimental.pallas import tpu as pltpu

F32 = jnp.float32
BF16 = jnp.bfloat16

D_MODEL = 1024
DEPTH = 4
EPS = 1e-6
MLA_HEADS = 8
MLA_NOPE = 64
MLA_ROPE = 32
MLA_V = 64
MLA_Q_RANK = 256
MLA_KV_RANK = 128
ROPE_THETA = 10000.0
GLA_HEADS = 4
GLA_DK = 64
GLA_DV = 128
GLA_GATE_RANK = 16
GLA_TAU = 16.0
MLSTM_HEADS = 4
MLSTM_DH = 256
MLSTM_CONV = 4
XATTN_HEADS = 4
XATTN_DH = 256
D_FF = 4096
EVEN_SPLITS = (256, 128, 32, 256, 256, 512, 16, 512)

V7X_LANES = 128
V7X_VMEM_BYTES = 64 * 1024 * 1024
VMEM_LIMIT = (V7X_VMEM_BYTES * 7) // 8

ROW_TILE = 512
FF_CHUNK = 1024
ATTN_TK = 512
ATTN_HEADS_PER_STEP = 2
SCAN_CHUNK = 128
GLA_CHUNKS_PER_STEP = 4
MLSTM_CHUNKS_PER_STEP = 4
HEAD_PAD = V7X_LANES
ATTN_VROWS = MLA_V + 16

LOG2E = 1.4426950408889634
NEG_BIG = -1e30


def _cparams(sem):
    return pltpu.CompilerParams(dimension_semantics=sem, vmem_limit_bytes=VMEM_LIMIT)


def _resident(shape):
    nd = len(shape)
    return pl.BlockSpec(shape, lambda *_: (0,) * nd, pipeline_mode=pl.Buffered(1))


def _rows(tile, width):
    return pl.BlockSpec((tile, width), lambda i: (i, 0))


def _rms(x, g):
    return x * lax.rsqrt(jnp.mean(x * x, axis=-1, keepdims=True) + EPS) * g


def _silu(x):
    return x * (1.0 / (1.0 + jnp.exp(-x)))


def _log_sigmoid(x):
    return jnp.minimum(x, 0.0) - jnp.log(1.0 + jnp.exp(-jnp.abs(x)))


def _dot(a, b):
    return jnp.dot(a, b, preferred_element_type=F32)


def _dot_nt(a, b):
    return lax.dot_general(a, b, (((1,), (1,)), ((), ())), preferred_element_type=F32)


def _dot_tn(a, b):
    return lax.dot_general(a, b, (((0,), (0,)), ((), ())), preferred_element_type=F32)


def _split2(x):
    x1 = x.astype(BF16)
    return x1, (x - x1.astype(F32)).astype(BF16)


def _split3(x):
    x1 = x.astype(BF16)
    r = x - x1.astype(F32)
    x2 = r.astype(BF16)
    x3 = (r - x2.astype(F32)).astype(BF16)
    return x1, x2, x3


def _rope_body(pos_ref, freq_ref, cos_ref, sin_ref):
    ang = pos_ref[...] * freq_ref[...]
    lane = lax.broadcasted_iota(jnp.int32, ang.shape, 1)
    on = jnp.logical_and(lane >= MLA_NOPE, lane < MLA_NOPE + MLA_ROPE)
    cos_ref[...] = jnp.where(on, jnp.cos(ang), 0.0)
    sin_ref[...] = jnp.where(on, jnp.sin(ang), 0.0)


def _rope_tables(positions):
    s = positions.shape[-1]
    pos = positions.astype(F32).reshape(s, 1)
    inv_freq = ROPE_THETA ** (-jnp.arange(0, MLA_ROPE, 2, dtype=F32) / MLA_ROPE)
    freq = jnp.concatenate([jnp.zeros((MLA_NOPE,), F32), inv_freq, inv_freq,
                            jnp.zeros((HEAD_PAD - MLA_NOPE - MLA_ROPE,), F32)]).reshape(1, HEAD_PAD)
    tile = min(2048, s)
    return pl.pallas_call(
        _rope_body,
        grid=(s // tile,),
        in_specs=[_rows(tile, 1), _resident((1, HEAD_PAD))],
        out_specs=[_rows(tile, HEAD_PAD), _rows(tile, HEAD_PAD)],
        out_shape=[jax.ShapeDtypeStruct((s, HEAD_PAD), F32)] * 2,
        compiler_params=_cparams(("parallel",)),
        name="rope_tables",
    )(pos, freq)


_EV_OFF = np.cumsum((0, 256, 128, 128, 128, 256, 256, 512, 128, 512))


def _even_proj_body(h_ref, gpre_ref, win_ref, gq_ref, wuq_ref, gkv_ref, wuk_ref, wuvt_ref, wgate_ref, bgate_ref,
                    cos_ref, sin_ref,
                    q_ref, k_ref, vt_ref, gq_out, gk_out, gv_out, la_out, r_out):
    o = _EV_OFF
    hn = _rms(h_ref[...], gpre_ref[...]).astype(BF16)
    proj = _dot(hn, win_ref[...])
    c_q = proj[:, o[0]:o[1]]
    c_kv = proj[:, o[1]:o[2]]
    kpe_a = proj[:, o[2]:o[3]]
    kpe_b = proj[:, o[3]:o[4]]
    cosk = cos_ref[...]
    sink = sin_ref[...]
    nh = MLA_HEADS
    width = nh * HEAD_PAD

    qs = (MLA_NOPE + MLA_ROPE) ** -0.5 * LOG2E
    lane = lax.broadcasted_iota(jnp.int32, cosk.shape, 1)
    cosq = qs * jnp.where(lane < MLA_NOPE, 1.0, cosk)
    sinq = qs * sink
    cqn = _rms(c_q, gq_ref[...]).astype(BF16)
    qab = _dot(cqn, wuq_ref[...])
    q = qab[:, :width] * jnp.tile(cosq, (1, nh)) + qab[:, width:] * jnp.tile(sinq, (1, nh))
    q_ref[...] = q.astype(BF16)

    ckvn = _rms(c_kv, gkv_ref[...]).astype(BF16)
    kpe = kpe_a * cosk + kpe_b * sink
    k_ref[...] = (_dot(ckvn, wuk_ref[...]) + jnp.tile(kpe, (1, nh))).astype(BF16)
    vt = _dot_nt(wuvt_ref[...], ckvn).astype(BF16)
    row = lax.broadcasted_iota(jnp.int32, (ATTN_VROWS - MLA_V, vt.shape[1]), 0)
    extra = jnp.where(row == 0, 1.0, 0.0).astype(BF16)
    for hd in range(nh):
        vt_ref[hd, 0:MLA_V, :] = vt[hd * MLA_V:(hd + 1) * MLA_V]
        vt_ref[hd, MLA_V:ATTN_VROWS, :] = extra

    gq_out[...] = proj[:, o[4]:o[5]]
    gk_out[...] = proj[:, o[5]:o[6]]
    gv_out[...] = proj[:, o[6]:o[7]]
    glr = proj[:, o[7]:o[8]].astype(BF16)
    x = _dot(glr, wgate_ref[...]) + bgate_ref[...]
    la_out[...] = _log_sigmoid(x) * (1.0 / GLA_TAU)
    r_out[...] = proj[:, o[8]:o[9]]


def _even_proj(h, g_pre, w_in, g_q, w_uq, g_kv, w_uk, w_uvt, w_gate, b_gate, cos_t, sin_t):
    s = h.shape[0]
    tm = min(ROW_TILE, s)
    width = MLA_HEADS * HEAD_PAD
    gdk = GLA_HEADS * GLA_DK
    gdv = GLA_HEADS * GLA_DV
    rows = [(width, BF16), (width, BF16), None, (gdk, F32), (gdk, F32), (gdv, F32), (gdk, F32), (gdv, F32)]
    vt_spec = pl.BlockSpec((MLA_HEADS, None, ATTN_VROWS, tm), lambda i: (0, i, 0, 0))
    vt_shape = jax.ShapeDtypeStruct((MLA_HEADS, s // tm, ATTN_VROWS, tm), BF16)
    return pl.pallas_call(
        _even_proj_body,
        grid=(s // tm,),
        in_specs=[_rows(tm, D_MODEL), _resident(g_pre.shape), _resident(w_in.shape), _resident(g_q.shape),
                  _resident(w_uq.shape), _resident(g_kv.shape), _resident(w_uk.shape), _resident(w_uvt.shape),
                  _resident(w_gate.shape), _resident(b_gate.shape), _rows(tm, HEAD_PAD), _rows(tm, HEAD_PAD)],
        out_specs=[vt_spec if r is None else _rows(tm, r[0]) for r in rows],
        out_shape=[vt_shape if r is None else jax.ShapeDtypeStruct((s, r[0]), r[1]) for r in rows],
        compiler_params=_cparams(("parallel",)),
        name="even_proj",
    )(h, g_pre, w_in, g_q, w_uq, g_kv, w_uk, w_uvt, w_gate, b_gate, cos_t, sin_t)


def _prep_even_weights(w_in, w_uq, w_ukv, w_gate, b_gate):
    d = w_in.shape[0]
    off = np.cumsum((0,) + EVEN_SPLITS)
    seg = [w_in[:, off[i]:off[i + 1]] for i in range(len(EVEN_SPLITS))]
    c_q, c_kv, k_pe, gq, gk, gv, glr, r = seg
    half = MLA_ROPE // 2
    z = lambda n: jnp.zeros((d, n), F32)
    pad = HEAD_PAD - MLA_NOPE - MLA_ROPE
    kpe_a = jnp.concatenate([z(MLA_NOPE), k_pe, z(pad)], axis=1)
    kpe_b = jnp.concatenate([z(MLA_NOPE), -k_pe[:, half:], k_pe[:, :half], z(pad)], axis=1)
    glr_p = jnp.concatenate([glr, z(HEAD_PAD - GLA_GATE_RANK)], axis=1)
    w_in_p = jnp.concatenate([c_q, c_kv, kpe_a, kpe_b, gq, gk, gv, glr_p, r], axis=1).astype(BF16)

    nh = MLA_HEADS
    wq3 = w_uq.reshape(MLA_Q_RANK, nh, MLA_NOPE + MLA_ROPE)
    nope, rope = wq3[..., :MLA_NOPE], wq3[..., MLA_NOPE:]
    zq = lambda n: jnp.zeros((MLA_Q_RANK, nh, n), F32)
    wa = jnp.concatenate([nope, rope, zq(pad)], axis=-1).reshape(MLA_Q_RANK, nh * HEAD_PAD)
    wb = jnp.concatenate([zq(MLA_NOPE), -rope[..., half:], rope[..., :half], zq(pad)],
                         axis=-1).reshape(MLA_Q_RANK, nh * HEAD_PAD)
    w_uq_p = jnp.concatenate([wa, wb], axis=1).astype(BF16)

    wkv3 = w_ukv.reshape(MLA_KV_RANK, nh, MLA_NOPE + MLA_V)
    zk = jnp.zeros((MLA_KV_RANK, nh, HEAD_PAD - MLA_NOPE), F32)
    w_uk_p = jnp.concatenate([wkv3[..., :MLA_NOPE], zk], axis=-1).reshape(MLA_KV_RANK, nh * HEAD_PAD).astype(BF16)
    w_uvt_p = wkv3[..., MLA_NOPE:].reshape(MLA_KV_RANK, nh * MLA_V).T.astype(BF16)

    w_gate_p = jnp.concatenate([w_gate, jnp.zeros((HEAD_PAD - GLA_GATE_RANK, w_gate.shape[1]), F32)],
                               axis=0).astype(BF16)
    return w_in_p, w_uq_p, w_uk_p, w_uvt_p, w_gate_p, b_gate.reshape(1, -1)


def _mla_attn_body(q_ref, k_ref, vt_ref, o_ref, acc_scr, st_scr, p_scr, m_scr, alpha_scr, *, tq, tk, hg):
    i = pl.program_id(1)
    heads = range(hg)

    def scores(t, slot):
        start = pl.multiple_of(t * tk, tk)
        for hd in heads:
            lanes = slice(hd * HEAD_PAD, (hd + 1) * HEAD_PAD)
            st_scr[slot, hd] = _dot_nt(k_ref[pl.ds(start, tk), lanes], q_ref[:, lanes])

    def accumulate(t, slot):
        for hd in heads:
            acc_scr[hd] = alpha_scr[slot, hd] * acc_scr[hd] + _dot(vt_ref[hd, t], p_scr[slot, hd])

    def softmax(slot, mask_shift):
        for hd in heads:
            st = st_scr[slot, hd]
            if mask_shift is not None:
                key = lax.broadcasted_iota(jnp.int32, st.shape, 0) + mask_shift
                qry = lax.broadcasted_iota(jnp.int32, st.shape, 1)
                st = jnp.where(key <= qry, st, NEG_BIG)
            m_prev = m_scr[hd]
            m_next = jnp.maximum(m_prev, jnp.max(st, axis=0, keepdims=True))
            p_scr[slot, hd] = jnp.exp2(st - m_next).astype(BF16)
            alpha_scr[slot, hd] = jnp.exp2(m_prev - m_next)
            m_scr[hd] = m_next

    def step(t, slot, mask_shift, more):
        softmax(slot, mask_shift)
        if more:
            scores(t + 1, 1 - slot)
        accumulate(jnp.maximum(t - 1, 0), 1 - slot)

    acc_scr[...] = jnp.zeros(acc_scr.shape, F32)
    p_scr[1] = jnp.zeros(p_scr.shape[1:], BF16)
    alpha_scr[1] = jnp.ones(alpha_scr.shape[1:], F32)
    m_scr[...] = jnp.full(m_scr.shape, NEG_BIG, F32)
    scores(0, 0)

    def pair(u, c):
        step(2 * u, 0, None, True)
        step(2 * u + 1, 1, None, True)
        return c

    lax.fori_loop(0, i, pair, 0)
    step(2 * i, 0, 0, True)
    step(2 * i + 1, 1, tk, False)
    accumulate(2 * i + 1, 1)
    for hd in heads:
        acc = acc_scr[hd]
        o_ref[hd] = (acc[:MLA_V] * (1.0 / acc[MLA_V:MLA_V + 1])).astype(o_ref.dtype)


def _mla_attn(q, k, vt):
    s = q.shape[0]
    nk, tk = vt.shape[1], vt.shape[3]
    tq = 2 * tk
    hg = ATTN_HEADS_PER_STEP
    out = pl.pallas_call(
        functools.partial(_mla_attn_body, tq=tq, tk=tk, hg=hg),
        grid=(MLA_HEADS // hg, s // tq),
        in_specs=[pl.BlockSpec((tq, hg * HEAD_PAD), lambda g, i: (i, g)),
                  pl.BlockSpec((s, hg * HEAD_PAD), lambda g, i: (0, g), pipeline_mode=pl.Buffered(1)),
                  pl.BlockSpec((hg, nk, ATTN_VROWS, tk), lambda g, i: (g, 0, 0, 0), pipeline_mode=pl.Buffered(1))],
        out_specs=pl.BlockSpec((hg, MLA_V, tq), lambda g, i: (g, 0, i)),
        out_shape=jax.ShapeDtypeStruct((MLA_HEADS, MLA_V, s), BF16),
        scratch_shapes=[pltpu.VMEM((hg, ATTN_VROWS, tq), F32),
                        pltpu.VMEM((2, hg, tk, tq), F32),
                        pltpu.VMEM((2, hg, tk, tq), BF16),
                        pltpu.VMEM((hg, 1, tq), F32),
                        pltpu.VMEM((2, hg, 1, tq), F32)],
        compiler_params=_cparams(("parallel", "arbitrary")),
        name="mla_attn",
    )(q, k, vt)
    return out.reshape(MLA_HEADS * MLA_V, s)


def _gla_levels(chunk):
    n = 0
    while (1 << n) < chunk:
        n += 1
    return n


def _gla_constants(chunk):
    t = np.arange(chunk)
    mats = [t[None, :] <= t[:, None]]
    masks = []
    b = 1
    while b < chunk:
        blk = t // (2 * b)
        upper = (t % (2 * b)) >= b
        e = blk * 2 * b + b - 1
        up_rows = upper[:, None] & (t[None, :] > e[:, None]) & (t[None, :] <= t[:, None])
        lo_rows = (~upper)[:, None] & (t[None, :] > t[:, None]) & (t[None, :] <= e[:, None])
        mats.append(up_rows | lo_rows)
        masks.append(upper[:, None] & (~upper)[None, :] & (blk[:, None] == blk[None, :]))
        b *= 2
    masks.append(t[:, None] == t[None, :])
    mats = np.concatenate(mats, axis=0).astype(np.float32)
    masks = np.stack(masks).astype(np.float32)
    masks = np.tile(masks, (1, GLA_HEADS, 1))
    return jnp.asarray(mats, BF16), jnp.asarray(masks, F32)


def _gla_body(q_ref, k_ref, v_ref, g_ref, mats_ref, masks_ref, o_ref, st_ref, *, chunk, n_sub):
    nlev = _gla_levels(chunk)
    nh, dk, dv = GLA_HEADS, GLA_DK, GLA_DV
    w = nh * dk

    @pl.when(pl.program_id(0) == 0)
    def _():
        st_ref[...] = jnp.zeros(st_ref.shape, F32)

    head_of_lane = lax.broadcasted_iota(jnp.int32, (chunk, w), 1) // dk

    def stack_heads(x):
        return jnp.concatenate([jnp.where(head_of_lane == hd, x, 0.0) for hd in range(nh)], axis=0).astype(BF16)

    chunks = [slice(c * chunk, (c + 1) * chunk) for c in range(n_sub)]
    qs = [q_ref[rows, :] * (dk ** -0.5) for rows in chunks]
    ks = [k_ref[rows, :] for rows in chunks]
    e_alls = [_dot(mats_ref[...], jnp.concatenate(_split2(g_ref[rows, :]), axis=1)) for rows in chunks]
    e_alls = [x[:, :w] + x[:, w:] for x in e_alls]
    attn = [_dot_nt(stack_heads(q), k.astype(BF16)) * masks_ref[nlev] for q, k in zip(qs, ks)]
    for lv in range(nlev):
        decs = [jnp.exp(x[(1 + lv) * chunk:(2 + lv) * chunk]) for x in e_alls]
        attn = [a + _dot_nt(stack_heads(q * d), (k * d).astype(BF16)) * masks_ref[lv]
                for a, q, k, d in zip(attn, qs, ks, decs)]
    attn = [a.astype(BF16) for a in attn]
    for c, rows in enumerate(chunks):
        for hd in range(nh):
            vh = v_ref[rows, hd * dv:(hd + 1) * dv].astype(BF16)
            o_ref[rows, hd * dv:(hd + 1) * dv] = _dot(attn[c][hd * chunk:(hd + 1) * chunk], vh)
    pending = []
    for q, k, x in zip(qs, ks, e_alls):
        b = x[:chunk]
        b_end = b[chunk - 1:chunk]
        pending.append((stack_heads(q * jnp.exp(b)), (k * jnp.exp(b_end - b)).astype(BF16), jnp.exp(b_end)))

    st = st_ref[...]
    head_of_state_lane = lax.broadcasted_iota(jnp.int32, st.shape, 1) // dk
    for c in range(n_sub):
        rows = slice(c * chunk, (c + 1) * chunk)
        qg, kg, dec_end = pending[c]
        inter = _dot_nt(qg, st.astype(BF16))
        new = st * dec_end
        for hd in range(nh):
            cols = slice(hd * dv, (hd + 1) * dv)
            o_ref[rows, cols] = o_ref[rows, cols] + inter[hd * chunk:(hd + 1) * chunk]
            upd = _dot_tn(v_ref[rows, cols].astype(BF16), kg)
            new = new + jnp.where(head_of_state_lane == hd, upd, 0.0)
        st = new
    st_ref[...] = st


def _gla(gq, gk, gv, log_a):
    s = gq.shape[0]
    chunk = min(SCAN_CHUNK, s)
    n_sub = min(GLA_CHUNKS_PER_STEP, s // chunk)
    rows = chunk * n_sub
    mats, masks = _gla_constants(chunk)
    gdk = GLA_HEADS * GLA_DK
    gdv = GLA_HEADS * GLA_DV
    return pl.pallas_call(
        functools.partial(_gla_body, chunk=chunk, n_sub=n_sub),
        grid=(s // rows,),
        in_specs=[_rows(rows, gdk), _rows(rows, gdk), _rows(rows, gdv), _rows(rows, gdk),
                  _resident(mats.shape), _resident(masks.shape)],
        out_specs=_rows(rows, gdv),
        out_shape=jax.ShapeDtypeStruct((s, gdv), F32),
        scratch_shapes=[pltpu.VMEM((GLA_DV, gdk), F32)],
        compiler_params=_cparams(("arbitrary",)),
        name="gla_scan",
    )(gq, gk, gv, log_a, mats, masks)


def _xattn_memory(mem_ref, gmem_ref, wk_ref, wv_ref, k_scr, v_scr):
    @pl.when(pl.program_id(0) == 0)
    def _():
        mem_n = _rms(mem_ref[...], gmem_ref[...]).astype(BF16)
        k_scr[...] = _dot(mem_n, wk_ref[...].astype(BF16)).astype(BF16)
        v_scr[...] = _dot(mem_n, wv_ref[...].astype(BF16)).astype(BF16)


def _xattn_rows(h, gpre_ref, wq_ref, wo_ref, gpost_ref, k_scr, v_scr):
    hn = _rms(h, gpre_ref[...]).astype(BF16)
    q = _dot(hn, wq_ref[...].astype(BF16)).astype(BF16)
    dh = XATTN_DH
    scale = dh ** -0.5
    outs = []
    for hd in range(XATTN_HEADS):
        sl = slice(hd * dh, (hd + 1) * dh)
        s = _dot_nt(q[:, sl], k_scr[:, sl]) * scale
        p = jnp.exp(s - jnp.max(s, axis=-1, keepdims=True))
        inv_l = 1.0 / jnp.sum(p, axis=-1, keepdims=True)
        outs.append((_dot(p.astype(BF16), v_scr[:, sl]) * inv_l).astype(BF16))
    o = jnp.concatenate(outs, axis=1)
    return h + _rms(_dot(o, wo_ref[...].astype(BF16)), gpost_ref[...])


def _layer_resident(shape, layer):
    return pl.BlockSpec((None,) + tuple(shape[1:]), lambda *_: (layer, 0, 0), pipeline_mode=pl.Buffered(1))


def _xattn_operands(xattn_args, layer):
    mem, g_mem, g_pre, w_q, w_k, w_v, w_o, g_post = xattn_args
    ins = (mem, g_mem, g_pre, w_q, w_k, w_v, w_o, g_post)
    specs = ([_resident(a.shape) for a in (mem, g_mem, g_pre)]
             + [_layer_resident(w.shape, layer) for w in (w_q, w_k, w_v, w_o)] + [_resident(g_post.shape)])
    scratch = [pltpu.VMEM((mem.shape[0], D_MODEL), BF16), pltpu.VMEM((mem.shape[0], D_MODEL), BF16)]
    return ins, specs, scratch


def _even_out_body(h_ref, at_ref, o_ref, r_ref, ggla_ref, wa_ref, wg_ref, gpost_ref,
                   mem_ref, gmem_ref, xpre_ref, wq_ref, wk_ref, wv_ref, wo_ref, xpost_ref, out_ref, k_scr, v_scr):
    _xattn_memory(mem_ref, gmem_ref, wk_ref, wv_ref, k_scr, v_scr)
    og = o_ref[...]
    gn = jnp.concatenate([_rms(og[:, hd * GLA_DV:(hd + 1) * GLA_DV], ggla_ref[...]) for hd in range(GLA_HEADS)],
                         axis=1)
    g = (gn * _silu(r_ref[...])).astype(BF16)
    mix = _dot_tn(at_ref[...], wa_ref[...]) + _dot(g, wg_ref[...])
    h = h_ref[...] + _rms(mix, gpost_ref[...])
    out_ref[...] = _xattn_rows(h, xpre_ref, wq_ref, wo_ref, xpost_ref, k_scr, v_scr)


def _even_out_xattn(h, a_t, o_gla, r, g_gla, w_a, w_g, g_post, xattn_args, layer):
    s = h.shape[0]
    tm = min(ROW_TILE, s)
    x_ins, x_specs, x_scratch = _xattn_operands(xattn_args, layer)
    return pl.pallas_call(
        _even_out_body,
        grid=(s // tm,),
        in_specs=[_rows(tm, D_MODEL), pl.BlockSpec((a_t.shape[0], tm), lambda i: (0, i)),
                  _rows(tm, o_gla.shape[1]), _rows(tm, r.shape[1]),
                  _resident(g_gla.shape), _resident(w_a.shape), _resident(w_g.shape), _resident(g_post.shape)] + x_specs,
        out_specs=_rows(tm, D_MODEL),
        out_shape=jax.ShapeDtypeStruct((s, D_MODEL), F32),
        scratch_shapes=x_scratch,
        compiler_params=_cparams(("arbitrary",)),
        name="even_out_xattn",
    )(h, a_t, o_gla, r, g_gla, w_a, w_g, g_post, *x_ins)


CONV_HALO = 8


def _odd_proj_body(h_ref, gpre_ref, win_ref, cw_ref, cb_ref, wq_ref, wk_ref, wkt_ref, wv_ref, wgq_ref, wgk_ref,
                   wgv_ref, bg_ref, q_out, k_out, kt_out, v_out, gates_out, xc_out, z_out, xbuf):
    tm = h_ref.shape[0]
    width = D_MODEL

    @pl.when(pl.program_id(0) == 0)
    def _():
        xbuf[0:CONV_HALO, :] = jnp.zeros((CONV_HALO, width), F32)

    hn = _rms(h_ref[...], gpre_ref[...]).astype(BF16)
    xz = _dot(hn, win_ref[...])
    x_m = xz[:, :width]
    z_out[...] = xz[:, width:]
    xbuf[CONV_HALO:CONV_HALO + tm, :] = x_m
    conv = cb_ref[...] + cw_ref[MLSTM_CONV - 1:MLSTM_CONV, :] * x_m
    for j in range(MLSTM_CONV - 1):
        back = MLSTM_CONV - 1 - j
        conv = conv + cw_ref[j:j + 1, :] * xbuf[CONV_HALO - back:CONV_HALO - back + tm, :]
    xbuf[0:CONV_HALO, :] = x_m[tm - CONV_HALO:, :]
    x_c = _silu(conv)
    xc_out[...] = x_c
    xcb = x_c.astype(BF16)

    dh = MLSTM_DH
    xmb = x_m.astype(BF16)
    qs, ks, vs = [], [], []
    for hd in range(MLSTM_HEADS):
        sl = slice(hd * dh, (hd + 1) * dh)
        qs.append(_dot(xcb[:, sl], wq_ref[hd]))
        ks.append(_dot(xcb[:, sl], wk_ref[hd]))
        vs.append(_dot(xmb[:, sl], wv_ref[hd]))
        kt_out[sl, :] = _dot_nt(wkt_ref[hd], xcb[:, sl]).astype(BF16)
    q = jnp.concatenate(qs, axis=1).astype(BF16)
    k = jnp.concatenate(ks, axis=1).astype(BF16)
    v = jnp.concatenate(vs, axis=1).astype(BF16)
    gates_out[...] = _dot(q, wgq_ref[...]) + _dot(k, wgk_ref[...]) + _dot(v, wgv_ref[...]) + bg_ref[...]
    q_out[...] = (q.astype(F32) * (dh ** -0.5)).astype(BF16)
    k_out[...] = k
    v_out[...] = v


def _odd_proj(h, g_pre, w_in, conv_w, conv_b, w_q, w_k, w_kt, w_v, wg_q, wg_k, wg_v, b_g):
    s = h.shape[0]
    tm = min(ROW_TILE, s)
    rows = [(D_MODEL, BF16), (D_MODEL, BF16), None, (D_MODEL, BF16), (V7X_LANES, F32), (D_MODEL, F32),
            (D_MODEL, F32)]
    ins = (g_pre, w_in, conv_w, conv_b, w_q, w_k, w_kt, w_v, wg_q, wg_k, wg_v, b_g)
    kt_spec = pl.BlockSpec((D_MODEL, tm), lambda i: (0, i))
    kt_shape = jax.ShapeDtypeStruct((D_MODEL, s), BF16)
    return pl.pallas_call(
        _odd_proj_body,
        grid=(s // tm,),
        in_specs=[_rows(tm, D_MODEL)] + [_resident(a.shape) for a in ins],
        out_specs=[kt_spec if r is None else _rows(tm, r[0]) for r in rows],
        out_shape=[kt_shape if r is None else jax.ShapeDtypeStruct((s, r[0]), r[1]) for r in rows],
        scratch_shapes=[pltpu.VMEM((CONV_HALO + tm, D_MODEL), F32)],
        compiler_params=_cparams(("arbitrary",)),
        name="odd_proj",
    )(h, *ins)


def _prep_odd_gate_weights(w_gates, b_gates):
    nh, dh = MLSTM_HEADS, MLSTM_DH
    w4 = w_gates.reshape(nh, 3, dh, 2 * nh)
    pad = jnp.zeros((nh * dh, V7X_LANES - 2 * nh), F32)
    parts = [jnp.concatenate([w4[:, j].reshape(nh * dh, 2 * nh), pad], axis=1).astype(BF16) for j in range(3)]
    b = jnp.concatenate([b_gates, jnp.zeros((V7X_LANES - 2 * nh,), F32)]).reshape(1, V7X_LANES)
    return parts[0], parts[1], parts[2], b


MLSTM_EXT = MLSTM_DH + V7X_LANES


def _mlstm_body(q_ref, k_ref, kt_ref, v_ref, gcol_ref, grow_ref, tri_ref, trit_ref, o_ref, c_scr, m_scr, *,
                chunk, n_sub):
    nh, dh = MLSTM_HEADS, MLSTM_DH
    lanes = V7X_LANES

    @pl.when(pl.program_id(0) == 0)
    def _():
        c_scr[...] = jnp.zeros(c_scr.shape, F32)
        m_scr[...] = jnp.zeros(m_scr.shape, F32)

    row = lax.broadcasted_iota(jnp.int32, (chunk, chunk), 0)
    col = lax.broadcasted_iota(jnp.int32, (chunk, chunk), 1)
    causal = col <= row
    ones_col = jnp.where(lax.broadcasted_iota(jnp.int32, (chunk, lanes), 1) == 0, 1.0, 0.0).astype(BF16)

    chunks = [slice(c * chunk, (c + 1) * chunk) for c in range(n_sub)]
    heads = [slice(hd * dh, (hd + 1) * dh) for hd in range(nh)]
    pairs = [(c, hd) for c in range(n_sub) for hd in range(nh)]
    b_cols = [_dot(tri_ref[...], jnp.concatenate(_split3(_log_sigmoid(gcol_ref[rows, :])), axis=1))
              for rows in chunks]
    b_cols = [x[:, :lanes] + x[:, lanes:2 * lanes] + x[:, 2 * lanes:] for x in b_cols]
    g_rows = [grow_ref[c] for c in range(n_sub)]
    nr = g_rows[0].shape[0]
    b_rows = [_dot(jnp.concatenate(_split3(_log_sigmoid(gr)), axis=0), trit_ref[...]) for gr in g_rows]
    b_rows = [x[:nr] + x[nr:2 * nr] + x[2 * nr:] for x in b_rows]
    r_h = {(c, hd): g_rows[c][hd:hd + 1, :] - b_rows[c][nh + hd:nh + hd + 1, :] for c, hd in pairs}
    b_h = {(c, hd): b_cols[c][:, nh + hd:nh + hd + 1] for c, hd in pairs}
    logw = {p: jnp.where(causal, r_h[p], NEG_BIG) for p in pairs}
    cm = {p: jnp.max(logw[p], axis=-1, keepdims=True) for p in pairs}
    qk = {(c, hd): _dot_nt(q_ref[chunks[c], heads[hd]], k_ref[chunks[c], heads[hd]]) for c, hd in pairs}
    w_intra = {p: (jnp.exp(logw[p] - cm[p]) * qk[p]).astype(BF16) for p in pairs}
    vext = {(c, hd): jnp.concatenate([v_ref[chunks[c], heads[hd]], ones_col], axis=1) for c, hd in pairs}
    intra = {p: _dot(w_intra[p], vext[p]) for p in pairs}

    for c, hd in pairs:
        p = (c, hd)
        rows, sl = chunks[c], heads[hd]
        m_old = m_scr[hd:hd + 1, 0:1]
        cext = c_scr[hd]
        big_m = jnp.maximum(cm[p], m_old)
        w_inter = jnp.exp(m_old - big_m)
        out = (jnp.exp(cm[p] - big_m) * intra[p]
               + _dot((q_ref[rows, sl].astype(F32) * w_inter).astype(BF16), cext.astype(BF16)))
        den = out[:, dh:dh + 1]
        o_ref[rows, sl] = out[:, :dh] / jnp.maximum(jnp.abs(den), jnp.exp(-(b_h[p] + big_m)))

        m_last = big_m[chunk - 1:chunk, :]
        w_s = jnp.exp(r_h[p] - m_last)
        ktw = (kt_ref[sl, rows].astype(F32) * w_s).astype(BF16)
        c_scr[hd] = jnp.exp(m_old - m_last) * cext + _dot(ktw, vext[p])
        m_scr[hd:hd + 1, :] = jnp.broadcast_to(b_h[p][chunk - 1:chunk, :] + m_last, (1, m_scr.shape[1]))


def _mlstm(q, k, kt, v, gates):
    s = q.shape[0]
    chunk = min(SCAN_CHUNK, s)
    nc = s // chunk
    n_sub = min(MLSTM_CHUNKS_PER_STEP, nc)
    rows = chunk * n_sub
    lanes = V7X_LANES
    ng = 2 * MLSTM_HEADS
    grow = gates[:, :ng].reshape(nc, chunk, ng).transpose(0, 2, 1)
    t = np.arange(chunk)
    tri = (t[None, :] <= t[:, None]).astype(np.float32)
    return pl.pallas_call(
        functools.partial(_mlstm_body, chunk=chunk, n_sub=n_sub),
        grid=(s // rows,),
        in_specs=[_rows(rows, D_MODEL), _rows(rows, D_MODEL), pl.BlockSpec((D_MODEL, rows), lambda c: (0, c)),
                  _rows(rows, D_MODEL), _rows(rows, lanes),
                  pl.BlockSpec((n_sub, ng, chunk), lambda c: (c, 0, 0)),
                  _resident((chunk, chunk)), _resident((chunk, chunk))],
        out_specs=_rows(rows, D_MODEL),
        out_shape=jax.ShapeDtypeStruct((s, D_MODEL), F32),
        scratch_shapes=[pltpu.VMEM((MLSTM_HEADS, MLSTM_DH, MLSTM_EXT), F32),
                        pltpu.VMEM((8, lanes), F32)],
        compiler_params=_cparams(("arbitrary",)),
        name="mlstm_scan",
    )(q, k, kt, v, gates, grow, jnp.asarray(tri, BF16), jnp.asarray(tri.T, BF16))


def _odd_out_body(h_ref, hc_ref, xc_ref, z_ref, gh_ref, skip_ref, wo_ref, gpost_ref,
                  mem_ref, gmem_ref, xpre_ref, wq_ref, wk_ref, wv_ref, xwo_ref, xpost_ref, out_ref, k_scr, v_scr):
    _xattn_memory(mem_ref, gmem_ref, wk_ref, wv_ref, k_scr, v_scr)
    hc = hc_ref[...]
    dh = MLSTM_DH
    parts = []
    for hd in range(MLSTM_HEADS):
        seg = hc[:, hd * dh:(hd + 1) * dh]
        cen = seg - jnp.mean(seg, axis=-1, keepdims=True)
        parts.append(cen * lax.rsqrt(jnp.mean(cen * cen, axis=-1, keepdims=True) + EPS))
    hn = jnp.concatenate(parts, axis=1) * gh_ref[...]
    out = ((hn + skip_ref[...] * xc_ref[...]) * _silu(z_ref[...])).astype(BF16)
    h = h_ref[...] + _rms(_dot(out, wo_ref[...]), gpost_ref[...])
    out_ref[...] = _xattn_rows(h, xpre_ref, wq_ref, xwo_ref, xpost_ref, k_scr, v_scr)


def _odd_out_xattn(h, hcell, x_c, z, g_hnorm, skip, w_out, g_post, xattn_args, layer):
    s = h.shape[0]
    tm = min(ROW_TILE, s)
    x_ins, x_specs, x_scratch = _xattn_operands(xattn_args, layer)
    return pl.pallas_call(
        _odd_out_body,
        grid=(s // tm,),
        in_specs=[_rows(tm, D_MODEL)] * 4 + [_resident(g_hnorm.shape), _resident(skip.shape),
                                              _resident(w_out.shape), _resident(g_post.shape)] + x_specs,
        out_specs=_rows(tm, D_MODEL),
        out_shape=jax.ShapeDtypeStruct((s, D_MODEL), F32),
        scratch_shapes=x_scratch,
        compiler_params=_cparams(("arbitrary",)),
        name="odd_out_xattn",
    )(h, hcell, x_c, z, g_hnorm, skip, w_out, g_post, *x_ins)


def _ffn_body(h_ref, gpre_ref, w1_ref, w2_ref, gpost_ref, out_ref):
    h = h_ref[...]
    xn = _rms(h, gpre_ref[...]).astype(BF16)
    acc = None
    for c in range(D_FF // FF_CHUNK):
        sl = slice(c * FF_CHUNK, (c + 1) * FF_CHUNK)
        a = jnp.square(jnp.maximum(_dot(xn, w1_ref[:, sl].astype(BF16)), 0.0)).astype(BF16)
        part = _dot(a, w2_ref[sl, :].astype(BF16))
        acc = part if acc is None else acc + part
    out_ref[...] = h + _rms(acc, gpost_ref[...])


def _ffn(h, g_pre, w1, w2, g_post, layer):
    s = h.shape[0]
    tm = min(ROW_TILE, s)
    return pl.pallas_call(
        _ffn_body,
        grid=(s // tm,),
        in_specs=[_rows(tm, D_MODEL), _resident(g_pre.shape), _layer_resident(w1.shape, layer),
                  _layer_resident(w2.shape, layer), _resident(g_post.shape)],
        out_specs=_rows(tm, D_MODEL),
        out_shape=jax.ShapeDtypeStruct((s, D_MODEL), F32),
        compiler_params=_cparams(("parallel",)),
        name="sq_relu_mlp",
    )(h, g_pre, w1, w2, g_post)


def _row(g):
    return g.reshape(1, -1)


def kernel(x, mem, positions, g_mix_pre, g_mix_post, g_xattn_pre, g_xattn_post, g_mem, g_ffn_pre, g_ffn_post, ev_w_in, ev_g_q, ev_w_uq, ev_g_kv, ev_w_ukv, ev_w_gate, ev_b_gate, ev_g_gla, ev_w_out, od_w_in, od_conv_w, od_conv_b, od_w_q, od_w_k, od_w_v, od_w_gates, od_b_gates, od_g_hnorm, od_skip, od_w_out, xa_w_q, xa_w_k, xa_w_v, xa_w_o, ffn_w1, ffn_w2):
    batch, seq, d = x.shape
    assert batch == 1 and d == D_MODEL and seq % ROW_TILE == 0 and seq % (2 * ATTN_TK) == 0
    assert seq % (SCAN_CHUNK * GLA_CHUNKS_PER_STEP) == 0 and seq % (SCAN_CHUNK * MLSTM_CHUNKS_PER_STEP) == 0
    assert ROW_TILE == ATTN_TK
    h = x.reshape(seq, d)
    mem2 = mem.reshape(mem.shape[1], d)
    cos_t, sin_t = _rope_tables(positions)
    n_a = MLA_HEADS * MLA_V

    for layer in range(DEPTH):
        j = layer // 2
        xattn_args = (mem2, _row(g_mem[layer]), _row(g_xattn_pre[layer]), xa_w_q, xa_w_k, xa_w_v, xa_w_o,
                      _row(g_xattn_post[layer]))
        if layer % 2 == 0:
            w_in_p, w_uq_p, w_uk_p, w_uvt_p, w_gate_p, b_gate = _prep_even_weights(
                ev_w_in[j], ev_w_uq[j], ev_w_ukv[j], ev_w_gate[j], ev_b_gate[j])
            q, k, vt, gq, gk, gv, log_a, r = _even_proj(
                h, _row(g_mix_pre[layer]), w_in_p, _row(ev_g_q[j]), w_uq_p, _row(ev_g_kv[j]), w_uk_p, w_uvt_p,
                w_gate_p, b_gate, cos_t, sin_t)
            a_t = _mla_attn(q, k, vt)
            o_gla = _gla(gq, gk, gv, log_a)
            w_out = ev_w_out[j].astype(BF16)
            h = _even_out_xattn(h, a_t, o_gla, r, _row(ev_g_gla[j]), w_out[:n_a], w_out[n_a:],
                                _row(g_mix_post[layer]), xattn_args, layer)
        else:
            wg_q, wg_k, wg_v, b_g = _prep_odd_gate_weights(od_w_gates[j], od_b_gates[j])
            w_k = od_w_k[j].astype(BF16)
            q, k, kt, v, gates, x_c, z = _odd_proj(
                h, _row(g_mix_pre[layer]), od_w_in[j].astype(BF16), od_conv_w[j], _row(od_conv_b[j]),
                od_w_q[j].astype(BF16), w_k, w_k.transpose(0, 2, 1), od_w_v[j].astype(BF16), wg_q, wg_k, wg_v, b_g)
            hcell = _mlstm(q, k, kt, v, gates)
            h = _odd_out_xattn(h, hcell, x_c, z, _row(od_g_hnorm[j]), _row(od_skip[j]), od_w_out[j].astype(BF16),
                               _row(g_mix_post[layer]), xattn_args, layer)
        h = _ffn(h, _row(g_ffn_pre[layer]), ffn_w1, ffn_w2, _row(g_ffn_post[layer]), layer)
    return h.reshape(batch, seq, d)
```

```python
import functools

import numpy as np
import jax
import jax.numpy as jnp
from jax import lax
from jax.experimental import pallas as pl
from jax.experimental.pallas import tpu as pltpu

F32 = jnp.float32
BF16 = jnp.bfloat16

D_MODEL = 1024
DEPTH = 4
EPS = 1e-6
MLA_HEADS = 8
MLA_NOPE = 64
MLA_ROPE = 32
MLA_V = 64
MLA_Q_RANK = 256
MLA_KV_RANK = 128
ROPE_THETA = 10000.0
GLA_HEADS = 4
GLA_DK = 64
GLA_DV = 128
GLA_GATE_RANK = 16
GLA_TAU = 16.0
MLSTM_HEADS = 4
MLSTM_DH = 256
MLSTM_CONV = 4
XATTN_HEADS = 4
XATTN_DH = 256
D_FF = 4096
EVEN_SPLITS = (256, 128, 32, 256, 256, 512, 16, 512)

V7X_LANES = 128
V7X_VMEM_BYTES = 64 * 1024 * 1024
VMEM_LIMIT = (V7X_VMEM_BYTES * 7) // 8

ROW_TILE = 512
OUT_TILE = 512
OUT_SUBTILE = 256
FF_CHUNK = 1024
ATTN_TK = 512
ATTN_HEADS_PER_STEP = 2
SCAN_CHUNK = 128
GLA_CHUNKS_PER_STEP = 4
MLSTM_CHUNKS_PER_STEP = 4
HEAD_PAD = V7X_LANES
ATTN_VROWS = MLA_V + 16

LOG2E = 1.4426950408889634
NEG_BIG = -1e30


def _cparams(sem):
    return pltpu.CompilerParams(dimension_semantics=sem, vmem_limit_bytes=VMEM_LIMIT)


def _resident(shape):
    nd = len(shape)
    return pl.BlockSpec(shape, lambda *_: (0,) * nd, pipeline_mode=pl.Buffered(1))


def _rows(tile, width):
    return pl.BlockSpec((tile, width), lambda i: (i, 0))


def _rms(x, g):
    return x * lax.rsqrt(jnp.mean(x * x, axis=-1, keepdims=True) + EPS) * g


def _silu(x):
    return x * (1.0 / (1.0 + jnp.exp(-x)))


def _log_sigmoid(x):
    return jnp.minimum(x, 0.0) - jnp.log(1.0 + jnp.exp(-jnp.abs(x)))


def _dot(a, b):
    return jnp.dot(a, b, preferred_element_type=F32)


def _dot_nt(a, b):
    return lax.dot_general(a, b, (((1,), (1,)), ((), ())), preferred_element_type=F32)


def _dot_tn(a, b):
    return lax.dot_general(a, b, (((0,), (0,)), ((), ())), preferred_element_type=F32)


def _split2(x):
    x1 = x.astype(BF16)
    return x1, (x - x1.astype(F32)).astype(BF16)


def _split3(x):
    x1 = x.astype(BF16)
    r = x - x1.astype(F32)
    x2 = r.astype(BF16)
    x3 = (r - x2.astype(F32)).astype(BF16)
    return x1, x2, x3


def _rope_body(pos_ref, freq_ref, cos_ref, sin_ref):
    ang = pos_ref[...] * freq_ref[...]
    lane = lax.broadcasted_iota(jnp.int32, ang.shape, 1)
    on = jnp.logical_and(lane >= MLA_NOPE, lane < MLA_NOPE + MLA_ROPE)
    cos_ref[...] = jnp.where(on, jnp.cos(ang), 0.0)
    sin_ref[...] = jnp.where(on, jnp.sin(ang), 0.0)


def _rope_tables(positions):
    s = positions.shape[-1]
    pos = positions.astype(F32).reshape(s, 1)
    inv_freq = ROPE_THETA ** (-jnp.arange(0, MLA_ROPE, 2, dtype=F32) / MLA_ROPE)
    freq = jnp.concatenate([jnp.zeros((MLA_NOPE,), F32), inv_freq, inv_freq,
                            jnp.zeros((HEAD_PAD - MLA_NOPE - MLA_ROPE,), F32)]).reshape(1, HEAD_PAD)
    tile = min(2048, s)
    return pl.pallas_call(
        _rope_body,
        grid=(s // tile,),
        in_specs=[_rows(tile, 1), _resident((1, HEAD_PAD))],
        out_specs=[_rows(tile, HEAD_PAD), _rows(tile, HEAD_PAD)],
        out_shape=[jax.ShapeDtypeStruct((s, HEAD_PAD), F32)] * 2,
        compiler_params=_cparams(("parallel",)),
        name="rope_tables",
    )(pos, freq)


_EV_OFF = np.cumsum((0, 256, 128, 128, 128, 256, 256, 512, 128, 512))


def _even_proj_body(h_ref, gpre_ref, win_ref, gq_ref, wuq_ref, gkv_ref, wuk_ref, wuvt_ref, wgate_ref, bgate_ref,
                    cos_ref, sin_ref,
                    q_ref, k_ref, vt_ref, gq_out, gk_out, gv_out, la_out, r_out):
    o = _EV_OFF
    hn = _rms(h_ref[...], gpre_ref[...]).astype(BF16)
    proj = _dot(hn, win_ref[...])
    c_q = proj[:, o[0]:o[1]]
    c_kv = proj[:, o[1]:o[2]]
    kpe_a = proj[:, o[2]:o[3]]
    kpe_b = proj[:, o[3]:o[4]]
    cosk = cos_ref[...]
    sink = sin_ref[...]
    nh = MLA_HEADS
    width = nh * HEAD_PAD

    qs = (MLA_NOPE + MLA_ROPE) ** -0.5 * LOG2E
    lane = lax.broadcasted_iota(jnp.int32, cosk.shape, 1)
    cosq = qs * jnp.where(lane < MLA_NOPE, 1.0, cosk)
    sinq = qs * sink
    cqn = _rms(c_q, gq_ref[...]).astype(BF16)
    qab = _dot(cqn, wuq_ref[...])
    q = qab[:, :width] * jnp.tile(cosq, (1, nh)) + qab[:, width:] * jnp.tile(sinq, (1, nh))
    q_ref[...] = q.astype(BF16)

    ckvn = _rms(c_kv, gkv_ref[...]).astype(BF16)
    kpe = kpe_a * cosk + kpe_b * sink
    k_ref[...] = (_dot(ckvn, wuk_ref[...]) + jnp.tile(kpe, (1, nh))).astype(BF16)
    vt = _dot_nt(wuvt_ref[...], ckvn).astype(BF16)
    row = lax.broadcasted_iota(jnp.int32, (ATTN_VROWS - MLA_V, vt.shape[1]), 0)
    extra = jnp.where(row == 0, 1.0, 0.0).astype(BF16)
    for hd in range(nh):
        vt_ref[hd, 0:MLA_V, :] = vt[hd * MLA_V:(hd + 1) * MLA_V]
        vt_ref[hd, MLA_V:ATTN_VROWS, :] = extra

    gq_out[...] = proj[:, o[4]:o[5]]
    gk_out[...] = proj[:, o[5]:o[6]]
    gv_out[...] = proj[:, o[6]:o[7]]
    glr = proj[:, o[7]:o[8]].astype(BF16)
    x = _dot(glr, wgate_ref[...]) + bgate_ref[...]
    la_out[...] = _log_sigmoid(x) * (1.0 / GLA_TAU)
    r_out[...] = proj[:, o[8]:o[9]]


def _even_proj(h, g_pre, w_in, g_q, w_uq, g_kv, w_uk, w_uvt, w_gate, b_gate, cos_t, sin_t):
    s = h.shape[0]
    tm = min(ROW_TILE, s)
    width = MLA_HEADS * HEAD_PAD
    gdk = GLA_HEADS * GLA_DK
    gdv = GLA_HEADS * GLA_DV
    rows = [(width, BF16), (width, BF16), None, (gdk, F32), (gdk, F32), (gdv, F32), (gdk, F32), (gdv, F32)]
    vt_spec = pl.BlockSpec((MLA_HEADS, None, ATTN_VROWS, tm), lambda i: (0, i, 0, 0))
    vt_shape = jax.ShapeDtypeStruct((MLA_HEADS, s // tm, ATTN_VROWS, tm), BF16)
    return pl.pallas_call(
        _even_proj_body,
        grid=(s // tm,),
        in_specs=[_rows(tm, D_MODEL), _resident(g_pre.shape), _resident(w_in.shape), _resident(g_q.shape),
                  _resident(w_uq.shape), _resident(g_kv.shape), _resident(w_uk.shape), _resident(w_uvt.shape),
                  _resident(w_gate.shape), _resident(b_gate.shape), _rows(tm, HEAD_PAD), _rows(tm, HEAD_PAD)],
        out_specs=[vt_spec if r is None else _rows(tm, r[0]) for r in rows],
        out_shape=[vt_shape if r is None else jax.ShapeDtypeStruct((s, r[0]), r[1]) for r in rows],
        compiler_params=_cparams(("parallel",)),
        name="even_proj",
    )(h, g_pre, w_in, g_q, w_uq, g_kv, w_uk, w_uvt, w_gate, b_gate, cos_t, sin_t)


def _prep_even_weights(w_in, w_uq, w_ukv, w_gate, b_gate):
    d = w_in.shape[0]
    off = np.cumsum((0,) + EVEN_SPLITS)
    seg = [w_in[:, off[i]:off[i + 1]] for i in range(len(EVEN_SPLITS))]
    c_q, c_kv, k_pe, gq, gk, gv, glr, r = seg
    half = MLA_ROPE // 2
    z = lambda n: jnp.zeros((d, n), F32)
    pad = HEAD_PAD - MLA_NOPE - MLA_ROPE
    kpe_a = jnp.concatenate([z(MLA_NOPE), k_pe, z(pad)], axis=1)
    kpe_b = jnp.concatenate([z(MLA_NOPE), -k_pe[:, half:], k_pe[:, :half], z(pad)], axis=1)
    glr_p = jnp.concatenate([glr, z(HEAD_PAD - GLA_GATE_RANK)], axis=1)
    w_in_p = jnp.concatenate([c_q, c_kv, kpe_a, kpe_b, gq, gk, gv, glr_p, r], axis=1).astype(BF16)

    nh = MLA_HEADS
    wq3 = w_uq.reshape(MLA_Q_RANK, nh, MLA_NOPE + MLA_ROPE)
    nope, rope = wq3[..., :MLA_NOPE], wq3[..., MLA_NOPE:]
    zq = lambda n: jnp.zeros((MLA_Q_RANK, nh, n), F32)
    wa = jnp.concatenate([nope, rope, zq(pad)], axis=-1).reshape(MLA_Q_RANK, nh * HEAD_PAD)
    wb = jnp.concatenate([zq(MLA_NOPE), -rope[..., half:], rope[..., :half], zq(pad)],
                         axis=-1).reshape(MLA_Q_RANK, nh * HEAD_PAD)
    w_uq_p = jnp.concatenate([wa, wb], axis=1).astype(BF16)

    wkv3 = w_ukv.reshape(MLA_KV_RANK, nh, MLA_NOPE + MLA_V)
    zk = jnp.zeros((MLA_KV_RANK, nh, HEAD_PAD - MLA_NOPE), F32)
    w_uk_p = jnp.concatenate([wkv3[..., :MLA_NOPE], zk], axis=-1).reshape(MLA_KV_RANK, nh * HEAD_PAD).astype(BF16)
    w_uvt_p = wkv3[..., MLA_NOPE:].reshape(MLA_KV_RANK, nh * MLA_V).T.astype(BF16)

    w_gate_p = jnp.concatenate([w_gate, jnp.zeros((HEAD_PAD - GLA_GATE_RANK, w_gate.shape[1]), F32)],
                               axis=0).astype(BF16)
    return w_in_p, w_uq_p, w_uk_p, w_uvt_p, w_gate_p, b_gate.reshape(1, -1)


def _mla_attn_body(q_ref, k_ref, vt_ref, o_ref, acc_scr, st_scr, p_scr, m_scr, alpha_scr, *, tq, tk, hg):
    i = pl.program_id(1)
    heads = range(hg)

    def scores(t, slot):
        start = pl.multiple_of(t * tk, tk)
        for hd in heads:
            lanes = slice(hd * HEAD_PAD, (hd + 1) * HEAD_PAD)
            st_scr[slot, hd] = _dot_nt(k_ref[pl.ds(start, tk), lanes], q_ref[:, lanes])

    def accumulate(t, slot):
        for hd in heads:
            acc_scr[hd] = alpha_scr[slot, hd] * acc_scr[hd] + _dot(vt_ref[hd, t], p_scr[slot, hd])

    def softmax(slot, mask_shift):
        for hd in heads:
            st = st_scr[slot, hd]
            if mask_shift is not None:
                key = lax.broadcasted_iota(jnp.int32, st.shape, 0) + mask_shift
                qry = lax.broadcasted_iota(jnp.int32, st.shape, 1)
                st = jnp.where(key <= qry, st, NEG_BIG)
            m_prev = m_scr[hd]
            m_next = jnp.maximum(m_prev, jnp.max(st, axis=0, keepdims=True))
            p_scr[slot, hd] = jnp.exp2(st - m_next).astype(BF16)
            alpha_scr[slot, hd] = jnp.exp2(m_prev - m_next)
            m_scr[hd] = m_next

    def step(t, slot, mask_shift, more):
        softmax(slot, mask_shift)
        if more:
            scores(t + 1, 1 - slot)
        accumulate(jnp.maximum(t - 1, 0), 1 - slot)

    acc_scr[...] = jnp.zeros(acc_scr.shape, F32)
    p_scr[1] = jnp.zeros(p_scr.shape[1:], BF16)
    alpha_scr[1] = jnp.ones(alpha_scr.shape[1:], F32)
    m_scr[...] = jnp.full(m_scr.shape, NEG_BIG, F32)
    scores(0, 0)

    def pair(u, c):
        step(2 * u, 0, None, True)
        step(2 * u + 1, 1, None, True)
        return c

    lax.fori_loop(0, i, pair, 0)
    step(2 * i, 0, 0, True)
    step(2 * i + 1, 1, tk, False)
    accumulate(2 * i + 1, 1)
    for hd in heads:
        acc = acc_scr[hd]
        o_ref[hd] = (acc[:MLA_V] * (1.0 / acc[MLA_V:MLA_V + 1])).astype(o_ref.dtype)


def _mla_attn(q, k, vt):
    s = q.shape[0]
    nk, tk = vt.shape[1], vt.shape[3]
    tq = 2 * tk
    hg = ATTN_HEADS_PER_STEP
    out = pl.pallas_call(
        functools.partial(_mla_attn_body, tq=tq, tk=tk, hg=hg),
        grid=(MLA_HEADS // hg, s // tq),
        in_specs=[pl.BlockSpec((tq, hg * HEAD_PAD), lambda g, i: (i, g)),
                  pl.BlockSpec((s, hg * HEAD_PAD), lambda g, i: (0, g), pipeline_mode=pl.Buffered(1)),
                  pl.BlockSpec((hg, nk, ATTN_VROWS, tk), lambda g, i: (g, 0, 0, 0), pipeline_mode=pl.Buffered(1))],
        out_specs=pl.BlockSpec((hg, MLA_V, tq), lambda g, i: (g, 0, i)),
        out_shape=jax.ShapeDtypeStruct((MLA_HEADS, MLA_V, s), BF16),
        scratch_shapes=[pltpu.VMEM((hg, ATTN_VROWS, tq), F32),
                        pltpu.VMEM((2, hg, tk, tq), F32),
                        pltpu.VMEM((2, hg, tk, tq), BF16),
                        pltpu.VMEM((hg, 1, tq), F32),
                        pltpu.VMEM((2, hg, 1, tq), F32)],
        compiler_params=_cparams(("parallel", "arbitrary")),
        name="mla_attn",
    )(q, k, vt)
    return out.reshape(MLA_HEADS * MLA_V, s)


def _gla_levels(chunk):
    n = 0
    while (1 << n) < chunk:
        n += 1
    return n


def _gla_constants(chunk):
    t = np.arange(chunk)
    mats = [t[None, :] <= t[:, None]]
    masks = []
    b = 1
    while b < chunk:
        blk = t // (2 * b)
        upper = (t % (2 * b)) >= b
        e = blk * 2 * b + b - 1
        up_rows = upper[:, None] & (t[None, :] > e[:, None]) & (t[None, :] <= t[:, None])
        lo_rows = (~upper)[:, None] & (t[None, :] > t[:, None]) & (t[None, :] <= e[:, None])
        mats.append(up_rows | lo_rows)
        masks.append(upper[:, None] & (~upper)[None, :] & (blk[:, None] == blk[None, :]))
        b *= 2
    masks.append(t[:, None] == t[None, :])
    mats = np.concatenate(mats, axis=0).astype(np.float32)
    masks = np.stack(masks).astype(np.float32)
    masks = np.tile(masks, (1, GLA_HEADS, 1))
    return jnp.asarray(mats, BF16), jnp.asarray(masks, F32)


def _gla_body(q_ref, k_ref, v_ref, g_ref, mats_ref, masks_ref, o_ref, st_ref, *, chunk, n_sub):
    nlev = _gla_levels(chunk)
    nh, dk, dv = GLA_HEADS, GLA_DK, GLA_DV
    w = nh * dk

    @pl.when(pl.program_id(0) == 0)
    def _():
        st_ref[...] = jnp.zeros(st_ref.shape, F32)

    head_of_lane = lax.broadcasted_iota(jnp.int32, (chunk, w), 1) // dk

    def stack_heads(x):
        return jnp.concatenate([jnp.where(head_of_lane == hd, x, 0.0) for hd in range(nh)], axis=0).astype(BF16)

    chunks = [slice(c * chunk, (c + 1) * chunk) for c in range(n_sub)]
    qs = [q_ref[rows, :] * (dk ** -0.5) for rows in chunks]
    ks = [k_ref[rows, :] for rows in chunks]
    e_alls = [_dot(mats_ref[...], jnp.concatenate(_split2(g_ref[rows, :]), axis=1)) for rows in chunks]
    e_alls = [x[:, :w] + x[:, w:] for x in e_alls]
    attn = [_dot_nt(stack_heads(q), k.astype(BF16)) * masks_ref[nlev] for q, k in zip(qs, ks)]
    for lv in range(nlev):
        decs = [jnp.exp(x[(1 + lv) * chunk:(2 + lv) * chunk]) for x in e_alls]
        attn = [a + _dot_nt(stack_heads(q * d), (k * d).astype(BF16)) * masks_ref[lv]
                for a, q, k, d in zip(attn, qs, ks, decs)]
    attn = [a.astype(BF16) for a in attn]
    for c, rows in enumerate(chunks):
        for hd in range(nh):
            vh = v_ref[rows, hd * dv:(hd + 1) * dv].astype(BF16)
            o_ref[rows, hd * dv:(hd + 1) * dv] = _dot(attn[c][hd * chunk:(hd + 1) * chunk], vh)
    pending = []
    for q, k, x in zip(qs, ks, e_alls):
        b = x[:chunk]
        b_end = b[chunk - 1:chunk]
        pending.append((stack_heads(q * jnp.exp(b)), (k * jnp.exp(b_end - b)).astype(BF16), jnp.exp(b_end)))

    st = st_ref[...]
    head_of_state_lane = lax.broadcasted_iota(jnp.int32, st.shape, 1) // dk
    for c in range(n_sub):
        rows = slice(c * chunk, (c + 1) * chunk)
        qg, kg, dec_end = pending[c]
        inter = _dot_nt(qg, st.astype(BF16))
        new = st * dec_end
        for hd in range(nh):
            cols = slice(hd * dv, (hd + 1) * dv)
            o_ref[rows, cols] = o_ref[rows, cols] + inter[hd * chunk:(hd + 1) * chunk]
            upd = _dot_tn(v_ref[rows, cols].astype(BF16), kg)
            new = new + jnp.where(head_of_state_lane == hd, upd, 0.0)
        st = new
    st_ref[...] = st


def _gla(gq, gk, gv, log_a):
    s = gq.shape[0]
    chunk = min(SCAN_CHUNK, s)
    n_sub = min(GLA_CHUNKS_PER_STEP, s // chunk)
    rows = chunk * n_sub
    mats, masks = _gla_constants(chunk)
    gdk = GLA_HEADS * GLA_DK
    gdv = GLA_HEADS * GLA_DV
    return pl.pallas_call(
        functools.partial(_gla_body, chunk=chunk, n_sub=n_sub),
        grid=(s // rows,),
        in_specs=[_rows(rows, gdk), _rows(rows, gdk), _rows(rows, gdv), _rows(rows, gdk),
                  _resident(mats.shape), _resident(masks.shape)],
        out_specs=_rows(rows, gdv),
        out_shape=jax.ShapeDtypeStruct((s, gdv), F32),
        scratch_shapes=[pltpu.VMEM((GLA_DV, gdk), F32)],
        compiler_params=_cparams(("arbitrary",)),
        name="gla_scan",
    )(gq, gk, gv, log_a, mats, masks)


def _xattn_memory(mem_ref, gmem_ref, wk_ref, wv_ref, k_scr, v_scr):
    @pl.when(pl.program_id(0) == 0)
    def _():
        mem_n = _rms(mem_ref[...], gmem_ref[...]).astype(BF16)
        k_scr[...] = _dot(mem_n, wk_ref[...].astype(BF16)).astype(BF16)
        v_scr[...] = _dot(mem_n, wv_ref[...].astype(BF16)).astype(BF16)


def _xattn_rows(h, gpre_ref, wq_ref, wo_ref, gpost_ref, k_scr, v_scr):
    hn = _rms(h, gpre_ref[...]).astype(BF16)
    q = _dot(hn, wq_ref[...].astype(BF16)).astype(BF16)
    dh = XATTN_DH
    scale = dh ** -0.5
    heads = [slice(hd * dh, (hd + 1) * dh) for hd in range(XATTN_HEADS)]
    scores = [_dot_nt(q[:, sl], k_scr[:, sl]) * scale for sl in heads]
    probs = [jnp.exp(s - jnp.max(s, axis=-1, keepdims=True)) for s in scores]
    inv_l = [1.0 / jnp.sum(p, axis=-1, keepdims=True) for p in probs]
    outs = [(_dot(p.astype(BF16), v_scr[:, sl]) * il).astype(BF16) for p, sl, il in zip(probs, heads, inv_l)]
    o = jnp.concatenate(outs, axis=1)
    return h + _rms(_dot(o, wo_ref[...].astype(BF16)), gpost_ref[...])


def _layer_resident(shape, layer):
    return pl.BlockSpec((None,) + tuple(shape[1:]), lambda *_: (layer, 0, 0), pipeline_mode=pl.Buffered(1))


def _subtiles(rows):
    sub = min(OUT_SUBTILE, rows)
    return [slice(i * sub, (i + 1) * sub) for i in range(rows // sub)]


def _xattn_operands(xattn_args, layer):
    mem, g_mem, g_pre, w_q, w_k, w_v, w_o, g_post = xattn_args
    ins = (mem, g_mem, g_pre, w_q, w_k, w_v, w_o, g_post)
    specs = ([_resident(a.shape) for a in (mem, g_mem, g_pre)]
             + [_layer_resident(w.shape, layer) for w in (w_q, w_k, w_v, w_o)] + [_resident(g_post.shape)])
    scratch = [pltpu.VMEM((mem.shape[0], D_MODEL), BF16), pltpu.VMEM((mem.shape[0], D_MODEL), BF16)]
    return ins, specs, scratch


def _even_out_body(h_ref, at_ref, o_ref, r_ref, ggla_ref, wa_ref, wg_ref, gpost_ref,
                   mem_ref, gmem_ref, xpre_ref, wq_ref, wk_ref, wv_ref, wo_ref, xpost_ref, out_ref, k_scr, v_scr):
    _xattn_memory(mem_ref, gmem_ref, wk_ref, wv_ref, k_scr, v_scr)
    subs = _subtiles(h_ref.shape[0])
    hs = []
    for rows in subs:
        og = o_ref[rows, :]
        gn = jnp.concatenate([_rms(og[:, hd * GLA_DV:(hd + 1) * GLA_DV], ggla_ref[...])
                              for hd in range(GLA_HEADS)], axis=1)
        g = (gn * _silu(r_ref[rows, :])).astype(BF16)
        mix = _dot_tn(at_ref[:, rows], wa_ref[...]) + _dot(g, wg_ref[...])
        hs.append(h_ref[rows, :] + _rms(mix, gpost_ref[...]))
    for rows, h in zip(subs, hs):
        out_ref[rows, :] = _xattn_rows(h, xpre_ref, wq_ref, wo_ref, xpost_ref, k_scr, v_scr)


def _even_out_xattn(h, a_t, o_gla, r, g_gla, w_a, w_g, g_post, xattn_args, layer):
    s = h.shape[0]
    tm = min(OUT_TILE, s)
    x_ins, x_specs, x_scratch = _xattn_operands(xattn_args, layer)
    return pl.pallas_call(
        _even_out_body,
        grid=(s // tm,),
        in_specs=[_rows(tm, D_MODEL), pl.BlockSpec((a_t.shape[0], tm), lambda i: (0, i)),
                  _rows(tm, o_gla.shape[1]), _rows(tm, r.shape[1]),
                  _resident(g_gla.shape), _resident(w_a.shape), _resident(w_g.shape), _resident(g_post.shape)] + x_specs,
        out_specs=_rows(tm, D_MODEL),
        out_shape=jax.ShapeDtypeStruct((s, D_MODEL), F32),
        scratch_shapes=x_scratch,
        compiler_params=_cparams(("arbitrary",)),
        name="even_out_xattn",
    )(h, a_t, o_gla, r, g_gla, w_a, w_g, g_post, *x_ins)


CONV_HALO = 8


def _odd_proj_body(h_ref, gpre_ref, win_ref, cw_ref, cb_ref, wq_ref, wk_ref, wkt_ref, wv_ref, wgq_ref, wgk_ref,
                   wgv_ref, bg_ref, q_out, k_out, kt_out, v_out, gates_out, xc_out, z_out, xbuf):
    tm = h_ref.shape[0]
    width = D_MODEL

    @pl.when(pl.program_id(0) == 0)
    def _():
        xbuf[0:CONV_HALO, :] = jnp.zeros((CONV_HALO, width), F32)

    hn = _rms(h_ref[...], gpre_ref[...]).astype(BF16)
    xz = _dot(hn, win_ref[...])
    x_m = xz[:, :width]
    z_out[...] = xz[:, width:]
    xbuf[CONV_HALO:CONV_HALO + tm, :] = x_m
    conv = cb_ref[...] + cw_ref[MLSTM_CONV - 1:MLSTM_CONV, :] * x_m
    for j in range(MLSTM_CONV - 1):
        back = MLSTM_CONV - 1 - j
        conv = conv + cw_ref[j:j + 1, :] * xbuf[CONV_HALO - back:CONV_HALO - back + tm, :]
    xbuf[0:CONV_HALO, :] = x_m[tm - CONV_HALO:, :]
    x_c = _silu(conv)
    xc_out[...] = x_c
    xcb = x_c.astype(BF16)

    dh = MLSTM_DH
    xmb = x_m.astype(BF16)
    qs, ks, vs = [], [], []
    for hd in range(MLSTM_HEADS):
        sl = slice(hd * dh, (hd + 1) * dh)
        qs.append(_dot(xcb[:, sl], wq_ref[hd]))
        ks.append(_dot(xcb[:, sl], wk_ref[hd]))
        vs.append(_dot(xmb[:, sl], wv_ref[hd]))
        kt_out[sl, :] = _dot_nt(wkt_ref[hd], xcb[:, sl]).astype(BF16)
    q = jnp.concatenate(qs, axis=1).astype(BF16)
    k = jnp.concatenate(ks, axis=1).astype(BF16)
    v = jnp.concatenate(vs, axis=1).astype(BF16)
    gates_out[...] = _dot(q, wgq_ref[...]) + _dot(k, wgk_ref[...]) + _dot(v, wgv_ref[...]) + bg_ref[...]
    q_out[...] = (q.astype(F32) * (dh ** -0.5)).astype(BF16)
    k_out[...] = k
    v_out[...] = v


def _odd_proj(h, g_pre, w_in, conv_w, conv_b, w_q, w_k, w_kt, w_v, wg_q, wg_k, wg_v, b_g):
    s = h.shape[0]
    tm = min(ROW_TILE, s)
    rows = [(D_MODEL, BF16), (D_MODEL, BF16), None, (D_MODEL, BF16), (V7X_LANES, F32), (D_MODEL, F32),
            (D_MODEL, F32)]
    ins = (g_pre, w_in, conv_w, conv_b, w_q, w_k, w_kt, w_v, wg_q, wg_k, wg_v, b_g)
    kt_spec = pl.BlockSpec((D_MODEL, tm), lambda i: (0, i))
    kt_shape = jax.ShapeDtypeStruct((D_MODEL, s), BF16)
    return pl.pallas_call(
        _odd_proj_body,
        grid=(s // tm,),
        in_specs=[_rows(tm, D_MODEL)] + [_resident(a.shape) for a in ins],
        out_specs=[kt_spec if r is None else _rows(tm, r[0]) for r in rows],
        out_shape=[kt_shape if r is None else jax.ShapeDtypeStruct((s, r[0]), r[1]) for r in rows],
        scratch_shapes=[pltpu.VMEM((CONV_HALO + tm, D_MODEL), F32)],
        compiler_params=_cparams(("arbitrary",)),
        name="odd_proj",
    )(h, *ins)


def _prep_odd_gate_weights(w_gates, b_gates):
    nh, dh = MLSTM_HEADS, MLSTM_DH
    w4 = w_gates.reshape(nh, 3, dh, 2 * nh)
    pad = jnp.zeros((nh * dh, V7X_LANES - 2 * nh), F32)
    parts = [jnp.concatenate([w4[:, j].reshape(nh * dh, 2 * nh), pad], axis=1).astype(BF16) for j in range(3)]
    b = jnp.concatenate([b_gates, jnp.zeros((V7X_LANES - 2 * nh,), F32)]).reshape(1, V7X_LANES)
    return parts[0], parts[1], parts[2], b


MLSTM_EXT = MLSTM_DH + V7X_LANES


def _mlstm_body(q_ref, k_ref, kt_ref, v_ref, gcol_ref, grow_ref, tri_ref, trit_ref, o_ref, c_scr, m_scr, *,
                chunk, n_sub):
    nh, dh = MLSTM_HEADS, MLSTM_DH
    lanes = V7X_LANES

    @pl.when(pl.program_id(0) == 0)
    def _():
        c_scr[...] = jnp.zeros(c_scr.shape, F32)
        m_scr[...] = jnp.zeros(m_scr.shape, F32)

    row = lax.broadcasted_iota(jnp.int32, (chunk, chunk), 0)
    col = lax.broadcasted_iota(jnp.int32, (chunk, chunk), 1)
    causal = col <= row
    ones_col = jnp.where(lax.broadcasted_iota(jnp.int32, (chunk, lanes), 1) == 0, 1.0, 0.0).astype(BF16)

    chunks = [slice(c * chunk, (c + 1) * chunk) for c in range(n_sub)]
    heads = [slice(hd * dh, (hd + 1) * dh) for hd in range(nh)]
    pairs = [(c, hd) for c in range(n_sub) for hd in range(nh)]
    b_cols = [_dot(tri_ref[...], jnp.concatenate(_split3(_log_sigmoid(gcol_ref[rows, :])), axis=1))
              for rows in chunks]
    b_cols = [x[:, :lanes] + x[:, lanes:2 * lanes] + x[:, 2 * lanes:] for x in b_cols]
    g_rows = [grow_ref[c] for c in range(n_sub)]
    nr = g_rows[0].shape[0]
    b_rows = [_dot(jnp.concatenate(_split3(_log_sigmoid(gr)), axis=0), trit_ref[...]) for gr in g_rows]
    b_rows = [x[:nr] + x[nr:2 * nr] + x[2 * nr:] for x in b_rows]
    r_h = {(c, hd): g_rows[c][hd:hd + 1, :] - b_rows[c][nh + hd:nh + hd + 1, :] for c, hd in pairs}
    b_h = {(c, hd): b_cols[c][:, nh + hd:nh + hd + 1] for c, hd in pairs}
    logw = {p: jnp.where(causal, r_h[p], NEG_BIG) for p in pairs}
    cm = {p: jnp.max(logw[p], axis=-1, keepdims=True) for p in pairs}
    qk = {(c, hd): _dot_nt(q_ref[chunks[c], heads[hd]], k_ref[chunks[c], heads[hd]]) for c, hd in pairs}
    w_intra = {p: (jnp.exp(logw[p] - cm[p]) * qk[p]).astype(BF16) for p in pairs}
    vext = {(c, hd): jnp.concatenate([v_ref[chunks[c], heads[hd]], ones_col], axis=1) for c, hd in pairs}
    intra = {p: _dot(w_intra[p], vext[p]) for p in pairs}

    for c, hd in pairs:
        p = (c, hd)
        rows, sl = chunks[c], heads[hd]
        m_old = m_scr[hd:hd + 1, 0:1]
        cext = c_scr[hd]
        big_m = jnp.maximum(cm[p], m_old)
        w_inter = jnp.exp(m_old - big_m)
        out = (jnp.exp(cm[p] - big_m) * intra[p]
               + _dot((q_ref[rows, sl].astype(F32) * w_inter).astype(BF16), cext.astype(BF16)))
        den = out[:, dh:dh + 1]
        o_ref[rows, sl] = out[:, :dh] / jnp.maximum(jnp.abs(den), jnp.exp(-(b_h[p] + big_m)))

        m_last = big_m[chunk - 1:chunk, :]
        w_s = jnp.exp(r_h[p] - m_last)
        ktw = (kt_ref[sl, rows].astype(F32) * w_s).astype(BF16)
        c_scr[hd] = jnp.exp(m_old - m_last) * cext + _dot(ktw, vext[p])
        m_scr[hd:hd + 1, :] = jnp.broadcast_to(b_h[p][chunk - 1:chunk, :] + m_last, (1, m_scr.shape[1]))


def _mlstm(q, k, kt, v, gates):
    s = q.shape[0]
    chunk = min(SCAN_CHUNK, s)
    nc = s // chunk
    n_sub = min(MLSTM_CHUNKS_PER_STEP, nc)
    rows = chunk * n_sub
    lanes = V7X_LANES
    ng = 2 * MLSTM_HEADS
    grow = gates[:, :ng].reshape(nc, chunk, ng).transpose(0, 2, 1)
    t = np.arange(chunk)
    tri = (t[None, :] <= t[:, None]).astype(np.float32)
    return pl.pallas_call(
        functools.partial(_mlstm_body, chunk=chunk, n_sub=n_sub),
        grid=(s // rows,),
        in_specs=[_rows(rows, D_MODEL), _rows(rows, D_MODEL), pl.BlockSpec((D_MODEL, rows), lambda c: (0, c)),
                  _rows(rows, D_MODEL), _rows(rows, lanes),
                  pl.BlockSpec((n_sub, ng, chunk), lambda c: (c, 0, 0)),
                  _resident((chunk, chunk)), _resident((chunk, chunk))],
        out_specs=_rows(rows, D_MODEL),
        out_shape=jax.ShapeDtypeStruct((s, D_MODEL), F32),
        scratch_shapes=[pltpu.VMEM((MLSTM_HEADS, MLSTM_DH, MLSTM_EXT), F32),
                        pltpu.VMEM((8, lanes), F32)],
        compiler_params=_cparams(("arbitrary",)),
        name="mlstm_scan",
    )(q, k, kt, v, gates, grow, jnp.asarray(tri, BF16), jnp.asarray(tri.T, BF16))


def _odd_out_body(h_ref, hc_ref, xc_ref, z_ref, gh_ref, skip_ref, wo_ref, gpost_ref,
                  mem_ref, gmem_ref, xpre_ref, wq_ref, wk_ref, wv_ref, xwo_ref, xpost_ref, out_ref, k_scr, v_scr):
    _xattn_memory(mem_ref, gmem_ref, wk_ref, wv_ref, k_scr, v_scr)
    dh = MLSTM_DH
    subs = _subtiles(h_ref.shape[0])
    hs = []
    for rows in subs:
        hc = hc_ref[rows, :]
        parts = []
        for hd in range(MLSTM_HEADS):
            seg = hc[:, hd * dh:(hd + 1) * dh]
            cen = seg - jnp.mean(seg, axis=-1, keepdims=True)
            parts.append(cen * lax.rsqrt(jnp.mean(cen * cen, axis=-1, keepdims=True) + EPS))
        hn = jnp.concatenate(parts, axis=1) * gh_ref[...]
        out = ((hn + skip_ref[...] * xc_ref[rows, :]) * _silu(z_ref[rows, :])).astype(BF16)
        hs.append(h_ref[rows, :] + _rms(_dot(out, wo_ref[...]), gpost_ref[...]))
    for rows, h in zip(subs, hs):
        out_ref[rows, :] = _xattn_rows(h, xpre_ref, wq_ref, xwo_ref, xpost_ref, k_scr, v_scr)


def _odd_out_xattn(h, hcell, x_c, z, g_hnorm, skip, w_out, g_post, xattn_args, layer):
    s = h.shape[0]
    tm = min(OUT_TILE, s)
    x_ins, x_specs, x_scratch = _xattn_operands(xattn_args, layer)
    return pl.pallas_call(
        _odd_out_body,
        grid=(s // tm,),
        in_specs=[_rows(tm, D_MODEL)] * 4 + [_resident(g_hnorm.shape), _resident(skip.shape),
                                              _resident(w_out.shape), _resident(g_post.shape)] + x_specs,
        out_specs=_rows(tm, D_MODEL),
        out_shape=jax.ShapeDtypeStruct((s, D_MODEL), F32),
        scratch_shapes=x_scratch,
        compiler_params=_cparams(("arbitrary",)),
        name="odd_out_xattn",
    )(h, hcell, x_c, z, g_hnorm, skip, w_out, g_post, *x_ins)


def _ffn_body(h_ref, gpre_ref, w1_ref, w2_ref, gpost_ref, out_ref):
    h = h_ref[...]
    xn = _rms(h, gpre_ref[...]).astype(BF16)
    acc = None
    for c in range(D_FF // FF_CHUNK):
        sl = slice(c * FF_CHUNK, (c + 1) * FF_CHUNK)
        a = jnp.square(jnp.maximum(_dot(xn, w1_ref[:, sl].astype(BF16)), 0.0)).astype(BF16)
        part = _dot(a, w2_ref[sl, :].astype(BF16))
        acc = part if acc is None else acc + part
    out_ref[...] = h + _rms(acc, gpost_ref[...])


def _ffn(h, g_pre, w1, w2, g_post, layer):
    s = h.shape[0]
    tm = min(ROW_TILE, s)
    return pl.pallas_call(
        _ffn_body,
        grid=(s // tm,),
        in_specs=[_rows(tm, D_MODEL), _resident(g_pre.shape), _layer_resident(w1.shape, layer),
                  _layer_resident(w2.shape, layer), _resident(g_post.shape)],
        out_specs=_rows(tm, D_MODEL),
        out_shape=jax.ShapeDtypeStruct((s, D_MODEL), F32),
        compiler_params=_cparams(("parallel",)),
        name="sq_relu_mlp",
    )(h, g_pre, w1, w2, g_post)


def _row(g):
    return g.reshape(1, -1)


def kernel(x, mem, positions, g_mix_pre, g_mix_post, g_xattn_pre, g_xattn_post, g_mem, g_ffn_pre, g_ffn_post, ev_w_in, ev_g_q, ev_w_uq, ev_g_kv, ev_w_ukv, ev_w_gate, ev_b_gate, ev_g_gla, ev_w_out, od_w_in, od_conv_w, od_conv_b, od_w_q, od_w_k, od_w_v, od_w_gates, od_b_gates, od_g_hnorm, od_skip, od_w_out, xa_w_q, xa_w_k, xa_w_v, xa_w_o, ffn_w1, ffn_w2):
    batch, seq, d = x.shape
    assert batch == 1 and d == D_MODEL and seq % ROW_TILE == 0 and seq % (2 * ATTN_TK) == 0
    assert seq % (SCAN_CHUNK * GLA_CHUNKS_PER_STEP) == 0 and seq % (SCAN_CHUNK * MLSTM_CHUNKS_PER_STEP) == 0
    assert ROW_TILE == ATTN_TK and seq % OUT_TILE == 0
    h = x.reshape(seq, d)
    mem2 = mem.reshape(mem.shape[1], d)
    cos_t, sin_t = _rope_tables(positions)
    n_a = MLA_HEADS * MLA_V

    for layer in range(DEPTH):
        j = layer // 2
        xattn_args = (mem2, _row(g_mem[layer]), _row(g_xattn_pre[layer]), xa_w_q, xa_w_k, xa_w_v, xa_w_o,
                      _row(g_xattn_post[layer]))
        if layer % 2 == 0:
            w_in_p, w_uq_p, w_uk_p, w_uvt_p, w_gate_p, b_gate = _prep_even_weights(
                ev_w_in[j], ev_w_uq[j], ev_w_ukv[j], ev_w_gate[j], ev_b_gate[j])
            q, k, vt, gq, gk, gv, log_a, r = _even_proj(
                h, _row(g_mix_pre[layer]), w_in_p, _row(ev_g_q[j]), w_uq_p, _row(ev_g_kv[j]), w_uk_p, w_uvt_p,
                w_gate_p, b_gate, cos_t, sin_t)
            a_t = _mla_attn(q, k, vt)
            o_gla = _gla(gq, gk, gv, log_a)
            w_out = ev_w_out[j].astype(BF16)
            h = _even_out_xattn(h, a_t, o_gla, r, _row(ev_g_gla[j]), w_out[:n_a], w_out[n_a:],
                                _row(g_mix_post[layer]), xattn_args, layer)
        else:
            wg_q, wg_k, wg_v, b_g = _prep_odd_gate_weights(od_w_gates[j], od_b_gates[j])
            w_k = od_w_k[j].astype(BF16)
            q, k, kt, v, gates, x_c, z = _odd_proj(
                h, _row(g_mix_pre[layer]), od_w_in[j].astype(BF16), od_conv_w[j], _row(od_conv_b[j]),
                od_w_q[j].astype(BF16), w_k, w_k.transpose(0, 2, 1), od_w_v[j].astype(BF16), wg_q, wg_k, wg_v, b_g)
            hcell = _mlstm(q, k, kt, v, gates)
            h = _odd_out_xattn(h, hcell, x_c, z, _row(od_g_hnorm[j]), _row(od_skip[j]), od_w_out[j].astype(BF16),
                               _row(g_mix_post[layer]), xattn_args, layer)
        h = _ffn(h, _row(g_ffn_pre[layer]), ffn_w1, ffn_w2, _row(g_ffn_post[layer]), layer)
    return h.reshape(batch, seq, d)
```

```python
import functools

import numpy as np
import jax
import jax.numpy as jnp
from jax import lax
from jax.experimental import pallas as pl
from jax.experimental.pallas import tpu as pltpu

F32 = jnp.float32
BF16 = jnp.bfloat16

D_MODEL = 1024
DEPTH = 4
EPS = 1e-6
MLA_HEADS = 8
MLA_NOPE = 64
MLA_ROPE = 32
MLA_V = 64
MLA_Q_RANK = 256
MLA_KV_RANK = 128
ROPE_THETA = 10000.0
GLA_HEADS = 4
GLA_DK = 64
GLA_DV = 128
GLA_GATE_RANK = 16
GLA_TAU = 16.0
MLSTM_HEADS = 4
MLSTM_DH = 256
MLSTM_CONV = 4
XATTN_HEADS = 4
XATTN_DH = 256
D_FF = 4096
EVEN_SPLITS = (256, 128, 32, 256, 256, 512, 16, 512)

V7X_LANES = 128
V7X_VMEM_BYTES = 64 * 1024 * 1024
VMEM_LIMIT = (V7X_VMEM_BYTES * 7) // 8

ROW_TILE = 512
OUT_TILE = 512
OUT_SUBTILE = 256
FF_CHUNK = 1024
ATTN_TK = 512
ATTN_HEADS_PER_STEP = 2
SCAN_CHUNK = 128
GLA_CHUNKS_PER_STEP = 4
MLSTM_CHUNKS_PER_STEP = 4
HEAD_PAD = V7X_LANES
ATTN_VROWS = MLA_V + 16

LOG2E = 1.4426950408889634
NEG_BIG = -1e30


def _cparams(sem):
    return pltpu.CompilerParams(dimension_semantics=sem, vmem_limit_bytes=VMEM_LIMIT)


def _resident(shape):
    nd = len(shape)
    return pl.BlockSpec(shape, lambda *_: (0,) * nd, pipeline_mode=pl.Buffered(1))


def _rows(tile, width):
    return pl.BlockSpec((tile, width), lambda i: (i, 0))


def _rms(x, g):
    return x * lax.rsqrt(jnp.mean(x * x, axis=-1, keepdims=True) + EPS) * g


def _silu(x):
    return x * (1.0 / (1.0 + jnp.exp(-x)))


def _log_sigmoid(x):
    return jnp.minimum(x, 0.0) - jnp.log(1.0 + jnp.exp(-jnp.abs(x)))


def _dot(a, b):
    return jnp.dot(a, b, preferred_element_type=F32)


def _dot_nt(a, b):
    return lax.dot_general(a, b, (((1,), (1,)), ((), ())), preferred_element_type=F32)


def _dot_tn(a, b):
    return lax.dot_general(a, b, (((0,), (0,)), ((), ())), preferred_element_type=F32)


def _split2(x):
    x1 = x.astype(BF16)
    return x1, (x - x1.astype(F32)).astype(BF16)


def _split3(x):
    x1 = x.astype(BF16)
    r = x - x1.astype(F32)
    x2 = r.astype(BF16)
    x3 = (r - x2.astype(F32)).astype(BF16)
    return x1, x2, x3


ROPE_PACK = V7X_LANES // (MLA_ROPE // 2)


def _rope_body(pos_ref, freq_ref, cos_ref, sin_ref):
    ang = pos_ref[...] * freq_ref[...]
    cos_ref[...] = jnp.cos(ang)
    sin_ref[...] = jnp.sin(ang)


def _rope_tables(positions):
    s = positions.shape[-1]
    nf = MLA_ROPE // 2
    rows = s // ROPE_PACK
    pos = jnp.repeat(positions.astype(F32).reshape(rows, ROPE_PACK), nf, axis=1)
    inv_freq = ROPE_THETA ** (-jnp.arange(0, MLA_ROPE, 2, dtype=F32) / MLA_ROPE)
    freq = jnp.tile(inv_freq, ROPE_PACK).reshape(1, V7X_LANES)
    tile = min(512, rows)
    cos_p, sin_p = pl.pallas_call(
        _rope_body,
        grid=(rows // tile,),
        in_specs=[_rows(tile, V7X_LANES), _resident((1, V7X_LANES))],
        out_specs=[_rows(tile, V7X_LANES), _rows(tile, V7X_LANES)],
        out_shape=[jax.ShapeDtypeStruct((rows, V7X_LANES), F32)] * 2,
        compiler_params=_cparams(("parallel",)),
        name="rope_tables",
    )(pos, freq)

    def place(t):
        t = t.reshape(s, nf)
        return jnp.concatenate([jnp.zeros((s, MLA_NOPE), F32), t, t,
                                jnp.zeros((s, HEAD_PAD - MLA_NOPE - MLA_ROPE), F32)], axis=1)

    return place(cos_p), place(sin_p)


_EV_OFF = np.cumsum((0, 256, 128, 128, 128, 256, 256, 512, 128, 512))


def _even_proj_body(h_ref, gpre_ref, win_ref, gq_ref, wuq_ref, gkv_ref, wuk_ref, wuvt_ref, wgate_ref, bgate_ref,
                    cos_ref, sin_ref,
                    q_ref, k_ref, vt_ref, gq_out, gk_out, gv_out, la_out, r_out):
    o = _EV_OFF
    hn = _rms(h_ref[...], gpre_ref[...]).astype(BF16)
    proj = _dot(hn, win_ref[...])
    c_q = proj[:, o[0]:o[1]]
    c_kv = proj[:, o[1]:o[2]]
    kpe_a = proj[:, o[2]:o[3]]
    kpe_b = proj[:, o[3]:o[4]]
    cosk = cos_ref[...]
    sink = sin_ref[...]
    nh = MLA_HEADS
    width = nh * HEAD_PAD

    qs = (MLA_NOPE + MLA_ROPE) ** -0.5 * LOG2E
    lane = lax.broadcasted_iota(jnp.int32, cosk.shape, 1)
    cosq = qs * jnp.where(lane < MLA_NOPE, 1.0, cosk)
    sinq = qs * sink
    cqn = _rms(c_q, gq_ref[...]).astype(BF16)
    qab = _dot(cqn, wuq_ref[...])
    q = qab[:, :width] * jnp.tile(cosq, (1, nh)) + qab[:, width:] * jnp.tile(sinq, (1, nh))
    q_ref[...] = q.astype(BF16)

    ckvn = _rms(c_kv, gkv_ref[...]).astype(BF16)
    kpe = kpe_a * cosk + kpe_b * sink
    k_ref[...] = (_dot(ckvn, wuk_ref[...]) + jnp.tile(kpe, (1, nh))).astype(BF16)
    vt = _dot_nt(wuvt_ref[...], ckvn).astype(BF16)
    row = lax.broadcasted_iota(jnp.int32, (ATTN_VROWS - MLA_V, vt.shape[1]), 0)
    extra = jnp.where(row == 0, 1.0, 0.0).astype(BF16)
    for hd in range(nh):
        vt_ref[hd, 0:MLA_V, :] = vt[hd * MLA_V:(hd + 1) * MLA_V]
        vt_ref[hd, MLA_V:ATTN_VROWS, :] = extra

    gq_out[...] = proj[:, o[4]:o[5]]
    gk_out[...] = proj[:, o[5]:o[6]]
    gv_out[...] = proj[:, o[6]:o[7]]
    glr = proj[:, o[7]:o[8]].astype(BF16)
    x = _dot(glr, wgate_ref[...]) + bgate_ref[...]
    la_out[...] = _log_sigmoid(x) * (1.0 / GLA_TAU)
    r_out[...] = proj[:, o[8]:o[9]]


def _even_proj(h, g_pre, w_in, g_q, w_uq, g_kv, w_uk, w_uvt, w_gate, b_gate, cos_t, sin_t):
    s = h.shape[0]
    tm = min(ROW_TILE, s)
    width = MLA_HEADS * HEAD_PAD
    gdk = GLA_HEADS * GLA_DK
    gdv = GLA_HEADS * GLA_DV
    rows = [(width, BF16), (width, BF16), None, (gdk, F32), (gdk, F32), (gdv, F32), (gdk, F32), (gdv, F32)]
    vt_spec = pl.BlockSpec((MLA_HEADS, None, ATTN_VROWS, tm), lambda i: (0, i, 0, 0))
    vt_shape = jax.ShapeDtypeStruct((MLA_HEADS, s // tm, ATTN_VROWS, tm), BF16)
    return pl.pallas_call(
        _even_proj_body,
        grid=(s // tm,),
        in_specs=[_rows(tm, D_MODEL), _resident(g_pre.shape), _resident(w_in.shape), _resident(g_q.shape),
                  _resident(w_uq.shape), _resident(g_kv.shape), _resident(w_uk.shape), _resident(w_uvt.shape),
                  _resident(w_gate.shape), _resident(b_gate.shape), _rows(tm, HEAD_PAD), _rows(tm, HEAD_PAD)],
        out_specs=[vt_spec if r is None else _rows(tm, r[0]) for r in rows],
        out_shape=[vt_shape if r is None else jax.ShapeDtypeStruct((s, r[0]), r[1]) for r in rows],
        compiler_params=_cparams(("parallel",)),
        name="even_proj",
    )(h, g_pre, w_in, g_q, w_uq, g_kv, w_uk, w_uvt, w_gate, b_gate, cos_t, sin_t)


def _prep_even_weights(w_in, w_uq, w_ukv, w_gate, b_gate):
    d = w_in.shape[0]
    off = np.cumsum((0,) + EVEN_SPLITS)
    seg = [w_in[:, off[i]:off[i + 1]] for i in range(len(EVEN_SPLITS))]
    c_q, c_kv, k_pe, gq, gk, gv, glr, r = seg
    half = MLA_ROPE // 2
    z = lambda n: jnp.zeros((d, n), F32)
    pad = HEAD_PAD - MLA_NOPE - MLA_ROPE
    kpe_a = jnp.concatenate([z(MLA_NOPE), k_pe, z(pad)], axis=1)
    kpe_b = jnp.concatenate([z(MLA_NOPE), -k_pe[:, half:], k_pe[:, :half], z(pad)], axis=1)
    glr_p = jnp.concatenate([glr, z(HEAD_PAD - GLA_GATE_RANK)], axis=1)
    w_in_p = jnp.concatenate([c_q, c_kv, kpe_a, kpe_b, gq, gk, gv, glr_p, r], axis=1).astype(BF16)

    nh = MLA_HEADS
    wq3 = w_uq.reshape(MLA_Q_RANK, nh, MLA_NOPE + MLA_ROPE)
    nope, rope = wq3[..., :MLA_NOPE], wq3[..., MLA_NOPE:]
    zq = lambda n: jnp.zeros((MLA_Q_RANK, nh, n), F32)
    wa = jnp.concatenate([nope, rope, zq(pad)], axis=-1).reshape(MLA_Q_RANK, nh * HEAD_PAD)
    wb = jnp.concatenate([zq(MLA_NOPE), -rope[..., half:], rope[..., :half], zq(pad)],
                         axis=-1).reshape(MLA_Q_RANK, nh * HEAD_PAD)
    w_uq_p = jnp.concatenate([wa, wb], axis=1).astype(BF16)

    wkv3 = w_ukv.reshape(MLA_KV_RANK, nh, MLA_NOPE + MLA_V)
    zk = jnp.zeros((MLA_KV_RANK, nh, HEAD_PAD - MLA_NOPE), F32)
    w_uk_p = jnp.concatenate([wkv3[..., :MLA_NOPE], zk], axis=-1).reshape(MLA_KV_RANK, nh * HEAD_PAD).astype(BF16)
    w_uvt_p = wkv3[..., MLA_NOPE:].reshape(MLA_KV_RANK, nh * MLA_V).T.astype(BF16)

    w_gate_p = jnp.concatenate([w_gate, jnp.zeros((HEAD_PAD - GLA_GATE_RANK, w_gate.shape[1]), F32)],
                               axis=0).astype(BF16)
    return w_in_p, w_uq_p, w_uk_p, w_uvt_p, w_gate_p, b_gate.reshape(1, -1)


def _mla_attn_body(q_ref, k_ref, vt_ref, o_ref, acc_scr, st_scr, p_scr, m_scr, alpha_scr, *, tq, tk, hg):
    i = pl.program_id(1)
    heads = range(hg)

    def scores(t, slot):
        start = pl.multiple_of(t * tk, tk)
        for hd in heads:
            lanes = slice(hd * HEAD_PAD, (hd + 1) * HEAD_PAD)
            st_scr[slot, hd] = _dot_nt(k_ref[pl.ds(start, tk), lanes], q_ref[:, lanes])

    def accumulate(t, slot):
        for hd in heads:
            acc_scr[hd] = alpha_scr[slot, hd] * acc_scr[hd] + _dot(vt_ref[hd, t], p_scr[slot, hd])

    def softmax(slot, mask_shift):
        for hd in heads:
            st = st_scr[slot, hd]
            if mask_shift is not None:
                key = lax.broadcasted_iota(jnp.int32, st.shape, 0) + mask_shift
                qry = lax.broadcasted_iota(jnp.int32, st.shape, 1)
                st = jnp.where(key <= qry, st, NEG_BIG)
            m_prev = m_scr[hd]
            m_next = jnp.maximum(m_prev, jnp.max(st, axis=0, keepdims=True))
            p_scr[slot, hd] = jnp.exp2(st - m_next).astype(BF16)
            alpha_scr[slot, hd] = jnp.exp2(m_prev - m_next)
            m_scr[hd] = m_next

    def step(t, slot, mask_shift, more):
        softmax(slot, mask_shift)
        if more:
            scores(t + 1, 1 - slot)
        accumulate(jnp.maximum(t - 1, 0), 1 - slot)

    acc_scr[...] = jnp.zeros(acc_scr.shape, F32)
    p_scr[1] = jnp.zeros(p_scr.shape[1:], BF16)
    alpha_scr[1] = jnp.ones(alpha_scr.shape[1:], F32)
    m_scr[...] = jnp.full(m_scr.shape, NEG_BIG, F32)
    scores(0, 0)

    def pair(u, c):
        step(2 * u, 0, None, True)
        step(2 * u + 1, 1, None, True)
        return c

    lax.fori_loop(0, i, pair, 0)
    step(2 * i, 0, 0, True)
    step(2 * i + 1, 1, tk, False)
    accumulate(2 * i + 1, 1)
    for hd in heads:
        acc = acc_scr[hd]
        o_ref[hd] = (acc[:MLA_V] * (1.0 / acc[MLA_V:MLA_V + 1])).astype(o_ref.dtype)


def _mla_attn(q, k, vt):
    s = q.shape[0]
    nk, tk = vt.shape[1], vt.shape[3]
    tq = 2 * tk
    hg = ATTN_HEADS_PER_STEP
    out = pl.pallas_call(
        functools.partial(_mla_attn_body, tq=tq, tk=tk, hg=hg),
        grid=(MLA_HEADS // hg, s // tq),
        in_specs=[pl.BlockSpec((tq, hg * HEAD_PAD), lambda g, i: (i, g)),
                  pl.BlockSpec((s, hg * HEAD_PAD), lambda g, i: (0, g), pipeline_mode=pl.Buffered(1)),
                  pl.BlockSpec((hg, nk, ATTN_VROWS, tk), lambda g, i: (g, 0, 0, 0), pipeline_mode=pl.Buffered(1))],
        out_specs=pl.BlockSpec((hg, MLA_V, tq), lambda g, i: (g, 0, i)),
        out_shape=jax.ShapeDtypeStruct((MLA_HEADS, MLA_V, s), BF16),
        scratch_shapes=[pltpu.VMEM((hg, ATTN_VROWS, tq), F32),
                        pltpu.VMEM((2, hg, tk, tq), F32),
                        pltpu.VMEM((2, hg, tk, tq), BF16),
                        pltpu.VMEM((hg, 1, tq), F32),
                        pltpu.VMEM((2, hg, 1, tq), F32)],
        compiler_params=_cparams(("parallel", "arbitrary")),
        name="mla_attn",
    )(q, k, vt)
    return out.reshape(MLA_HEADS * MLA_V, s)


def _gla_levels(chunk):
    n = 0
    while (1 << n) < chunk:
        n += 1
    return n


def _gla_constants(chunk):
    t = np.arange(chunk)
    mats = [t[None, :] <= t[:, None]]
    masks = []
    b = 1
    while b < chunk:
        blk = t // (2 * b)
        upper = (t % (2 * b)) >= b
        e = blk * 2 * b + b - 1
        up_rows = upper[:, None] & (t[None, :] > e[:, None]) & (t[None, :] <= t[:, None])
        lo_rows = (~upper)[:, None] & (t[None, :] > t[:, None]) & (t[None, :] <= e[:, None])
        mats.append(up_rows | lo_rows)
        masks.append(upper[:, None] & (~upper)[None, :] & (blk[:, None] == blk[None, :]))
        b *= 2
    masks.append(t[:, None] == t[None, :])
    mats = np.concatenate(mats, axis=0).astype(np.float32)
    masks = np.stack(masks).astype(np.float32)
    masks = np.tile(masks, (1, GLA_HEADS, 1))
    return jnp.asarray(mats, BF16), jnp.asarray(masks, F32)


def _gla_body(q_ref, k_ref, v_ref, g_ref, mats_ref, masks_ref, o_ref, st_ref, *, chunk, n_sub):
    nlev = _gla_levels(chunk)
    nh, dk, dv = GLA_HEADS, GLA_DK, GLA_DV
    w = nh * dk

    @pl.when(pl.program_id(0) == 0)
    def _():
        st_ref[...] = jnp.zeros(st_ref.shape, F32)

    head_of_lane = lax.broadcasted_iota(jnp.int32, (chunk, w), 1) // dk

    def stack_heads(x):
        return jnp.concatenate([jnp.where(head_of_lane == hd, x, 0.0) for hd in range(nh)], axis=0).astype(BF16)

    chunks = [slice(c * chunk, (c + 1) * chunk) for c in range(n_sub)]
    qs = [q_ref[rows, :] * (dk ** -0.5) for rows in chunks]
    ks = [k_ref[rows, :] for rows in chunks]
    e_alls = [_dot(mats_ref[...], jnp.concatenate(_split2(g_ref[rows, :]), axis=1)) for rows in chunks]
    e_alls = [x[:, :w] + x[:, w:] for x in e_alls]
    attn = [_dot_nt(stack_heads(q), k.astype(BF16)) * masks_ref[nlev] for q, k in zip(qs, ks)]
    for lv in range(nlev):
        decs = [jnp.exp(x[(1 + lv) * chunk:(2 + lv) * chunk]) for x in e_alls]
        attn = [a + _dot_nt(stack_heads(q * d), (k * d).astype(BF16)) * masks_ref[lv]
                for a, q, k, d in zip(attn, qs, ks, decs)]
    attn = [a.astype(BF16) for a in attn]
    for c, rows in enumerate(chunks):
        for hd in range(nh):
            vh = v_ref[rows, hd * dv:(hd + 1) * dv].astype(BF16)
            o_ref[rows, hd * dv:(hd + 1) * dv] = _dot(attn[c][hd * chunk:(hd + 1) * chunk], vh)
    pending = []
    for q, k, x in zip(qs, ks, e_alls):
        b = x[:chunk]
        b_end = b[chunk - 1:chunk]
        pending.append((stack_heads(q * jnp.exp(b)), (k * jnp.exp(b_end - b)).astype(BF16), jnp.exp(b_end)))

    st = st_ref[...]
    head_of_state_lane = lax.broadcasted_iota(jnp.int32, st.shape, 1) // dk
    for c in range(n_sub):
        rows = slice(c * chunk, (c + 1) * chunk)
        qg, kg, dec_end = pending[c]
        inter = _dot_nt(qg, st.astype(BF16))
        new = st * dec_end
        for hd in range(nh):
            cols = slice(hd * dv, (hd + 1) * dv)
            o_ref[rows, cols] = o_ref[rows, cols] + inter[hd * chunk:(hd + 1) * chunk]
            upd = _dot_tn(v_ref[rows, cols].astype(BF16), kg)
            new = new + jnp.where(head_of_state_lane == hd, upd, 0.0)
        st = new
    st_ref[...] = st


def _gla(gq, gk, gv, log_a):
    s = gq.shape[0]
    chunk = min(SCAN_CHUNK, s)
    n_sub = min(GLA_CHUNKS_PER_STEP, s // chunk)
    rows = chunk * n_sub
    mats, masks = _gla_constants(chunk)
    gdk = GLA_HEADS * GLA_DK
    gdv = GLA_HEADS * GLA_DV
    return pl.pallas_call(
        functools.partial(_gla_body, chunk=chunk, n_sub=n_sub),
        grid=(s // rows,),
        in_specs=[_rows(rows, gdk), _rows(rows, gdk), _rows(rows, gdv), _rows(rows, gdk),
                  _resident(mats.shape), _resident(masks.shape)],
        out_specs=_rows(rows, gdv),
        out_shape=jax.ShapeDtypeStruct((s, gdv), F32),
        scratch_shapes=[pltpu.VMEM((GLA_DV, gdk), F32)],
        compiler_params=_cparams(("arbitrary",)),
        name="gla_scan",
    )(gq, gk, gv, log_a, mats, masks)


def _xattn_memory(mem_ref, gmem_ref, wk_ref, wv_ref, k_scr, v_scr):
    @pl.when(pl.program_id(0) == 0)
    def _():
        mem_n = _rms(mem_ref[...], gmem_ref[...]).astype(BF16)
        k_scr[...] = _dot(mem_n, wk_ref[...].astype(BF16)).astype(BF16)
        v_scr[...] = _dot(mem_n, wv_ref[...].astype(BF16)).astype(BF16)


def _xattn_rows(h, gpre_ref, wq_ref, wo_ref, gpost_ref, k_scr, v_scr):
    hn = _rms(h, gpre_ref[...]).astype(BF16)
    q = _dot(hn, wq_ref[...].astype(BF16)).astype(BF16)
    dh = XATTN_DH
    scale = dh ** -0.5
    heads = [slice(hd * dh, (hd + 1) * dh) for hd in range(XATTN_HEADS)]
    scores = [_dot_nt(q[:, sl], k_scr[:, sl]) * scale for sl in heads]
    probs = [jnp.exp(s - jnp.max(s, axis=-1, keepdims=True)) for s in scores]
    inv_l = [1.0 / jnp.sum(p, axis=-1, keepdims=True) for p in probs]
    outs = [(_dot(p.astype(BF16), v_scr[:, sl]) * il).astype(BF16) for p, sl, il in zip(probs, heads, inv_l)]
    o = jnp.concatenate(outs, axis=1)
    return h + _rms(_dot(o, wo_ref[...].astype(BF16)), gpost_ref[...])


def _layer_resident(shape, layer):
    rest = tuple(shape[1:])
    return pl.BlockSpec((None,) + rest, lambda *_: (layer,) + (0,) * len(rest), pipeline_mode=pl.Buffered(1))


def _subtiles(rows):
    sub = min(OUT_SUBTILE, rows)
    return [slice(i * sub, (i + 1) * sub) for i in range(rows // sub)]


def _xattn_operands(xattn_args, layer):
    mem, g_mem, g_pre, w_q, w_k, w_v, w_o, g_post = xattn_args
    ins = (mem, g_mem, g_pre, w_q, w_k, w_v, w_o, g_post)
    specs = ([_resident(a.shape) for a in (mem, g_mem, g_pre)]
             + [_layer_resident(w.shape, layer) for w in (w_q, w_k, w_v, w_o)] + [_resident(g_post.shape)])
    scratch = [pltpu.VMEM((mem.shape[0], D_MODEL), BF16), pltpu.VMEM((mem.shape[0], D_MODEL), BF16)]
    return ins, specs, scratch


def _even_out_body(h_ref, at_ref, o_ref, r_ref, ggla_ref, wout_ref, gpost_ref,
                   mem_ref, gmem_ref, xpre_ref, wq_ref, wk_ref, wv_ref, wo_ref, xpost_ref, out_ref, k_scr, v_scr):
    _xattn_memory(mem_ref, gmem_ref, wk_ref, wv_ref, k_scr, v_scr)
    subs = _subtiles(h_ref.shape[0])
    n_a = at_ref.shape[0]
    hs = []
    for rows in subs:
        og = o_ref[rows, :]
        gn = jnp.concatenate([_rms(og[:, hd * GLA_DV:(hd + 1) * GLA_DV], ggla_ref[...])
                              for hd in range(GLA_HEADS)], axis=1)
        g = (gn * _silu(r_ref[rows, :])).astype(BF16)
        mix = (_dot_tn(at_ref[:, rows], wout_ref[:n_a, :].astype(BF16))
               + _dot(g, wout_ref[n_a:, :].astype(BF16)))
        hs.append(h_ref[rows, :] + _rms(mix, gpost_ref[...]))
    for rows, h in zip(subs, hs):
        out_ref[rows, :] = _xattn_rows(h, xpre_ref, wq_ref, wo_ref, xpost_ref, k_scr, v_scr)


def _even_out_xattn(h, a_t, o_gla, r, g_gla, w_out, g_post, xattn_args, layer):
    s = h.shape[0]
    tm = min(OUT_TILE, s)
    x_ins, x_specs, x_scratch = _xattn_operands(xattn_args, layer)
    return pl.pallas_call(
        _even_out_body,
        grid=(s // tm,),
        in_specs=[_rows(tm, D_MODEL), pl.BlockSpec((a_t.shape[0], tm), lambda i: (0, i)),
                  _rows(tm, o_gla.shape[1]), _rows(tm, r.shape[1]),
                  _resident(g_gla.shape), _layer_resident(w_out.shape, layer // 2), _resident(g_post.shape)] + x_specs,
        out_specs=_rows(tm, D_MODEL),
        out_shape=jax.ShapeDtypeStruct((s, D_MODEL), F32),
        scratch_shapes=x_scratch,
        compiler_params=_cparams(("arbitrary",)),
        name="even_out_xattn",
    )(h, a_t, o_gla, r, g_gla, w_out, g_post, *x_ins)


CONV_HALO = 8


def _odd_proj_body(h_ref, gpre_ref, win_ref, cw_ref, cb_ref, wq_ref, wk_ref, wkt_ref, wv_ref, wgq_ref, wgk_ref,
                   wgv_ref, bg_ref, q_out, k_out, kt_out, v_out, gates_out, xc_out, z_out, xbuf):
    tm = h_ref.shape[0]
    width = D_MODEL

    @pl.when(pl.program_id(0) == 0)
    def _():
        xbuf[0:CONV_HALO, :] = jnp.zeros((CONV_HALO, width), F32)

    hn = _rms(h_ref[...], gpre_ref[...]).astype(BF16)
    xz = _dot(hn, win_ref[...].astype(BF16))
    x_m = xz[:, :width]
    z_out[...] = xz[:, width:]
    xbuf[CONV_HALO:CONV_HALO + tm, :] = x_m
    conv = cb_ref[...] + cw_ref[MLSTM_CONV - 1:MLSTM_CONV, :] * x_m
    for j in range(MLSTM_CONV - 1):
        back = MLSTM_CONV - 1 - j
        conv = conv + cw_ref[j:j + 1, :] * xbuf[CONV_HALO - back:CONV_HALO - back + tm, :]
    xbuf[0:CONV_HALO, :] = x_m[tm - CONV_HALO:, :]
    x_c = _silu(conv)
    xc_out[...] = x_c
    xcb = x_c.astype(BF16)

    dh = MLSTM_DH
    xmb = x_m.astype(BF16)
    qs, ks, vs = [], [], []
    for hd in range(MLSTM_HEADS):
        sl = slice(hd * dh, (hd + 1) * dh)
        qs.append(_dot(xcb[:, sl], wq_ref[hd]))
        ks.append(_dot(xcb[:, sl], wk_ref[hd]))
        vs.append(_dot(xmb[:, sl], wv_ref[hd]))
        kt_out[sl, :] = _dot_nt(wkt_ref[hd], xcb[:, sl]).astype(BF16)
    q = jnp.concatenate(qs, axis=1).astype(BF16)
    k = jnp.concatenate(ks, axis=1).astype(BF16)
    v = jnp.concatenate(vs, axis=1).astype(BF16)
    gates_out[...] = _dot(q, wgq_ref[...]) + _dot(k, wgk_ref[...]) + _dot(v, wgv_ref[...]) + bg_ref[...]
    q_out[...] = (q.astype(F32) * (dh ** -0.5)).astype(BF16)
    k_out[...] = k
    v_out[...] = v


def _odd_proj(h, g_pre, w_in, conv_w, conv_b, w_q, w_k, w_kt, w_v, wg_q, wg_k, wg_v, b_g, j):
    s = h.shape[0]
    tm = min(ROW_TILE, s)
    rows = [(D_MODEL, BF16), (D_MODEL, BF16), None, (D_MODEL, BF16), (V7X_LANES, F32), (D_MODEL, F32),
            (D_MODEL, F32)]
    ins = (g_pre, w_in, conv_w, conv_b, w_q, w_k, w_kt, w_v, wg_q, wg_k, wg_v, b_g)
    kt_spec = pl.BlockSpec((D_MODEL, tm), lambda i: (0, i))
    kt_shape = jax.ShapeDtypeStruct((D_MODEL, s), BF16)
    return pl.pallas_call(
        _odd_proj_body,
        grid=(s // tm,),
        in_specs=([_rows(tm, D_MODEL), _resident(g_pre.shape), _layer_resident(w_in.shape, j)]
                  + [_resident(a.shape) for a in ins[2:]]),
        out_specs=[kt_spec if r is None else _rows(tm, r[0]) for r in rows],
        out_shape=[kt_shape if r is None else jax.ShapeDtypeStruct((s, r[0]), r[1]) for r in rows],
        scratch_shapes=[pltpu.VMEM((CONV_HALO + tm, D_MODEL), F32)],
        compiler_params=_cparams(("arbitrary",)),
        name="odd_proj",
    )(h, *ins)


def _prep_odd_gate_weights(w_gates, b_gates):
    nh, dh = MLSTM_HEADS, MLSTM_DH
    w4 = w_gates.reshape(nh, 3, dh, 2 * nh)
    pad = jnp.zeros((nh * dh, V7X_LANES - 2 * nh), F32)
    parts = [jnp.concatenate([w4[:, j].reshape(nh * dh, 2 * nh), pad], axis=1).astype(BF16) for j in range(3)]
    b = jnp.concatenate([b_gates, jnp.zeros((V7X_LANES - 2 * nh,), F32)]).reshape(1, V7X_LANES)
    return parts[0], parts[1], parts[2], b


MLSTM_EXT = MLSTM_DH + V7X_LANES


def _mlstm_body(q_ref, k_ref, kt_ref, v_ref, gcol_ref, grow_ref, tri_ref, trit_ref, o_ref, c_scr, m_scr, *,
                chunk, n_sub):
    nh, dh = MLSTM_HEADS, MLSTM_DH
    lanes = V7X_LANES

    @pl.when(pl.program_id(0) == 0)
    def _():
        c_scr[...] = jnp.zeros(c_scr.shape, F32)
        m_scr[...] = jnp.zeros(m_scr.shape, F32)

    row = lax.broadcasted_iota(jnp.int32, (chunk, chunk), 0)
    col = lax.broadcasted_iota(jnp.int32, (chunk, chunk), 1)
    causal = col <= row
    ones_col = jnp.where(lax.broadcasted_iota(jnp.int32, (chunk, lanes), 1) == 0, 1.0, 0.0).astype(BF16)

    chunks = [slice(c * chunk, (c + 1) * chunk) for c in range(n_sub)]
    heads = [slice(hd * dh, (hd + 1) * dh) for hd in range(nh)]
    pairs = [(c, hd) for c in range(n_sub) for hd in range(nh)]
    b_cols = [_dot(tri_ref[...], jnp.concatenate(_split3(_log_sigmoid(gcol_ref[rows, :])), axis=1))
              for rows in chunks]
    b_cols = [x[:, :lanes] + x[:, lanes:2 * lanes] + x[:, 2 * lanes:] for x in b_cols]
    g_rows = [grow_ref[c] for c in range(n_sub)]
    nr = g_rows[0].shape[0]
    b_rows = [_dot(jnp.concatenate(_split3(_log_sigmoid(gr)), axis=0), trit_ref[...]) for gr in g_rows]
    b_rows = [x[:nr] + x[nr:2 * nr] + x[2 * nr:] for x in b_rows]
    r_h = {(c, hd): g_rows[c][hd:hd + 1, :] - b_rows[c][nh + hd:nh + hd + 1, :] for c, hd in pairs}
    b_h = {(c, hd): b_cols[c][:, nh + hd:nh + hd + 1] for c, hd in pairs}
    logw = {p: jnp.where(causal, r_h[p], NEG_BIG) for p in pairs}
    cm = {p: jnp.max(logw[p], axis=-1, keepdims=True) for p in pairs}
    qk = {(c, hd): _dot_nt(q_ref[chunks[c], heads[hd]], k_ref[chunks[c], heads[hd]]) for c, hd in pairs}
    w_intra = {p: (jnp.exp(logw[p] - cm[p]) * qk[p]).astype(BF16) for p in pairs}
    vext = {(c, hd): jnp.concatenate([v_ref[chunks[c], heads[hd]], ones_col], axis=1) for c, hd in pairs}
    intra = {p: _dot(w_intra[p], vext[p]) for p in pairs}

    for c, hd in pairs:
        p = (c, hd)
        rows, sl = chunks[c], heads[hd]
        m_old = m_scr[hd:hd + 1, 0:1]
        cext = c_scr[hd]
        big_m = jnp.maximum(cm[p], m_old)
        w_inter = jnp.exp(m_old - big_m)
        out = (jnp.exp(cm[p] - big_m) * intra[p]
               + _dot((q_ref[rows, sl].astype(F32) * w_inter).astype(BF16), cext.astype(BF16)))
        den = out[:, dh:dh + 1]
        o_ref[rows, sl] = out[:, :dh] / jnp.maximum(jnp.abs(den), jnp.exp(-(b_h[p] + big_m)))

        m_last = big_m[chunk - 1:chunk, :]
        w_s = jnp.exp(r_h[p] - m_last)
        ktw = (kt_ref[sl, rows].astype(F32) * w_s).astype(BF16)
        c_scr[hd] = jnp.exp(m_old - m_last) * cext + _dot(ktw, vext[p])
        m_scr[hd:hd + 1, :] = jnp.broadcast_to(b_h[p][chunk - 1:chunk, :] + m_last, (1, m_scr.shape[1]))


def _mlstm(q, k, kt, v, gates):
    s = q.shape[0]
    chunk = min(SCAN_CHUNK, s)
    nc = s // chunk
    n_sub = min(MLSTM_CHUNKS_PER_STEP, nc)
    rows = chunk * n_sub
    lanes = V7X_LANES
    ng = 2 * MLSTM_HEADS
    grow = gates[:, :ng].reshape(nc, chunk, ng).transpose(0, 2, 1)
    t = np.arange(chunk)
    tri = (t[None, :] <= t[:, None]).astype(np.float32)
    return pl.pallas_call(
        functools.partial(_mlstm_body, chunk=chunk, n_sub=n_sub),
        grid=(s // rows,),
        in_specs=[_rows(rows, D_MODEL), _rows(rows, D_MODEL), pl.BlockSpec((D_MODEL, rows), lambda c: (0, c)),
                  _rows(rows, D_MODEL), _rows(rows, lanes),
                  pl.BlockSpec((n_sub, ng, chunk), lambda c: (c, 0, 0)),
                  _resident((chunk, chunk)), _resident((chunk, chunk))],
        out_specs=_rows(rows, D_MODEL),
        out_shape=jax.ShapeDtypeStruct((s, D_MODEL), F32),
        scratch_shapes=[pltpu.VMEM((MLSTM_HEADS, MLSTM_DH, MLSTM_EXT), F32),
                        pltpu.VMEM((8, lanes), F32)],
        compiler_params=_cparams(("arbitrary",)),
        name="mlstm_scan",
    )(q, k, kt, v, gates, grow, jnp.asarray(tri, BF16), jnp.asarray(tri.T, BF16))


def _odd_out_body(h_ref, hc_ref, xc_ref, z_ref, gh_ref, skip_ref, wo_ref, gpost_ref,
                  mem_ref, gmem_ref, xpre_ref, wq_ref, wk_ref, wv_ref, xwo_ref, xpost_ref, out_ref, k_scr, v_scr):
    _xattn_memory(mem_ref, gmem_ref, wk_ref, wv_ref, k_scr, v_scr)
    dh = MLSTM_DH
    subs = _subtiles(h_ref.shape[0])
    hs = []
    for rows in subs:
        hc = hc_ref[rows, :]
        parts = []
        for hd in range(MLSTM_HEADS):
            seg = hc[:, hd * dh:(hd + 1) * dh]
            cen = seg - jnp.mean(seg, axis=-1, keepdims=True)
            parts.append(cen * lax.rsqrt(jnp.mean(cen * cen, axis=-1, keepdims=True) + EPS))
        hn = jnp.concatenate(parts, axis=1) * gh_ref[...]
        out = ((hn + skip_ref[...] * xc_ref[rows, :]) * _silu(z_ref[rows, :])).astype(BF16)
        hs.append(h_ref[rows, :] + _rms(_dot(out, wo_ref[...].astype(BF16)), gpost_ref[...]))
    for rows, h in zip(subs, hs):
        out_ref[rows, :] = _xattn_rows(h, xpre_ref, wq_ref, xwo_ref, xpost_ref, k_scr, v_scr)


def _odd_out_xattn(h, hcell, x_c, z, g_hnorm, skip, w_out, g_post, xattn_args, layer):
    s = h.shape[0]
    tm = min(OUT_TILE, s)
    x_ins, x_specs, x_scratch = _xattn_operands(xattn_args, layer)
    return pl.pallas_call(
        _odd_out_body,
        grid=(s // tm,),
        in_specs=[_rows(tm, D_MODEL)] * 4 + [_resident(g_hnorm.shape), _resident(skip.shape),
                                              _layer_resident(w_out.shape, layer // 2),
                                              _resident(g_post.shape)] + x_specs,
        out_specs=_rows(tm, D_MODEL),
        out_shape=jax.ShapeDtypeStruct((s, D_MODEL), F32),
        scratch_shapes=x_scratch,
        compiler_params=_cparams(("arbitrary",)),
        name="odd_out_xattn",
    )(h, hcell, x_c, z, g_hnorm, skip, w_out, g_post, *x_ins)


def _ffn_body(h_ref, gpre_ref, w1_ref, w2_ref, gpost_ref, out_ref):
    h = h_ref[...]
    xn = _rms(h, gpre_ref[...]).astype(BF16)
    acc = None
    for c in range(D_FF // FF_CHUNK):
        sl = slice(c * FF_CHUNK, (c + 1) * FF_CHUNK)
        a = jnp.square(jnp.maximum(_dot(xn, w1_ref[:, sl].astype(BF16)), 0.0)).astype(BF16)
        part = _dot(a, w2_ref[sl, :].astype(BF16))
        acc = part if acc is None else acc + part
    out_ref[...] = h + _rms(acc, gpost_ref[...])


def _ffn(h, g_pre, w1, w2, g_post, layer):
    s = h.shape[0]
    tm = min(ROW_TILE, s)
    return pl.pallas_call(
        _ffn_body,
        grid=(s // tm,),
        in_specs=[_rows(tm, D_MODEL), _resident(g_pre.shape), _layer_resident(w1.shape, layer),
                  _layer_resident(w2.shape, layer), _resident(g_post.shape)],
        out_specs=_rows(tm, D_MODEL),
        out_shape=jax.ShapeDtypeStruct((s, D_MODEL), F32),
        compiler_params=_cparams(("parallel",)),
        name="sq_relu_mlp",
    )(h, g_pre, w1, w2, g_post)


def _row(g):
    return g.reshape(1, -1)


def kernel(x, mem, positions, g_mix_pre, g_mix_post, g_xattn_pre, g_xattn_post, g_mem, g_ffn_pre, g_ffn_post, ev_w_in, ev_g_q, ev_w_uq, ev_g_kv, ev_w_ukv, ev_w_gate, ev_b_gate, ev_g_gla, ev_w_out, od_w_in, od_conv_w, od_conv_b, od_w_q, od_w_k, od_w_v, od_w_gates, od_b_gates, od_g_hnorm, od_skip, od_w_out, xa_w_q, xa_w_k, xa_w_v, xa_w_o, ffn_w1, ffn_w2):
    batch, seq, d = x.shape
    assert batch == 1 and d == D_MODEL and seq % ROW_TILE == 0 and seq % (2 * ATTN_TK) == 0
    assert seq % (SCAN_CHUNK * GLA_CHUNKS_PER_STEP) == 0 and seq % (SCAN_CHUNK * MLSTM_CHUNKS_PER_STEP) == 0
    assert ROW_TILE == ATTN_TK and seq % OUT_TILE == 0
    h = x.reshape(seq, d)
    mem2 = mem.reshape(mem.shape[1], d)
    cos_t, sin_t = _rope_tables(positions)

    for layer in range(DEPTH):
        j = layer // 2
        xattn_args = (mem2, _row(g_mem[layer]), _row(g_xattn_pre[layer]), xa_w_q, xa_w_k, xa_w_v, xa_w_o,
                      _row(g_xattn_post[layer]))
        if layer % 2 == 0:
            w_in_p, w_uq_p, w_uk_p, w_uvt_p, w_gate_p, b_gate = _prep_even_weights(
                ev_w_in[j], ev_w_uq[j], ev_w_ukv[j], ev_w_gate[j], ev_b_gate[j])
            q, k, vt, gq, gk, gv, log_a, r = _even_proj(
                h, _row(g_mix_pre[layer]), w_in_p, _row(ev_g_q[j]), w_uq_p, _row(ev_g_kv[j]), w_uk_p, w_uvt_p,
                w_gate_p, b_gate, cos_t, sin_t)
            a_t = _mla_attn(q, k, vt)
            o_gla = _gla(gq, gk, gv, log_a)
            h = _even_out_xattn(h, a_t, o_gla, r, _row(ev_g_gla[j]), ev_w_out, _row(g_mix_post[layer]), xattn_args,
                                layer)
        else:
            wg_q, wg_k, wg_v, b_g = _prep_odd_gate_weights(od_w_gates[j], od_b_gates[j])
            w_k = od_w_k[j].astype(BF16)
            q, k, kt, v, gates, x_c, z = _odd_proj(
                h, _row(g_mix_pre[layer]), od_w_in, od_conv_w[j], _row(od_conv_b[j]), od_w_q[j].astype(BF16), w_k,
                w_k.transpose(0, 2, 1), od_w_v[j].astype(BF16), wg_q, wg_k, wg_v, b_g, j)
            hcell = _mlstm(q, k, kt, v, gates)
            h = _odd_out_xattn(h, hcell, x_c, z, _row(od_g_hnorm[j]), _row(od_skip[j]), od_w_out,
                               _row(g_mix_post[layer]), xattn_args, layer)
        h = _ffn(h, _row(g_ffn_pre[layer]), ffn_w1, ffn_w2, _row(g_ffn_post[layer]), layer)
    return h.reshape(batch, seq, d)
```

```python
import functools

import numpy as np
import jax
import jax.numpy as jnp
from jax import lax
from jax.experimental import pallas as pl
from jax.experimental.pallas import tpu as pltpu

F32 = jnp.float32
BF16 = jnp.bfloat16

D_MODEL = 1024
DEPTH = 4
EPS = 1e-6
MLA_HEADS = 8
MLA_NOPE = 64
MLA_ROPE = 32
MLA_V = 64
MLA_Q_RANK = 256
MLA_KV_RANK = 128
ROPE_THETA = 10000.0
GLA_HEADS = 4
GLA_DK = 64
GLA_DV = 128
GLA_GATE_RANK = 16
GLA_TAU = 16.0
MLSTM_HEADS = 4
MLSTM_DH = 256
MLSTM_CONV = 4
XATTN_HEADS = 4
XATTN_DH = 256
D_FF = 4096
EVEN_SPLITS = (256, 128, 32, 256, 256, 512, 16, 512)

V7X_LANES = 128
V7X_VMEM_BYTES = 64 * 1024 * 1024
VMEM_LIMIT = (V7X_VMEM_BYTES * 7) // 8

ROW_TILE = 512
OUT_TILE = 512
OUT_SUBTILE = 256
FF_CHUNK = 1024
ATTN_TK = 512
ATTN_HEADS_PER_STEP = 2
SCAN_CHUNK = 128
GLA_CHUNKS_PER_STEP = 8
MLSTM_CHUNKS_PER_STEP = 4
HEAD_PAD = V7X_LANES
ATTN_VROWS = MLA_V + 16

LOG2E = 1.4426950408889634
NEG_BIG = -1e30


def _cparams(sem):
    return pltpu.CompilerParams(dimension_semantics=sem, vmem_limit_bytes=VMEM_LIMIT)


def _resident(shape):
    nd = len(shape)
    return pl.BlockSpec(shape, lambda *_: (0,) * nd, pipeline_mode=pl.Buffered(1))


def _rows(tile, width):
    return pl.BlockSpec((tile, width), lambda i: (i, 0))


def _rms(x, g):
    return x * lax.rsqrt(jnp.mean(x * x, axis=-1, keepdims=True) + EPS) * g


def _silu(x):
    return x * (1.0 / (1.0 + jnp.exp(-x)))


def _log_sigmoid(x):
    return jnp.minimum(x, 0.0) - jnp.log(1.0 + jnp.exp(-jnp.abs(x)))


def _dot(a, b):
    return jnp.dot(a, b, preferred_element_type=F32)


def _dot_nt(a, b):
    return lax.dot_general(a, b, (((1,), (1,)), ((), ())), preferred_element_type=F32)


def _dot_tn(a, b):
    return lax.dot_general(a, b, (((0,), (0,)), ((), ())), preferred_element_type=F32)


def _split2(x):
    x1 = x.astype(BF16)
    return x1, (x - x1.astype(F32)).astype(BF16)


def _split3(x):
    x1 = x.astype(BF16)
    r = x - x1.astype(F32)
    x2 = r.astype(BF16)
    x3 = (r - x2.astype(F32)).astype(BF16)
    return x1, x2, x3


ROPE_PACK = V7X_LANES // (MLA_ROPE // 2)


def _rope_body(pos_ref, freq_ref, cos_ref, sin_ref):
    ang = pos_ref[...] * freq_ref[...]
    cos_ref[...] = jnp.cos(ang)
    sin_ref[...] = jnp.sin(ang)


def _rope_tables(positions):
    s = positions.shape[-1]
    nf = MLA_ROPE // 2
    rows = s // ROPE_PACK
    pos = jnp.repeat(positions.astype(F32).reshape(rows, ROPE_PACK), nf, axis=1)
    inv_freq = ROPE_THETA ** (-jnp.arange(0, MLA_ROPE, 2, dtype=F32) / MLA_ROPE)
    freq = jnp.tile(inv_freq, ROPE_PACK).reshape(1, V7X_LANES)
    tile = min(512, rows)
    cos_p, sin_p = pl.pallas_call(
        _rope_body,
        grid=(rows // tile,),
        in_specs=[_rows(tile, V7X_LANES), _resident((1, V7X_LANES))],
        out_specs=[_rows(tile, V7X_LANES), _rows(tile, V7X_LANES)],
        out_shape=[jax.ShapeDtypeStruct((rows, V7X_LANES), F32)] * 2,
        compiler_params=_cparams(("parallel",)),
        name="rope_tables",
    )(pos, freq)

    def place(t):
        t = t.reshape(s, nf)
        return jnp.concatenate([jnp.zeros((s, MLA_NOPE), F32), t, t,
                                jnp.zeros((s, HEAD_PAD - MLA_NOPE - MLA_ROPE), F32)], axis=1)

    return place(cos_p), place(sin_p)


_EV_OFF = np.cumsum((0, 256, 128, 128, 128, 256, 256, 512, 128, 512))


def _even_proj_body(h_ref, gpre_ref, win_ref, gq_ref, wuq_ref, gkv_ref, wuk_ref, wuvt_ref, wgate_ref, bgate_ref,
                    cos_ref, sin_ref,
                    q_ref, k_ref, vt_ref, gq_out, gk_out, gv_out, la_out, r_out):
    o = _EV_OFF
    hn = _rms(h_ref[...], gpre_ref[...]).astype(BF16)
    proj = _dot(hn, win_ref[...])
    c_q = proj[:, o[0]:o[1]]
    c_kv = proj[:, o[1]:o[2]]
    kpe_a = proj[:, o[2]:o[3]]
    kpe_b = proj[:, o[3]:o[4]]
    cosk = cos_ref[...]
    sink = sin_ref[...]
    nh = MLA_HEADS
    width = nh * HEAD_PAD

    qs = (MLA_NOPE + MLA_ROPE) ** -0.5 * LOG2E
    lane = lax.broadcasted_iota(jnp.int32, cosk.shape, 1)
    cosq = qs * jnp.where(lane < MLA_NOPE, 1.0, cosk)
    sinq = qs * sink
    cqn = _rms(c_q, gq_ref[...]).astype(BF16)
    qab = _dot(cqn, wuq_ref[...])
    q = qab[:, :width] * jnp.tile(cosq, (1, nh)) + qab[:, width:] * jnp.tile(sinq, (1, nh))
    q_ref[...] = q.astype(BF16)

    ckvn = _rms(c_kv, gkv_ref[...]).astype(BF16)
    kpe = kpe_a * cosk + kpe_b * sink
    k_ref[...] = (_dot(ckvn, wuk_ref[...]) + jnp.tile(kpe, (1, nh))).astype(BF16)
    vt = _dot_nt(wuvt_ref[...], ckvn).astype(BF16)
    row = lax.broadcasted_iota(jnp.int32, (ATTN_VROWS - MLA_V, vt.shape[1]), 0)
    extra = jnp.where(row == 0, 1.0, 0.0).astype(BF16)
    for hd in range(nh):
        vt_ref[hd, 0:MLA_V, :] = vt[hd * MLA_V:(hd + 1) * MLA_V]
        vt_ref[hd, MLA_V:ATTN_VROWS, :] = extra

    gq_out[...] = proj[:, o[4]:o[5]]
    gk_out[...] = proj[:, o[5]:o[6]]
    gv_out[...] = proj[:, o[6]:o[7]]
    glr = proj[:, o[7]:o[8]].astype(BF16)
    x = _dot(glr, wgate_ref[...]) + bgate_ref[...]
    la_out[...] = _log_sigmoid(x) * (1.0 / GLA_TAU)
    r_out[...] = proj[:, o[8]:o[9]]


def _even_proj(h, g_pre, w_in, g_q, w_uq, g_kv, w_uk, w_uvt, w_gate, b_gate, cos_t, sin_t):
    s = h.shape[0]
    tm = min(ROW_TILE, s)
    width = MLA_HEADS * HEAD_PAD
    gdk = GLA_HEADS * GLA_DK
    gdv = GLA_HEADS * GLA_DV
    rows = [(width, BF16), (width, BF16), None, (gdk, F32), (gdk, F32), (gdv, F32), (gdk, F32), (gdv, F32)]
    vt_spec = pl.BlockSpec((MLA_HEADS, None, ATTN_VROWS, tm), lambda i: (0, i, 0, 0))
    vt_shape = jax.ShapeDtypeStruct((MLA_HEADS, s // tm, ATTN_VROWS, tm), BF16)
    return pl.pallas_call(
        _even_proj_body,
        grid=(s // tm,),
        in_specs=[_rows(tm, D_MODEL), _resident(g_pre.shape), _resident(w_in.shape), _resident(g_q.shape),
                  _resident(w_uq.shape), _resident(g_kv.shape), _resident(w_uk.shape), _resident(w_uvt.shape),
                  _resident(w_gate.shape), _resident(b_gate.shape), _rows(tm, HEAD_PAD), _rows(tm, HEAD_PAD)],
        out_specs=[vt_spec if r is None else _rows(tm, r[0]) for r in rows],
        out_shape=[vt_shape if r is None else jax.ShapeDtypeStruct((s, r[0]), r[1]) for r in rows],
        compiler_params=_cparams(("parallel",)),
        name="even_proj",
    )(h, g_pre, w_in, g_q, w_uq, g_kv, w_uk, w_uvt, w_gate, b_gate, cos_t, sin_t)


def _prep_even_weights(w_in, w_uq, w_ukv, w_gate, b_gate):
    d = w_in.shape[0]
    off = np.cumsum((0,) + EVEN_SPLITS)
    seg = [w_in[:, off[i]:off[i + 1]] for i in range(len(EVEN_SPLITS))]
    c_q, c_kv, k_pe, gq, gk, gv, glr, r = seg
    half = MLA_ROPE // 2
    z = lambda n: jnp.zeros((d, n), F32)
    pad = HEAD_PAD - MLA_NOPE - MLA_ROPE
    kpe_a = jnp.concatenate([z(MLA_NOPE), k_pe, z(pad)], axis=1)
    kpe_b = jnp.concatenate([z(MLA_NOPE), -k_pe[:, half:], k_pe[:, :half], z(pad)], axis=1)
    glr_p = jnp.concatenate([glr, z(HEAD_PAD - GLA_GATE_RANK)], axis=1)
    w_in_p = jnp.concatenate([c_q, c_kv, kpe_a, kpe_b, gq, gk, gv, glr_p, r], axis=1).astype(BF16)

    nh = MLA_HEADS
    wq3 = w_uq.reshape(MLA_Q_RANK, nh, MLA_NOPE + MLA_ROPE)
    nope, rope = wq3[..., :MLA_NOPE], wq3[..., MLA_NOPE:]
    zq = lambda n: jnp.zeros((MLA_Q_RANK, nh, n), F32)
    wa = jnp.concatenate([nope, rope, zq(pad)], axis=-1).reshape(MLA_Q_RANK, nh * HEAD_PAD)
    wb = jnp.concatenate([zq(MLA_NOPE), -rope[..., half:], rope[..., :half], zq(pad)],
                         axis=-1).reshape(MLA_Q_RANK, nh * HEAD_PAD)
    w_uq_p = jnp.concatenate([wa, wb], axis=1).astype(BF16)

    wkv3 = w_ukv.reshape(MLA_KV_RANK, nh, MLA_NOPE + MLA_V)
    zk = jnp.zeros((MLA_KV_RANK, nh, HEAD_PAD - MLA_NOPE), F32)
    w_uk_p = jnp.concatenate([wkv3[..., :MLA_NOPE], zk], axis=-1).reshape(MLA_KV_RANK, nh * HEAD_PAD).astype(BF16)
    w_uvt_p = wkv3[..., MLA_NOPE:].reshape(MLA_KV_RANK, nh * MLA_V).T.astype(BF16)

    w_gate_p = jnp.concatenate([w_gate, jnp.zeros((HEAD_PAD - GLA_GATE_RANK, w_gate.shape[1]), F32)],
                               axis=0).astype(BF16)
    return w_in_p, w_uq_p, w_uk_p, w_uvt_p, w_gate_p, b_gate.reshape(1, -1)


def _mla_attn_body(q_ref, k_ref, vt_ref, o_ref, acc_scr, st_scr, p_scr, m_scr, alpha_scr, *, tq, tk, hg):
    i = pl.program_id(1)
    heads = range(hg)

    def scores(t, slot):
        start = pl.multiple_of(t * tk, tk)
        for hd in heads:
            lanes = slice(hd * HEAD_PAD, (hd + 1) * HEAD_PAD)
            st_scr[slot, hd] = _dot_nt(k_ref[pl.ds(start, tk), lanes], q_ref[:, lanes])

    def accumulate(t, slot):
        for hd in heads:
            acc_scr[hd] = alpha_scr[slot, hd] * acc_scr[hd] + _dot(vt_ref[hd, t], p_scr[slot, hd])

    def softmax(slot, mask_shift):
        for hd in heads:
            st = st_scr[slot, hd]
            if mask_shift is not None:
                key = lax.broadcasted_iota(jnp.int32, st.shape, 0) + mask_shift
                qry = lax.broadcasted_iota(jnp.int32, st.shape, 1)
                st = jnp.where(key <= qry, st, NEG_BIG)
            m_prev = m_scr[hd]
            m_next = jnp.maximum(m_prev, jnp.max(st, axis=0, keepdims=True))
            p_scr[slot, hd] = jnp.exp2(st - m_next).astype(BF16)
            alpha_scr[slot, hd] = jnp.exp2(m_prev - m_next)
            m_scr[hd] = m_next

    def step(t, slot, mask_shift, more):
        softmax(slot, mask_shift)
        if more:
            scores(t + 1, 1 - slot)
        accumulate(jnp.maximum(t - 1, 0), 1 - slot)

    acc_scr[...] = jnp.zeros(acc_scr.shape, F32)
    p_scr[1] = jnp.zeros(p_scr.shape[1:], BF16)
    alpha_scr[1] = jnp.ones(alpha_scr.shape[1:], F32)
    m_scr[...] = jnp.full(m_scr.shape, NEG_BIG, F32)
    scores(0, 0)

    def pair(u, c):
        step(2 * u, 0, None, True)
        step(2 * u + 1, 1, None, True)
        return c

    lax.fori_loop(0, i, pair, 0)
    step(2 * i, 0, 0, True)
    step(2 * i + 1, 1, tk, False)
    accumulate(2 * i + 1, 1)
    for hd in heads:
        acc = acc_scr[hd]
        o_ref[hd] = (acc[:MLA_V] * (1.0 / acc[MLA_V:MLA_V + 1])).astype(o_ref.dtype)


def _mla_attn(q, k, vt):
    s = q.shape[0]
    nk, tk = vt.shape[1], vt.shape[3]
    tq = 2 * tk
    hg = ATTN_HEADS_PER_STEP
    out = pl.pallas_call(
        functools.partial(_mla_attn_body, tq=tq, tk=tk, hg=hg),
        grid=(MLA_HEADS // hg, s // tq),
        in_specs=[pl.BlockSpec((tq, hg * HEAD_PAD), lambda g, i: (i, g)),
                  pl.BlockSpec((s, hg * HEAD_PAD), lambda g, i: (0, g), pipeline_mode=pl.Buffered(1)),
                  pl.BlockSpec((hg, nk, ATTN_VROWS, tk), lambda g, i: (g, 0, 0, 0), pipeline_mode=pl.Buffered(1))],
        out_specs=pl.BlockSpec((hg, MLA_V, tq), lambda g, i: (g, 0, i)),
        out_shape=jax.ShapeDtypeStruct((MLA_HEADS, MLA_V, s), BF16),
        scratch_shapes=[pltpu.VMEM((hg, ATTN_VROWS, tq), F32),
                        pltpu.VMEM((2, hg, tk, tq), F32),
                        pltpu.VMEM((2, hg, tk, tq), BF16),
                        pltpu.VMEM((hg, 1, tq), F32),
                        pltpu.VMEM((2, hg, 1, tq), F32)],
        compiler_params=_cparams(("parallel", "arbitrary")),
        name="mla_attn",
    )(q, k, vt)
    return out.reshape(MLA_HEADS * MLA_V, s)


def _gla_levels(chunk):
    n = 0
    while (1 << n) < chunk:
        n += 1
    return n


def _gla_constants(chunk):
    t = np.arange(chunk)
    mats = [t[None, :] <= t[:, None]]
    masks = []
    b = 1
    while b < chunk:
        blk = t // (2 * b)
        upper = (t % (2 * b)) >= b
        e = blk * 2 * b + b - 1
        up_rows = upper[:, None] & (t[None, :] > e[:, None]) & (t[None, :] <= t[:, None])
        lo_rows = (~upper)[:, None] & (t[None, :] > t[:, None]) & (t[None, :] <= e[:, None])
        mats.append(up_rows | lo_rows)
        masks.append(upper[:, None] & (~upper)[None, :] & (blk[:, None] == blk[None, :]))
        b *= 2
    masks.append(t[:, None] == t[None, :])
    mats = np.concatenate(mats, axis=0).astype(np.float32)
    masks = np.stack(masks).astype(np.float32)
    masks = np.tile(masks, (1, GLA_HEADS, 1))
    return jnp.asarray(mats, BF16), jnp.asarray(masks, F32)


def _gla_body(q_ref, k_ref, v_ref, g_ref, mats_ref, masks_ref, o_ref, st_ref, *, chunk, n_sub):
    nlev = _gla_levels(chunk)
    nh, dk, dv = GLA_HEADS, GLA_DK, GLA_DV
    w = nh * dk

    @pl.when(pl.program_id(0) == 0)
    def _():
        st_ref[...] = jnp.zeros(st_ref.shape, F32)

    head_of_lane = lax.broadcasted_iota(jnp.int32, (chunk, w), 1) // dk

    def stack_heads(x):
        return jnp.concatenate([jnp.where(head_of_lane == hd, x, 0.0) for hd in range(nh)], axis=0).astype(BF16)

    chunks = [slice(c * chunk, (c + 1) * chunk) for c in range(n_sub)]
    qs = [q_ref[rows, :] * (dk ** -0.5) for rows in chunks]
    ks = [k_ref[rows, :] for rows in chunks]
    e_alls = [_dot(mats_ref[...], jnp.concatenate(_split2(g_ref[rows, :]), axis=1)) for rows in chunks]
    e_alls = [x[:, :w] + x[:, w:] for x in e_alls]
    attn = [_dot_nt(stack_heads(q), k.astype(BF16)) * masks_ref[nlev] for q, k in zip(qs, ks)]
    for lv in range(nlev):
        decs = [jnp.exp(x[(1 + lv) * chunk:(2 + lv) * chunk]) for x in e_alls]
        attn = [a + _dot_nt(stack_heads(q * d), (k * d).astype(BF16)) * masks_ref[lv]
                for a, q, k, d in zip(attn, qs, ks, decs)]
    attn = [a.astype(BF16) for a in attn]
    for c, rows in enumerate(chunks):
        for hd in range(nh):
            vh = v_ref[rows, hd * dv:(hd + 1) * dv].astype(BF16)
            o_ref[rows, hd * dv:(hd + 1) * dv] = _dot(attn[c][hd * chunk:(hd + 1) * chunk], vh)
    pending = []
    for q, k, x in zip(qs, ks, e_alls):
        b = x[:chunk]
        b_end = b[chunk - 1:chunk]
        pending.append((stack_heads(q * jnp.exp(b)), (k * jnp.exp(b_end - b)).astype(BF16), jnp.exp(b_end)))

    st = st_ref[...]
    head_of_state_lane = lax.broadcasted_iota(jnp.int32, st.shape, 1) // dk
    for c in range(n_sub):
        rows = slice(c * chunk, (c + 1) * chunk)
        qg, kg, dec_end = pending[c]
        inter = _dot_nt(qg, st.astype(BF16))
        new = st * dec_end
        for hd in range(nh):
            cols = slice(hd * dv, (hd + 1) * dv)
            o_ref[rows, cols] = o_ref[rows, cols] + inter[hd * chunk:(hd + 1) * chunk]
            upd = _dot_tn(v_ref[rows, cols].astype(BF16), kg)
            new = new + jnp.where(head_of_state_lane == hd, upd, 0.0)
        st = new
    st_ref[...] = st


def _gla(gq, gk, gv, log_a):
    s = gq.shape[0]
    chunk = min(SCAN_CHUNK, s)
    n_sub = min(GLA_CHUNKS_PER_STEP, s // chunk)
    rows = chunk * n_sub
    mats, masks = _gla_constants(chunk)
    gdk = GLA_HEADS * GLA_DK
    gdv = GLA_HEADS * GLA_DV
    return pl.pallas_call(
        functools.partial(_gla_body, chunk=chunk, n_sub=n_sub),
        grid=(s // rows,),
        in_specs=[_rows(rows, gdk), _rows(rows, gdk), _rows(rows, gdv), _rows(rows, gdk),
                  _resident(mats.shape), _resident(masks.shape)],
        out_specs=_rows(rows, gdv),
        out_shape=jax.ShapeDtypeStruct((s, gdv), F32),
        scratch_shapes=[pltpu.VMEM((GLA_DV, gdk), F32)],
        compiler_params=_cparams(("arbitrary",)),
        name="gla_scan",
    )(gq, gk, gv, log_a, mats, masks)


def _xattn_memory(mem_ref, gmem_ref, wk_ref, wv_ref, k_scr, v_scr):
    @pl.when(pl.program_id(0) == 0)
    def _():
        mem_n = _rms(mem_ref[...], gmem_ref[...]).astype(BF16)
        k_scr[...] = _dot(mem_n, wk_ref[...].astype(BF16)).astype(BF16)
        v_scr[...] = _dot(mem_n, wv_ref[...].astype(BF16)).astype(BF16)


def _xattn_rows(h, gpre_ref, wq_ref, wo_ref, gpost_ref, k_scr, v_scr):
    hn = _rms(h, gpre_ref[...]).astype(BF16)
    q = _dot(hn, wq_ref[...].astype(BF16)).astype(BF16)
    dh = XATTN_DH
    scale = dh ** -0.5
    heads = [slice(hd * dh, (hd + 1) * dh) for hd in range(XATTN_HEADS)]
    scores = [_dot_nt(q[:, sl], k_scr[:, sl]) * scale for sl in heads]
    probs = [jnp.exp(s - jnp.max(s, axis=-1, keepdims=True)) for s in scores]
    inv_l = [1.0 / jnp.sum(p, axis=-1, keepdims=True) for p in probs]
    outs = [(_dot(p.astype(BF16), v_scr[:, sl]) * il).astype(BF16) for p, sl, il in zip(probs, heads, inv_l)]
    o = jnp.concatenate(outs, axis=1)
    return h + _rms(_dot(o, wo_ref[...].astype(BF16)), gpost_ref[...])


def _layer_resident(shape, layer):
    rest = tuple(shape[1:])
    return pl.BlockSpec((None,) + rest, lambda *_: (layer,) + (0,) * len(rest), pipeline_mode=pl.Buffered(1))


def _subtiles(rows):
    sub = min(OUT_SUBTILE, rows)
    return [slice(i * sub, (i + 1) * sub) for i in range(rows // sub)]


def _xattn_operands(xattn_args, layer):
    mem, g_mem, g_pre, w_q, w_k, w_v, w_o, g_post = xattn_args
    ins = (mem, g_mem, g_pre, w_q, w_k, w_v, w_o, g_post)
    specs = ([_resident(a.shape) for a in (mem, g_mem, g_pre)]
             + [_layer_resident(w.shape, layer) for w in (w_q, w_k, w_v, w_o)] + [_resident(g_post.shape)])
    scratch = [pltpu.VMEM((mem.shape[0], D_MODEL), BF16), pltpu.VMEM((mem.shape[0], D_MODEL), BF16)]
    return ins, specs, scratch


def _even_out_body(h_ref, at_ref, o_ref, r_ref, ggla_ref, wout_ref, gpost_ref,
                   mem_ref, gmem_ref, xpre_ref, wq_ref, wk_ref, wv_ref, wo_ref, xpost_ref, out_ref, k_scr, v_scr):
    _xattn_memory(mem_ref, gmem_ref, wk_ref, wv_ref, k_scr, v_scr)
    subs = _subtiles(h_ref.shape[0])
    n_a = at_ref.shape[0]
    hs = []
    for rows in subs:
        og = o_ref[rows, :]
        gn = jnp.concatenate([_rms(og[:, hd * GLA_DV:(hd + 1) * GLA_DV], ggla_ref[...])
                              for hd in range(GLA_HEADS)], axis=1)
        g = (gn * _silu(r_ref[rows, :])).astype(BF16)
        mix = (_dot_tn(at_ref[:, rows], wout_ref[:n_a, :].astype(BF16))
               + _dot(g, wout_ref[n_a:, :].astype(BF16)))
        hs.append(h_ref[rows, :] + _rms(mix, gpost_ref[...]))
    for rows, h in zip(subs, hs):
        out_ref[rows, :] = _xattn_rows(h, xpre_ref, wq_ref, wo_ref, xpost_ref, k_scr, v_scr)


def _even_out_xattn(h, a_t, o_gla, r, g_gla, w_out, g_post, xattn_args, layer):
    s = h.shape[0]
    tm = min(OUT_TILE, s)
    x_ins, x_specs, x_scratch = _xattn_operands(xattn_args, layer)
    return pl.pallas_call(
        _even_out_body,
        grid=(s // tm,),
        in_specs=[_rows(tm, D_MODEL), pl.BlockSpec((a_t.shape[0], tm), lambda i: (0, i)),
                  _rows(tm, o_gla.shape[1]), _rows(tm, r.shape[1]),
                  _resident(g_gla.shape), _layer_resident(w_out.shape, layer // 2), _resident(g_post.shape)] + x_specs,
        out_specs=_rows(tm, D_MODEL),
        out_shape=jax.ShapeDtypeStruct((s, D_MODEL), F32),
        scratch_shapes=x_scratch,
        compiler_params=_cparams(("arbitrary",)),
        name="even_out_xattn",
    )(h, a_t, o_gla, r, g_gla, w_out, g_post, *x_ins)


CONV_HALO = 8


def _odd_proj_body(h_ref, gpre_ref, win_ref, cw_ref, cb_ref, wq_ref, wk_ref, wkt_ref, wv_ref, wgq_ref, wgk_ref,
                   wgv_ref, bg_ref, q_out, k_out, kt_out, v_out, gates_out, xc_out, z_out, xbuf):
    tm = h_ref.shape[0]
    width = D_MODEL

    @pl.when(pl.program_id(0) == 0)
    def _():
        xbuf[0:CONV_HALO, :] = jnp.zeros((CONV_HALO, width), F32)

    hn = _rms(h_ref[...], gpre_ref[...]).astype(BF16)
    xz = _dot(hn, win_ref[...].astype(BF16))
    x_m = xz[:, :width]
    z_out[...] = xz[:, width:]
    xbuf[CONV_HALO:CONV_HALO + tm, :] = x_m
    conv = cb_ref[...] + cw_ref[MLSTM_CONV - 1:MLSTM_CONV, :] * x_m
    for j in range(MLSTM_CONV - 1):
        back = MLSTM_CONV - 1 - j
        conv = conv + cw_ref[j:j + 1, :] * xbuf[CONV_HALO - back:CONV_HALO - back + tm, :]
    xbuf[0:CONV_HALO, :] = x_m[tm - CONV_HALO:, :]
    x_c = _silu(conv)
    xc_out[...] = x_c
    xcb = x_c.astype(BF16)

    dh = MLSTM_DH
    xmb = x_m.astype(BF16)
    qs, ks, vs = [], [], []
    for hd in range(MLSTM_HEADS):
        sl = slice(hd * dh, (hd + 1) * dh)
        qs.append(_dot(xcb[:, sl], wq_ref[hd]))
        ks.append(_dot(xcb[:, sl], wk_ref[hd]))
        vs.append(_dot(xmb[:, sl], wv_ref[hd]))
        kt_out[sl, :] = _dot_nt(wkt_ref[hd], xcb[:, sl]).astype(BF16)
    q = jnp.concatenate(qs, axis=1).astype(BF16)
    k = jnp.concatenate(ks, axis=1).astype(BF16)
    v = jnp.concatenate(vs, axis=1).astype(BF16)
    gates_out[...] = _dot(q, wgq_ref[...]) + _dot(k, wgk_ref[...]) + _dot(v, wgv_ref[...]) + bg_ref[...]
    q_out[...] = (q.astype(F32) * (dh ** -0.5)).astype(BF16)
    k_out[...] = k
    v_out[...] = v


def _odd_proj(h, g_pre, w_in, conv_w, conv_b, w_q, w_k, w_kt, w_v, wg_q, wg_k, wg_v, b_g, j):
    s = h.shape[0]
    tm = min(ROW_TILE, s)
    rows = [(D_MODEL, BF16), (D_MODEL, BF16), None, (D_MODEL, BF16), (V7X_LANES, F32), (D_MODEL, F32),
            (D_MODEL, F32)]
    ins = (g_pre, w_in, conv_w, conv_b, w_q, w_k, w_kt, w_v, wg_q, wg_k, wg_v, b_g)
    kt_spec = pl.BlockSpec((D_MODEL, tm), lambda i: (0, i))
    kt_shape = jax.ShapeDtypeStruct((D_MODEL, s), BF16)
    return pl.pallas_call(
        _odd_proj_body,
        grid=(s // tm,),
        in_specs=([_rows(tm, D_MODEL), _resident(g_pre.shape), _layer_resident(w_in.shape, j)]
                  + [_resident(a.shape) for a in ins[2:]]),
        out_specs=[kt_spec if r is None else _rows(tm, r[0]) for r in rows],
        out_shape=[kt_shape if r is None else jax.ShapeDtypeStruct((s, r[0]), r[1]) for r in rows],
        scratch_shapes=[pltpu.VMEM((CONV_HALO + tm, D_MODEL), F32)],
        compiler_params=_cparams(("arbitrary",)),
        name="odd_proj",
    )(h, *ins)


def _prep_odd_gate_weights(w_gates, b_gates):
    nh, dh = MLSTM_HEADS, MLSTM_DH
    w4 = w_gates.reshape(nh, 3, dh, 2 * nh)
    pad = jnp.zeros((nh * dh, V7X_LANES - 2 * nh), F32)
    parts = [jnp.concatenate([w4[:, j].reshape(nh * dh, 2 * nh), pad], axis=1).astype(BF16) for j in range(3)]
    b = jnp.concatenate([b_gates, jnp.zeros((V7X_LANES - 2 * nh,), F32)]).reshape(1, V7X_LANES)
    return parts[0], parts[1], parts[2], b


MLSTM_EXT = MLSTM_DH + V7X_LANES


def _mlstm_body(q_ref, k_ref, kt_ref, v_ref, gcol_ref, grow_ref, tri_ref, trit_ref, o_ref, c_scr, m_scr, *,
                chunk, n_sub):
    nh, dh = MLSTM_HEADS, MLSTM_DH
    lanes = V7X_LANES

    @pl.when(pl.program_id(0) == 0)
    def _():
        c_scr[...] = jnp.zeros(c_scr.shape, F32)
        m_scr[...] = jnp.zeros(m_scr.shape, F32)

    row = lax.broadcasted_iota(jnp.int32, (chunk, chunk), 0)
    col = lax.broadcasted_iota(jnp.int32, (chunk, chunk), 1)
    causal = col <= row
    ones_col = jnp.where(lax.broadcasted_iota(jnp.int32, (chunk, lanes), 1) == 0, 1.0, 0.0).astype(BF16)

    chunks = [slice(c * chunk, (c + 1) * chunk) for c in range(n_sub)]
    heads = [slice(hd * dh, (hd + 1) * dh) for hd in range(nh)]
    pairs = [(c, hd) for c in range(n_sub) for hd in range(nh)]
    b_cols = [_dot(tri_ref[...], jnp.concatenate(_split3(_log_sigmoid(gcol_ref[rows, :])), axis=1))
              for rows in chunks]
    b_cols = [x[:, :lanes] + x[:, lanes:2 * lanes] + x[:, 2 * lanes:] for x in b_cols]
    g_rows = [grow_ref[c] for c in range(n_sub)]
    nr = g_rows[0].shape[0]
    b_rows = [_dot(jnp.concatenate(_split3(_log_sigmoid(gr)), axis=0), trit_ref[...]) for gr in g_rows]
    b_rows = [x[:nr] + x[nr:2 * nr] + x[2 * nr:] for x in b_rows]
    r_h = {(c, hd): g_rows[c][hd:hd + 1, :] - b_rows[c][nh + hd:nh + hd + 1, :] for c, hd in pairs}
    b_h = {(c, hd): b_cols[c][:, nh + hd:nh + hd + 1] for c, hd in pairs}
    logw = {p: jnp.where(causal, r_h[p], NEG_BIG) for p in pairs}
    cm = {p: jnp.max(logw[p], axis=-1, keepdims=True) for p in pairs}
    qk = {(c, hd): _dot_nt(q_ref[chunks[c], heads[hd]], k_ref[chunks[c], heads[hd]]) for c, hd in pairs}
    w_intra = {p: (jnp.exp(logw[p] - cm[p]) * qk[p]).astype(BF16) for p in pairs}
    vext = {(c, hd): jnp.concatenate([v_ref[chunks[c], heads[hd]], ones_col], axis=1) for c, hd in pairs}
    intra = {p: _dot(w_intra[p], vext[p]) for p in pairs}

    for c, hd in pairs:
        p = (c, hd)
        rows, sl = chunks[c], heads[hd]
        m_old = m_scr[hd:hd + 1, 0:1]
        cext = c_scr[hd]
        big_m = jnp.maximum(cm[p], m_old)
        w_inter = jnp.exp(m_old - big_m)
        out = (jnp.exp(cm[p] - big_m) * intra[p]
               + _dot((q_ref[rows, sl].astype(F32) * w_inter).astype(BF16), cext.astype(BF16)))
        den = out[:, dh:dh + 1]
        o_ref[rows, sl] = out[:, :dh] / jnp.maximum(jnp.abs(den), jnp.exp(-(b_h[p] + big_m)))

        m_last = big_m[chunk - 1:chunk, :]
        w_s = jnp.exp(r_h[p] - m_last)
        ktw = (kt_ref[sl, rows].astype(F32) * w_s).astype(BF16)
        c_scr[hd] = jnp.exp(m_old - m_last) * cext + _dot(ktw, vext[p])
        m_scr[hd:hd + 1, :] = jnp.broadcast_to(b_h[p][chunk - 1:chunk, :] + m_last, (1, m_scr.shape[1]))


def _mlstm(q, k, kt, v, gates):
    s = q.shape[0]
    chunk = min(SCAN_CHUNK, s)
    nc = s // chunk
    n_sub = min(MLSTM_CHUNKS_PER_STEP, nc)
    rows = chunk * n_sub
    lanes = V7X_LANES
    ng = 2 * MLSTM_HEADS
    grow = gates[:, :ng].reshape(nc, chunk, ng).transpose(0, 2, 1)
    t = np.arange(chunk)
    tri = (t[None, :] <= t[:, None]).astype(np.float32)
    return pl.pallas_call(
        functools.partial(_mlstm_body, chunk=chunk, n_sub=n_sub),
        grid=(s // rows,),
        in_specs=[_rows(rows, D_MODEL), _rows(rows, D_MODEL), pl.BlockSpec((D_MODEL, rows), lambda c: (0, c)),
                  _rows(rows, D_MODEL), _rows(rows, lanes),
                  pl.BlockSpec((n_sub, ng, chunk), lambda c: (c, 0, 0)),
                  _resident((chunk, chunk)), _resident((chunk, chunk))],
        out_specs=_rows(rows, D_MODEL),
        out_shape=jax.ShapeDtypeStruct((s, D_MODEL), F32),
        scratch_shapes=[pltpu.VMEM((MLSTM_HEADS, MLSTM_DH, MLSTM_EXT), F32),
                        pltpu.VMEM((8, lanes), F32)],
        compiler_params=_cparams(("arbitrary",)),
        name="mlstm_scan",
    )(q, k, kt, v, gates, grow, jnp.asarray(tri, BF16), jnp.asarray(tri.T, BF16))


def _odd_out_body(h_ref, hc_ref, xc_ref, z_ref, gh_ref, skip_ref, wo_ref, gpost_ref,
                  mem_ref, gmem_ref, xpre_ref, wq_ref, wk_ref, wv_ref, xwo_ref, xpost_ref, out_ref, k_scr, v_scr):
    _xattn_memory(mem_ref, gmem_ref, wk_ref, wv_ref, k_scr, v_scr)
    dh = MLSTM_DH
    subs = _subtiles(h_ref.shape[0])
    hs = []
    for rows in subs:
        hc = hc_ref[rows, :]
        parts = []
        for hd in range(MLSTM_HEADS):
            seg = hc[:, hd * dh:(hd + 1) * dh]
            cen = seg - jnp.mean(seg, axis=-1, keepdims=True)
            parts.append(cen * lax.rsqrt(jnp.mean(cen * cen, axis=-1, keepdims=True) + EPS))
        hn = jnp.concatenate(parts, axis=1) * gh_ref[...]
        out = ((hn + skip_ref[...] * xc_ref[rows, :]) * _silu(z_ref[rows, :])).astype(BF16)
        hs.append(h_ref[rows, :] + _rms(_dot(out, wo_ref[...].astype(BF16)), gpost_ref[...]))
    for rows, h in zip(subs, hs):
        out_ref[rows, :] = _xattn_rows(h, xpre_ref, wq_ref, xwo_ref, xpost_ref, k_scr, v_scr)


def _odd_out_xattn(h, hcell, x_c, z, g_hnorm, skip, w_out, g_post, xattn_args, layer):
    s = h.shape[0]
    tm = min(OUT_TILE, s)
    x_ins, x_specs, x_scratch = _xattn_operands(xattn_args, layer)
    return pl.pallas_call(
        _odd_out_body,
        grid=(s // tm,),
        in_specs=[_rows(tm, D_MODEL)] * 4 + [_resident(g_hnorm.shape), _resident(skip.shape),
                                              _layer_resident(w_out.shape, layer // 2),
                                              _resident(g_post.shape)] + x_specs,
        out_specs=_rows(tm, D_MODEL),
        out_shape=jax.ShapeDtypeStruct((s, D_MODEL), F32),
        scratch_shapes=x_scratch,
        compiler_params=_cparams(("arbitrary",)),
        name="odd_out_xattn",
    )(h, hcell, x_c, z, g_hnorm, skip, w_out, g_post, *x_ins)


def _ffn_body(h_ref, gpre_ref, w1_ref, w2_ref, gpost_ref, out_ref):
    h = h_ref[...]
    xn = _rms(h, gpre_ref[...]).astype(BF16)
    acc = None
    for c in range(D_FF // FF_CHUNK):
        sl = slice(c * FF_CHUNK, (c + 1) * FF_CHUNK)
        a = jnp.square(jnp.maximum(_dot(xn, w1_ref[:, sl].astype(BF16)), 0.0)).astype(BF16)
        part = _dot(a, w2_ref[sl, :].astype(BF16))
        acc = part if acc is None else acc + part
    out_ref[...] = h + _rms(acc, gpost_ref[...])


def _ffn(h, g_pre, w1, w2, g_post, layer):
    s = h.shape[0]
    tm = min(ROW_TILE, s)
    return pl.pallas_call(
        _ffn_body,
        grid=(s // tm,),
        in_specs=[_rows(tm, D_MODEL), _resident(g_pre.shape), _layer_resident(w1.shape, layer),
                  _layer_resident(w2.shape, layer), _resident(g_post.shape)],
        out_specs=_rows(tm, D_MODEL),
        out_shape=jax.ShapeDtypeStruct((s, D_MODEL), F32),
        compiler_params=_cparams(("parallel",)),
        name="sq_relu_mlp",
    )(h, g_pre, w1, w2, g_post)


def _row(g):
    return g.reshape(1, -1)


def kernel(x, mem, positions, g_mix_pre, g_mix_post, g_xattn_pre, g_xattn_post, g_mem, g_ffn_pre, g_ffn_post, ev_w_in, ev_g_q, ev_w_uq, ev_g_kv, ev_w_ukv, ev_w_gate, ev_b_gate, ev_g_gla, ev_w_out, od_w_in, od_conv_w, od_conv_b, od_w_q, od_w_k, od_w_v, od_w_gates, od_b_gates, od_g_hnorm, od_skip, od_w_out, xa_w_q, xa_w_k, xa_w_v, xa_w_o, ffn_w1, ffn_w2):
    batch, seq, d = x.shape
    assert batch == 1 and d == D_MODEL and seq % ROW_TILE == 0 and seq % (2 * ATTN_TK) == 0
    assert seq % (SCAN_CHUNK * GLA_CHUNKS_PER_STEP) == 0 and seq % (SCAN_CHUNK * MLSTM_CHUNKS_PER_STEP) == 0
    assert ROW_TILE == ATTN_TK and seq % OUT_TILE == 0
    h = x.reshape(seq, d)
    mem2 = mem.reshape(mem.shape[1], d)
    cos_t, sin_t = _rope_tables(positions)

    for layer in range(DEPTH):
        j = layer // 2
        xattn_args = (mem2, _row(g_mem[layer]), _row(g_xattn_pre[layer]), xa_w_q, xa_w_k, xa_w_v, xa_w_o,
                      _row(g_xattn_post[layer]))
        if layer % 2 == 0:
            w_in_p, w_uq_p, w_uk_p, w_uvt_p, w_gate_p, b_gate = _prep_even_weights(
                ev_w_in[j], ev_w_uq[j], ev_w_ukv[j], ev_w_gate[j], ev_b_gate[j])
            q, k, vt, gq, gk, gv, log_a, r = _even_proj(
                h, _row(g_mix_pre[layer]), w_in_p, _row(ev_g_q[j]), w_uq_p, _row(ev_g_kv[j]), w_uk_p, w_uvt_p,
                w_gate_p, b_gate, cos_t, sin_t)
            a_t = _mla_attn(q, k, vt)
            o_gla = _gla(gq, gk, gv, log_a)
            h = _even_out_xattn(h, a_t, o_gla, r, _row(ev_g_gla[j]), ev_w_out, _row(g_mix_post[layer]), xattn_args,
                                layer)
        else:
            wg_q, wg_k, wg_v, b_g = _prep_odd_gate_weights(od_w_gates[j], od_b_gates[j])
            w_k = od_w_k[j].astype(BF16)
            q, k, kt, v, gates, x_c, z = _odd_proj(
                h, _row(g_mix_pre[layer]), od_w_in, od_conv_w[j], _row(od_conv_b[j]), od_w_q[j].astype(BF16), w_k,
                w_k.transpose(0, 2, 1), od_w_v[j].astype(BF16), wg_q, wg_k, wg_v, b_g, j)
            hcell = _mlstm(q, k, kt, v, gates)
            h = _odd_out_xattn(h, hcell, x_c, z, _row(od_g_hnorm[j]), _row(od_skip[j]), od_w_out,
                               _row(g_mix_post[layer]), xattn_args, layer)
        h = _ffn(h, _row(g_ffn_pre[layer]), ffn_w1, ffn_w2, _row(g_ffn_post[layer]), layer)
    return h.reshape(batch, seq, d)
```

```python
import functools

import numpy as np
import jax
import jax.numpy as jnp
from jax import lax
from jax.experimental import pallas as pl
from jax.experimental.pallas import tpu as pltpu

F32 = jnp.float32
BF16 = jnp.bfloat16

D_MODEL = 1024
DEPTH = 4
EPS = 1e-6
MLA_HEADS = 8
MLA_NOPE = 64
MLA_ROPE = 32
MLA_V = 64
MLA_Q_RANK = 256
MLA_KV_RANK = 128
ROPE_THETA = 10000.0
GLA_HEADS = 4
GLA_DK = 64
GLA_DV = 128
GLA_GATE_RANK = 16
GLA_TAU = 16.0
MLSTM_HEADS = 4
MLSTM_DH = 256
MLSTM_CONV = 4
XATTN_HEADS = 4
XATTN_DH = 256
D_FF = 4096
EVEN_SPLITS = (256, 128, 32, 256, 256, 512, 16, 512)

V7X_LANES = 128
V7X_VMEM_BYTES = 64 * 1024 * 1024
VMEM_LIMIT = (V7X_VMEM_BYTES * 7) // 8

ROW_TILE = 512
OUT_TILE = 512
OUT_SUBTILE = 256
FF_CHUNK = 1024
ATTN_TK = 512
ATTN_HEADS_PER_STEP = 2
SCAN_CHUNK = 128
GLA_CHUNKS_PER_STEP = 4
MLSTM_CHUNKS_PER_STEP = 4
HEAD_PAD = V7X_LANES
ATTN_VROWS = MLA_V + 16

LOG2E = 1.4426950408889634
NEG_BIG = -1e30


def _cparams(sem):
    return pltpu.CompilerParams(dimension_semantics=sem, vmem_limit_bytes=VMEM_LIMIT)


def _resident(shape):
    nd = len(shape)
    return pl.BlockSpec(shape, lambda *_: (0,) * nd, pipeline_mode=pl.Buffered(1))


def _rows(tile, width):
    return pl.BlockSpec((tile, width), lambda i: (i, 0))


def _rms(x, g):
    return x * lax.rsqrt(jnp.mean(x * x, axis=-1, keepdims=True) + EPS) * g


def _silu(x):
    return x * (1.0 / (1.0 + jnp.exp(-x)))


def _log_sigmoid(x):
    return jnp.minimum(x, 0.0) - jnp.log(1.0 + jnp.exp(-jnp.abs(x)))


def _dot(a, b):
    return jnp.dot(a, b, preferred_element_type=F32)


def _dot_nt(a, b):
    return lax.dot_general(a, b, (((1,), (1,)), ((), ())), preferred_element_type=F32)


def _dot_tn(a, b):
    return lax.dot_general(a, b, (((0,), (0,)), ((), ())), preferred_element_type=F32)


def _split2(x):
    x1 = x.astype(BF16)
    return x1, (x - x1.astype(F32)).astype(BF16)


def _split3(x):
    x1 = x.astype(BF16)
    r = x - x1.astype(F32)
    x2 = r.astype(BF16)
    x3 = (r - x2.astype(F32)).astype(BF16)
    return x1, x2, x3


ROPE_PACK = V7X_LANES // (MLA_ROPE // 2)


def _rope_body(pos_ref, freq_ref, cos_ref, sin_ref):
    ang = pos_ref[...] * freq_ref[...]
    cos_ref[...] = jnp.cos(ang)
    sin_ref[...] = jnp.sin(ang)


def _rope_tables(positions):
    s = positions.shape[-1]
    nf = MLA_ROPE // 2
    rows = s // ROPE_PACK
    pos = jnp.repeat(positions.astype(F32).reshape(rows, ROPE_PACK), nf, axis=1)
    inv_freq = ROPE_THETA ** (-jnp.arange(0, MLA_ROPE, 2, dtype=F32) / MLA_ROPE)
    freq = jnp.tile(inv_freq, ROPE_PACK).reshape(1, V7X_LANES)
    tile = min(512, rows)
    cos_p, sin_p = pl.pallas_call(
        _rope_body,
        grid=(rows // tile,),
        in_specs=[_rows(tile, V7X_LANES), _resident((1, V7X_LANES))],
        out_specs=[_rows(tile, V7X_LANES), _rows(tile, V7X_LANES)],
        out_shape=[jax.ShapeDtypeStruct((rows, V7X_LANES), F32)] * 2,
        compiler_params=_cparams(("parallel",)),
        name="rope_tables",
    )(pos, freq)

    def place(t):
        t = t.reshape(s, nf)
        return jnp.concatenate([jnp.zeros((s, MLA_NOPE), F32), t, t,
                                jnp.zeros((s, HEAD_PAD - MLA_NOPE - MLA_ROPE), F32)], axis=1)

    return place(cos_p), place(sin_p)


_EV_OFF = np.cumsum((0, 256, 128, 128, 128, 256, 256, 512, 128, 512))


def _even_proj_body(h_ref, gpre_ref, win_ref, gq_ref, wuq_ref, gkv_ref, wuk_ref, wuvt_ref, wgate_ref, bgate_ref,
                    cos_ref, sin_ref,
                    q_ref, k_ref, vt_ref, gq_out, gk_out, gv_out, la_out, r_out):
    o = _EV_OFF
    hn = _rms(h_ref[...], gpre_ref[...]).astype(BF16)
    proj = _dot(hn, win_ref[...])
    c_q = proj[:, o[0]:o[1]]
    c_kv = proj[:, o[1]:o[2]]
    kpe_a = proj[:, o[2]:o[3]]
    kpe_b = proj[:, o[3]:o[4]]
    cosk = cos_ref[...]
    sink = sin_ref[...]
    nh = MLA_HEADS
    width = nh * HEAD_PAD

    qs = (MLA_NOPE + MLA_ROPE) ** -0.5 * LOG2E
    lane = lax.broadcasted_iota(jnp.int32, cosk.shape, 1)
    cosq = qs * jnp.where(lane < MLA_NOPE, 1.0, cosk)
    sinq = qs * sink
    cqn = _rms(c_q, gq_ref[...]).astype(BF16)
    qab = _dot(cqn, wuq_ref[...])
    q = qab[:, :width] * jnp.tile(cosq, (1, nh)) + qab[:, width:] * jnp.tile(sinq, (1, nh))
    q_ref[...] = q.astype(BF16)

    ckvn = _rms(c_kv, gkv_ref[...]).astype(BF16)
    kpe = kpe_a * cosk + kpe_b * sink
    k_ref[...] = (_dot(ckvn, wuk_ref[...]) + jnp.tile(kpe, (1, nh))).astype(BF16)
    vt = _dot_nt(wuvt_ref[...], ckvn).astype(BF16)
    row = lax.broadcasted_iota(jnp.int32, (ATTN_VROWS - MLA_V, vt.shape[1]), 0)
    extra = jnp.where(row == 0, 1.0, 0.0).astype(BF16)
    for hd in range(nh):
        vt_ref[hd, 0:MLA_V, :] = vt[hd * MLA_V:(hd + 1) * MLA_V]
        vt_ref[hd, MLA_V:ATTN_VROWS, :] = extra

    gq_out[...] = proj[:, o[4]:o[5]]
    gk_out[...] = proj[:, o[5]:o[6]]
    gv_out[...] = proj[:, o[6]:o[7]]
    glr = proj[:, o[7]:o[8]].astype(BF16)
    x = _dot(glr, wgate_ref[...]) + bgate_ref[...]
    la_out[...] = _log_sigmoid(x) * (1.0 / GLA_TAU)
    r_out[...] = proj[:, o[8]:o[9]]


def _even_proj(h, g_pre, w_in, g_q, w_uq, g_kv, w_uk, w_uvt, w_gate, b_gate, cos_t, sin_t):
    s = h.shape[0]
    tm = min(ROW_TILE, s)
    width = MLA_HEADS * HEAD_PAD
    gdk = GLA_HEADS * GLA_DK
    gdv = GLA_HEADS * GLA_DV
    rows = [(width, BF16), (width, BF16), None, (gdk, F32), (gdk, F32), (gdv, F32), (gdk, F32), (gdv, F32)]
    vt_spec = pl.BlockSpec((MLA_HEADS, None, ATTN_VROWS, tm), lambda i: (0, i, 0, 0))
    vt_shape = jax.ShapeDtypeStruct((MLA_HEADS, s // tm, ATTN_VROWS, tm), BF16)
    return pl.pallas_call(
        _even_proj_body,
        grid=(s // tm,),
        in_specs=[_rows(tm, D_MODEL), _resident(g_pre.shape), _resident(w_in.shape), _resident(g_q.shape),
                  _resident(w_uq.shape), _resident(g_kv.shape), _resident(w_uk.shape), _resident(w_uvt.shape),
                  _resident(w_gate.shape), _resident(b_gate.shape), _rows(tm, HEAD_PAD), _rows(tm, HEAD_PAD)],
        out_specs=[vt_spec if r is None else _rows(tm, r[0]) for r in rows],
        out_shape=[vt_shape if r is None else jax.ShapeDtypeStruct((s, r[0]), r[1]) for r in rows],
        compiler_params=_cparams(("parallel",)),
        name="even_proj",
    )(h, g_pre, w_in, g_q, w_uq, g_kv, w_uk, w_uvt, w_gate, b_gate, cos_t, sin_t)


def _prep_even_weights(w_in, w_uq, w_ukv, w_gate, b_gate):
    d = w_in.shape[0]
    off = np.cumsum((0,) + EVEN_SPLITS)
    seg = [w_in[:, off[i]:off[i + 1]] for i in range(len(EVEN_SPLITS))]
    c_q, c_kv, k_pe, gq, gk, gv, glr, r = seg
    half = MLA_ROPE // 2
    z = lambda n: jnp.zeros((d, n), F32)
    pad = HEAD_PAD - MLA_NOPE - MLA_ROPE
    kpe_a = jnp.concatenate([z(MLA_NOPE), k_pe, z(pad)], axis=1)
    kpe_b = jnp.concatenate([z(MLA_NOPE), -k_pe[:, half:], k_pe[:, :half], z(pad)], axis=1)
    glr_p = jnp.concatenate([glr, z(HEAD_PAD - GLA_GATE_RANK)], axis=1)
    w_in_p = jnp.concatenate([c_q, c_kv, kpe_a, kpe_b, gq, gk, gv, glr_p, r], axis=1).astype(BF16)

    nh = MLA_HEADS
    wq3 = w_uq.reshape(MLA_Q_RANK, nh, MLA_NOPE + MLA_ROPE)
    nope, rope = wq3[..., :MLA_NOPE], wq3[..., MLA_NOPE:]
    zq = lambda n: jnp.zeros((MLA_Q_RANK, nh, n), F32)
    wa = jnp.concatenate([nope, rope, zq(pad)], axis=-1).reshape(MLA_Q_RANK, nh * HEAD_PAD)
    wb = jnp.concatenate([zq(MLA_NOPE), -rope[..., half:], rope[..., :half], zq(pad)],
                         axis=-1).reshape(MLA_Q_RANK, nh * HEAD_PAD)
    w_uq_p = jnp.concatenate([wa, wb], axis=1).astype(BF16)

    wkv3 = w_ukv.reshape(MLA_KV_RANK, nh, MLA_NOPE + MLA_V)
    zk = jnp.zeros((MLA_KV_RANK, nh, HEAD_PAD - MLA_NOPE), F32)
    w_uk_p = jnp.concatenate([wkv3[..., :MLA_NOPE], zk], axis=-1).reshape(MLA_KV_RANK, nh * HEAD_PAD).astype(BF16)
    w_uvt_p = wkv3[..., MLA_NOPE:].reshape(MLA_KV_RANK, nh * MLA_V).T.astype(BF16)

    w_gate_p = jnp.concatenate([w_gate, jnp.zeros((HEAD_PAD - GLA_GATE_RANK, w_gate.shape[1]), F32)],
                               axis=0).astype(BF16)
    return w_in_p, w_uq_p, w_uk_p, w_uvt_p, w_gate_p, b_gate.reshape(1, -1)


def _mla_attn_body(q_ref, k_ref, vt_ref, o_ref, acc_scr, st_scr, p_scr, m_scr, alpha_scr, *, tq, tk, hg):
    i = pl.program_id(1)
    heads = range(hg)

    def scores(t, slot):
        start = pl.multiple_of(t * tk, tk)
        for hd in heads:
            lanes = slice(hd * HEAD_PAD, (hd + 1) * HEAD_PAD)
            st_scr[slot, hd] = _dot_nt(k_ref[pl.ds(start, tk), lanes], q_ref[:, lanes])

    def accumulate(t, slot):
        for hd in heads:
            acc_scr[hd] = alpha_scr[slot, hd] * acc_scr[hd] + _dot(vt_ref[hd, t], p_scr[slot, hd])

    def softmax(slot, mask_shift):
        for hd in heads:
            st = st_scr[slot, hd]
            if mask_shift is not None:
                key = lax.broadcasted_iota(jnp.int32, st.shape, 0) + mask_shift
                qry = lax.broadcasted_iota(jnp.int32, st.shape, 1)
                st = jnp.where(key <= qry, st, NEG_BIG)
            m_prev = m_scr[hd]
            m_next = jnp.maximum(m_prev, jnp.max(st, axis=0, keepdims=True))
            p_scr[slot, hd] = jnp.exp2(st - m_next).astype(BF16)
            alpha_scr[slot, hd] = jnp.exp2(m_prev - m_next)
            m_scr[hd] = m_next

    def step(t, slot, mask_shift, more):
        softmax(slot, mask_shift)
        if more:
            scores(t + 1, 1 - slot)
        accumulate(jnp.maximum(t - 1, 0), 1 - slot)

    acc_scr[...] = jnp.zeros(acc_scr.shape, F32)
    p_scr[1] = jnp.zeros(p_scr.shape[1:], BF16)
    alpha_scr[1] = jnp.ones(alpha_scr.shape[1:], F32)
    m_scr[...] = jnp.full(m_scr.shape, NEG_BIG, F32)
    scores(0, 0)

    def pair(u, c):
        step(2 * u, 0, None, True)
        step(2 * u + 1, 1, None, True)
        return c

    lax.fori_loop(0, i, pair, 0)
    step(2 * i, 0, 0, True)
    step(2 * i + 1, 1, tk, False)
    accumulate(2 * i + 1, 1)
    for hd in heads:
        acc = acc_scr[hd]
        o_ref[hd] = (acc[:MLA_V] * (1.0 / acc[MLA_V:MLA_V + 1])).astype(o_ref.dtype)


def _mla_attn(q, k, vt):
    s = q.shape[0]
    nk, tk = vt.shape[1], vt.shape[3]
    tq = 2 * tk
    hg = ATTN_HEADS_PER_STEP
    out = pl.pallas_call(
        functools.partial(_mla_attn_body, tq=tq, tk=tk, hg=hg),
        grid=(MLA_HEADS // hg, s // tq),
        in_specs=[pl.BlockSpec((tq, hg * HEAD_PAD), lambda g, i: (i, g)),
                  pl.BlockSpec((s, hg * HEAD_PAD), lambda g, i: (0, g)),
                  pl.BlockSpec((hg, nk, ATTN_VROWS, tk), lambda g, i: (g, 0, 0, 0))],
        out_specs=pl.BlockSpec((hg, MLA_V, tq), lambda g, i: (g, 0, i)),
        out_shape=jax.ShapeDtypeStruct((MLA_HEADS, MLA_V, s), BF16),
        scratch_shapes=[pltpu.VMEM((hg, ATTN_VROWS, tq), F32),
                        pltpu.VMEM((2, hg, tk, tq), F32),
                        pltpu.VMEM((2, hg, tk, tq), BF16),
                        pltpu.VMEM((hg, 1, tq), F32),
                        pltpu.VMEM((2, hg, 1, tq), F32)],
        compiler_params=_cparams(("parallel", "arbitrary")),
        name="mla_attn",
    )(q, k, vt)
    return out.reshape(MLA_HEADS * MLA_V, s)


def _gla_levels(chunk):
    n = 0
    while (1 << n) < chunk:
        n += 1
    return n


def _gla_constants(chunk):
    t = np.arange(chunk)
    mats = [t[None, :] <= t[:, None]]
    masks = []
    b = 1
    while b < chunk:
        blk = t // (2 * b)
        upper = (t % (2 * b)) >= b
        e = blk * 2 * b + b - 1
        up_rows = upper[:, None] & (t[None, :] > e[:, None]) & (t[None, :] <= t[:, None])
        lo_rows = (~upper)[:, None] & (t[None, :] > t[:, None]) & (t[None, :] <= e[:, None])
        mats.append(up_rows | lo_rows)
        masks.append(upper[:, None] & (~upper)[None, :] & (blk[:, None] == blk[None, :]))
        b *= 2
    masks.append(t[:, None] == t[None, :])
    mats = np.concatenate(mats, axis=0).astype(np.float32)
    masks = np.stack(masks).astype(np.float32)
    masks = np.tile(masks, (1, GLA_HEADS, 1))
    return jnp.asarray(mats, BF16), jnp.asarray(masks, F32)


def _gla_body(q_ref, k_ref, v_ref, g_ref, mats_ref, masks_ref, o_ref, st_ref, *, chunk, n_sub):
    nlev = _gla_levels(chunk)
    nh, dk, dv = GLA_HEADS, GLA_DK, GLA_DV
    w = nh * dk

    @pl.when(pl.program_id(0) == 0)
    def _():
        st_ref[...] = jnp.zeros(st_ref.shape, F32)

    head_of_lane = lax.broadcasted_iota(jnp.int32, (chunk, w), 1) // dk

    def stack_heads(x):
        return jnp.concatenate([jnp.where(head_of_lane == hd, x, 0.0) for hd in range(nh)], axis=0).astype(BF16)

    chunks = [slice(c * chunk, (c + 1) * chunk) for c in range(n_sub)]
    qs = [q_ref[rows, :] * (dk ** -0.5) for rows in chunks]
    ks = [k_ref[rows, :] for rows in chunks]
    e_alls = [_dot(mats_ref[...], jnp.concatenate(_split2(g_ref[rows, :]), axis=1)) for rows in chunks]
    e_alls = [x[:, :w] + x[:, w:] for x in e_alls]
    attn = [_dot_nt(stack_heads(q), k.astype(BF16)) * masks_ref[nlev] for q, k in zip(qs, ks)]
    for lv in range(nlev):
        decs = [jnp.exp(x[(1 + lv) * chunk:(2 + lv) * chunk]) for x in e_alls]
        attn = [a + _dot_nt(stack_heads(q * d), (k * d).astype(BF16)) * masks_ref[lv]
                for a, q, k, d in zip(attn, qs, ks, decs)]
    attn = [a.astype(BF16) for a in attn]
    for c, rows in enumerate(chunks):
        for hd in range(nh):
            vh = v_ref[rows, hd * dv:(hd + 1) * dv].astype(BF16)
            o_ref[rows, hd * dv:(hd + 1) * dv] = _dot(attn[c][hd * chunk:(hd + 1) * chunk], vh)
    pending = []
    for q, k, x in zip(qs, ks, e_alls):
        b = x[:chunk]
        b_end = b[chunk - 1:chunk]
        pending.append((stack_heads(q * jnp.exp(b)), (k * jnp.exp(b_end - b)).astype(BF16), jnp.exp(b_end)))

    st = st_ref[...]
    head_of_state_lane = lax.broadcasted_iota(jnp.int32, st.shape, 1) // dk
    for c in range(n_sub):
        rows = slice(c * chunk, (c + 1) * chunk)
        qg, kg, dec_end = pending[c]
        inter = _dot_nt(qg, st.astype(BF16))
        new = st * dec_end
        for hd in range(nh):
            cols = slice(hd * dv, (hd + 1) * dv)
            o_ref[rows, cols] = o_ref[rows, cols] + inter[hd * chunk:(hd + 1) * chunk]
            upd = _dot_tn(v_ref[rows, cols].astype(BF16), kg)
            new = new + jnp.where(head_of_state_lane == hd, upd, 0.0)
        st = new
    st_ref[...] = st


def _gla(gq, gk, gv, log_a):
    s = gq.shape[0]
    chunk = min(SCAN_CHUNK, s)
    n_sub = min(GLA_CHUNKS_PER_STEP, s // chunk)
    rows = chunk * n_sub
    mats, masks = _gla_constants(chunk)
    gdk = GLA_HEADS * GLA_DK
    gdv = GLA_HEADS * GLA_DV
    return pl.pallas_call(
        functools.partial(_gla_body, chunk=chunk, n_sub=n_sub),
        grid=(s // rows,),
        in_specs=[_rows(rows, gdk), _rows(rows, gdk), _rows(rows, gdv), _rows(rows, gdk),
                  _resident(mats.shape), _resident(masks.shape)],
        out_specs=_rows(rows, gdv),
        out_shape=jax.ShapeDtypeStruct((s, gdv), F32),
        scratch_shapes=[pltpu.VMEM((GLA_DV, gdk), F32)],
        compiler_params=_cparams(("arbitrary",)),
        name="gla_scan",
    )(gq, gk, gv, log_a, mats, masks)


def _xattn_memory(mem_ref, gmem_ref, wk_ref, wv_ref, k_scr, v_scr):
    @pl.when(pl.program_id(0) == 0)
    def _():
        mem_n = _rms(mem_ref[...], gmem_ref[...]).astype(BF16)
        k_scr[...] = _dot(mem_n, wk_ref[...].astype(BF16)).astype(BF16)
        v_scr[...] = _dot(mem_n, wv_ref[...].astype(BF16)).astype(BF16)


def _xattn_rows(h, gpre_ref, wq_ref, wo_ref, gpost_ref, k_scr, v_scr):
    hn = _rms(h, gpre_ref[...]).astype(BF16)
    q = _dot(hn, wq_ref[...].astype(BF16)).astype(BF16)
    dh = XATTN_DH
    scale = dh ** -0.5
    heads = [slice(hd * dh, (hd + 1) * dh) for hd in range(XATTN_HEADS)]
    scores = [_dot_nt(q[:, sl], k_scr[:, sl]) * scale for sl in heads]
    probs = [jnp.exp(s - jnp.max(s, axis=-1, keepdims=True)) for s in scores]
    inv_l = [1.0 / jnp.sum(p, axis=-1, keepdims=True) for p in probs]
    outs = [(_dot(p.astype(BF16), v_scr[:, sl]) * il).astype(BF16) for p, sl, il in zip(probs, heads, inv_l)]
    o = jnp.concatenate(outs, axis=1)
    return h + _rms(_dot(o, wo_ref[...].astype(BF16)), gpost_ref[...])


def _layer_resident(shape, layer):
    rest = tuple(shape[1:])
    return pl.BlockSpec((None,) + rest, lambda *_: (layer,) + (0,) * len(rest), pipeline_mode=pl.Buffered(1))


def _subtiles(rows):
    sub = min(OUT_SUBTILE, rows)
    return [slice(i * sub, (i + 1) * sub) for i in range(rows // sub)]


def _xattn_operands(xattn_args, layer):
    mem, g_mem, g_pre, w_q, w_k, w_v, w_o, g_post = xattn_args
    ins = (mem, g_mem, g_pre, w_q, w_k, w_v, w_o, g_post)
    specs = ([_resident(a.shape) for a in (mem, g_mem, g_pre)]
             + [_layer_resident(w.shape, layer) for w in (w_q, w_k, w_v, w_o)] + [_resident(g_post.shape)])
    scratch = [pltpu.VMEM((mem.shape[0], D_MODEL), BF16), pltpu.VMEM((mem.shape[0], D_MODEL), BF16)]
    return ins, specs, scratch


def _even_out_body(h_ref, at_ref, o_ref, r_ref, ggla_ref, wout_ref, gpost_ref,
                   mem_ref, gmem_ref, xpre_ref, wq_ref, wk_ref, wv_ref, wo_ref, xpost_ref, out_ref, k_scr, v_scr):
    _xattn_memory(mem_ref, gmem_ref, wk_ref, wv_ref, k_scr, v_scr)
    subs = _subtiles(h_ref.shape[0])
    n_a = at_ref.shape[0]
    hs = []
    for rows in subs:
        og = o_ref[rows, :]
        gn = jnp.concatenate([_rms(og[:, hd * GLA_DV:(hd + 1) * GLA_DV], ggla_ref[...])
                              for hd in range(GLA_HEADS)], axis=1)
        g = (gn * _silu(r_ref[rows, :])).astype(BF16)
        mix = (_dot_tn(at_ref[:, rows], wout_ref[:n_a, :].astype(BF16))
               + _dot(g, wout_ref[n_a:, :].astype(BF16)))
        hs.append(h_ref[rows, :] + _rms(mix, gpost_ref[...]))
    for rows, h in zip(subs, hs):
        out_ref[rows, :] = _xattn_rows(h, xpre_ref, wq_ref, wo_ref, xpost_ref, k_scr, v_scr)


def _even_out_xattn(h, a_t, o_gla, r, g_gla, w_out, g_post, xattn_args, layer):
    s = h.shape[0]
    tm = min(OUT_TILE, s)
    x_ins, x_specs, x_scratch = _xattn_operands(xattn_args, layer)
    return pl.pallas_call(
        _even_out_body,
        grid=(s // tm,),
        in_specs=[_rows(tm, D_MODEL), pl.BlockSpec((a_t.shape[0], tm), lambda i: (0, i)),
                  _rows(tm, o_gla.shape[1]), _rows(tm, r.shape[1]),
                  _resident(g_gla.shape), _layer_resident(w_out.shape, layer // 2), _resident(g_post.shape)] + x_specs,
        out_specs=_rows(tm, D_MODEL),
        out_shape=jax.ShapeDtypeStruct((s, D_MODEL), F32),
        scratch_shapes=x_scratch,
        compiler_params=_cparams(("arbitrary",)),
        name="even_out_xattn",
    )(h, a_t, o_gla, r, g_gla, w_out, g_post, *x_ins)


CONV_HALO = 8


def _odd_proj_body(h_ref, gpre_ref, win_ref, cw_ref, cb_ref, wq_ref, wk_ref, wkt_ref, wv_ref, wgq_ref, wgk_ref,
                   wgv_ref, bg_ref, q_out, k_out, kt_out, v_out, gates_out, xc_out, z_out, xbuf):
    tm = h_ref.shape[0]
    width = D_MODEL

    @pl.when(pl.program_id(0) == 0)
    def _():
        xbuf[0:CONV_HALO, :] = jnp.zeros((CONV_HALO, width), F32)

    hn = _rms(h_ref[...], gpre_ref[...]).astype(BF16)
    xz = _dot(hn, win_ref[...].astype(BF16))
    x_m = xz[:, :width]
    z_out[...] = xz[:, width:]
    xbuf[CONV_HALO:CONV_HALO + tm, :] = x_m
    conv = cb_ref[...] + cw_ref[MLSTM_CONV - 1:MLSTM_CONV, :] * x_m
    for j in range(MLSTM_CONV - 1):
        back = MLSTM_CONV - 1 - j
        conv = conv + cw_ref[j:j + 1, :] * xbuf[CONV_HALO - back:CONV_HALO - back + tm, :]
    xbuf[0:CONV_HALO, :] = x_m[tm - CONV_HALO:, :]
    x_c = _silu(conv)
    xc_out[...] = x_c
    xcb = x_c.astype(BF16)

    dh = MLSTM_DH
    xmb = x_m.astype(BF16)
    qs, ks, vs = [], [], []
    for hd in range(MLSTM_HEADS):
        sl = slice(hd * dh, (hd + 1) * dh)
        qs.append(_dot(xcb[:, sl], wq_ref[hd]))
        ks.append(_dot(xcb[:, sl], wk_ref[hd]))
        vs.append(_dot(xmb[:, sl], wv_ref[hd]))
        kt_out[sl, :] = _dot_nt(wkt_ref[hd], xcb[:, sl]).astype(BF16)
    q = jnp.concatenate(qs, axis=1).astype(BF16)
    k = jnp.concatenate(ks, axis=1).astype(BF16)
    v = jnp.concatenate(vs, axis=1).astype(BF16)
    gates_out[...] = _dot(q, wgq_ref[...]) + _dot(k, wgk_ref[...]) + _dot(v, wgv_ref[...]) + bg_ref[...]
    q_out[...] = (q.astype(F32) * (dh ** -0.5)).astype(BF16)
    k_out[...] = k
    v_out[...] = v


def _odd_proj(h, g_pre, w_in, conv_w, conv_b, w_q, w_k, w_kt, w_v, wg_q, wg_k, wg_v, b_g, j):
    s = h.shape[0]
    tm = min(ROW_TILE, s)
    rows = [(D_MODEL, BF16), (D_MODEL, BF16), None, (D_MODEL, BF16), (V7X_LANES, F32), (D_MODEL, F32),
            (D_MODEL, F32)]
    ins = (g_pre, w_in, conv_w, conv_b, w_q, w_k, w_kt, w_v, wg_q, wg_k, wg_v, b_g)
    kt_spec = pl.BlockSpec((D_MODEL, tm), lambda i: (0, i))
    kt_shape = jax.ShapeDtypeStruct((D_MODEL, s), BF16)
    return pl.pallas_call(
        _odd_proj_body,
        grid=(s // tm,),
        in_specs=([_rows(tm, D_MODEL), _resident(g_pre.shape), _layer_resident(w_in.shape, j)]
                  + [_resident(a.shape) for a in ins[2:]]),
        out_specs=[kt_spec if r is None else _rows(tm, r[0]) for r in rows],
        out_shape=[kt_shape if r is None else jax.ShapeDtypeStruct((s, r[0]), r[1]) for r in rows],
        scratch_shapes=[pltpu.VMEM((CONV_HALO + tm, D_MODEL), F32)],
        compiler_params=_cparams(("arbitrary",)),
        name="odd_proj",
    )(h, *ins)


def _prep_odd_gate_weights(w_gates, b_gates):
    nh, dh = MLSTM_HEADS, MLSTM_DH
    w4 = w_gates.reshape(nh, 3, dh, 2 * nh)
    pad = jnp.zeros((nh * dh, V7X_LANES - 2 * nh), F32)
    parts = [jnp.concatenate([w4[:, j].reshape(nh * dh, 2 * nh), pad], axis=1).astype(BF16) for j in range(3)]
    b = jnp.concatenate([b_gates, jnp.zeros((V7X_LANES - 2 * nh,), F32)]).reshape(1, V7X_LANES)
    return parts[0], parts[1], parts[2], b


MLSTM_EXT = MLSTM_DH + V7X_LANES


def _mlstm_body(q_ref, k_ref, kt_ref, v_ref, gcol_ref, grow_ref, tri_ref, trit_ref, o_ref, c_scr, m_scr, *,
                chunk, n_sub):
    nh, dh = MLSTM_HEADS, MLSTM_DH
    lanes = V7X_LANES

    @pl.when(pl.program_id(0) == 0)
    def _():
        c_scr[...] = jnp.zeros(c_scr.shape, F32)
        m_scr[...] = jnp.zeros(m_scr.shape, F32)

    row = lax.broadcasted_iota(jnp.int32, (chunk, chunk), 0)
    col = lax.broadcasted_iota(jnp.int32, (chunk, chunk), 1)
    causal = col <= row
    ones_col = jnp.where(lax.broadcasted_iota(jnp.int32, (chunk, lanes), 1) == 0, 1.0, 0.0).astype(BF16)

    chunks = [slice(c * chunk, (c + 1) * chunk) for c in range(n_sub)]
    heads = [slice(hd * dh, (hd + 1) * dh) for hd in range(nh)]
    pairs = [(c, hd) for c in range(n_sub) for hd in range(nh)]
    b_cols = [_dot(tri_ref[...], jnp.concatenate(_split3(_log_sigmoid(gcol_ref[rows, :])), axis=1))
              for rows in chunks]
    b_cols = [x[:, :lanes] + x[:, lanes:2 * lanes] + x[:, 2 * lanes:] for x in b_cols]
    g_rows = [grow_ref[c] for c in range(n_sub)]
    nr = g_rows[0].shape[0]
    b_rows = [_dot(jnp.concatenate(_split3(_log_sigmoid(gr)), axis=0), trit_ref[...]) for gr in g_rows]
    b_rows = [x[:nr] + x[nr:2 * nr] + x[2 * nr:] for x in b_rows]
    r_h = {(c, hd): g_rows[c][hd:hd + 1, :] - b_rows[c][nh + hd:nh + hd + 1, :] for c, hd in pairs}
    b_h = {(c, hd): b_cols[c][:, nh + hd:nh + hd + 1] for c, hd in pairs}
    logw = {p: jnp.where(causal, r_h[p], NEG_BIG) for p in pairs}
    cm = {p: jnp.max(logw[p], axis=-1, keepdims=True) for p in pairs}
    qk = {(c, hd): _dot_nt(q_ref[chunks[c], heads[hd]], k_ref[chunks[c], heads[hd]]) for c, hd in pairs}
    w_intra = {p: (jnp.exp(logw[p] - cm[p]) * qk[p]).astype(BF16) for p in pairs}
    vext = {(c, hd): jnp.concatenate([v_ref[chunks[c], heads[hd]], ones_col], axis=1) for c, hd in pairs}
    intra = {p: _dot(w_intra[p], vext[p]) for p in pairs}

    for c, hd in pairs:
        p = (c, hd)
        rows, sl = chunks[c], heads[hd]
        m_old = m_scr[hd:hd + 1, 0:1]
        cext = c_scr[hd]
        big_m = jnp.maximum(cm[p], m_old)
        w_inter = jnp.exp(m_old - big_m)
        out = (jnp.exp(cm[p] - big_m) * intra[p]
               + _dot((q_ref[rows, sl].astype(F32) * w_inter).astype(BF16), cext.astype(BF16)))
        den = out[:, dh:dh + 1]
        o_ref[rows, sl] = out[:, :dh] / jnp.maximum(jnp.abs(den), jnp.exp(-(b_h[p] + big_m)))

        m_last = big_m[chunk - 1:chunk, :]
        w_s = jnp.exp(r_h[p] - m_last)
        ktw = (kt_ref[sl, rows].astype(F32) * w_s).astype(BF16)
        c_scr[hd] = jnp.exp(m_old - m_last) * cext + _dot(ktw, vext[p])
        m_scr[hd:hd + 1, :] = jnp.broadcast_to(b_h[p][chunk - 1:chunk, :] + m_last, (1, m_scr.shape[1]))


def _mlstm(q, k, kt, v, gates):
    s = q.shape[0]
    chunk = min(SCAN_CHUNK, s)
    nc = s // chunk
    n_sub = min(MLSTM_CHUNKS_PER_STEP, nc)
    rows = chunk * n_sub
    lanes = V7X_LANES
    ng = 2 * MLSTM_HEADS
    grow = gates[:, :ng].reshape(nc, chunk, ng).transpose(0, 2, 1)
    t = np.arange(chunk)
    tri = (t[None, :] <= t[:, None]).astype(np.float32)
    return pl.pallas_call(
        functools.partial(_mlstm_body, chunk=chunk, n_sub=n_sub),
        grid=(s // rows,),
        in_specs=[_rows(rows, D_MODEL), _rows(rows, D_MODEL), pl.BlockSpec((D_MODEL, rows), lambda c: (0, c)),
                  _rows(rows, D_MODEL), _rows(rows, lanes),
                  pl.BlockSpec((n_sub, ng, chunk), lambda c: (c, 0, 0)),
                  _resident((chunk, chunk)), _resident((chunk, chunk))],
        out_specs=_rows(rows, D_MODEL),
        out_shape=jax.ShapeDtypeStruct((s, D_MODEL), F32),
        scratch_shapes=[pltpu.VMEM((MLSTM_HEADS, MLSTM_DH, MLSTM_EXT), F32),
                        pltpu.VMEM((8, lanes), F32)],
        compiler_params=_cparams(("arbitrary",)),
        name="mlstm_scan",
    )(q, k, kt, v, gates, grow, jnp.asarray(tri, BF16), jnp.asarray(tri.T, BF16))


def _odd_out_body(h_ref, hc_ref, xc_ref, z_ref, gh_ref, skip_ref, wo_ref, gpost_ref,
                  mem_ref, gmem_ref, xpre_ref, wq_ref, wk_ref, wv_ref, xwo_ref, xpost_ref, out_ref, k_scr, v_scr):
    _xattn_memory(mem_ref, gmem_ref, wk_ref, wv_ref, k_scr, v_scr)
    dh = MLSTM_DH
    subs = _subtiles(h_ref.shape[0])
    hs = []
    for rows in subs:
        hc = hc_ref[rows, :]
        parts = []
        for hd in range(MLSTM_HEADS):
            seg = hc[:, hd * dh:(hd + 1) * dh]
            cen = seg - jnp.mean(seg, axis=-1, keepdims=True)
            parts.append(cen * lax.rsqrt(jnp.mean(cen * cen, axis=-1, keepdims=True) + EPS))
        hn = jnp.concatenate(parts, axis=1) * gh_ref[...]
        out = ((hn + skip_ref[...] * xc_ref[rows, :]) * _silu(z_ref[rows, :])).astype(BF16)
        hs.append(h_ref[rows, :] + _rms(_dot(out, wo_ref[...].astype(BF16)), gpost_ref[...]))
    for rows, h in zip(subs, hs):
        out_ref[rows, :] = _xattn_rows(h, xpre_ref, wq_ref, xwo_ref, xpost_ref, k_scr, v_scr)


def _odd_out_xattn(h, hcell, x_c, z, g_hnorm, skip, w_out, g_post, xattn_args, layer):
    s = h.shape[0]
    tm = min(OUT_TILE, s)
    x_ins, x_specs, x_scratch = _xattn_operands(xattn_args, layer)
    return pl.pallas_call(
        _odd_out_body,
        grid=(s // tm,),
        in_specs=[_rows(tm, D_MODEL)] * 4 + [_resident(g_hnorm.shape), _resident(skip.shape),
                                              _layer_resident(w_out.shape, layer // 2),
                                              _resident(g_post.shape)] + x_specs,
        out_specs=_rows(tm, D_MODEL),
        out_shape=jax.ShapeDtypeStruct((s, D_MODEL), F32),
        scratch_shapes=x_scratch,
        compiler_params=_cparams(("arbitrary",)),
        name="odd_out_xattn",
    )(h, hcell, x_c, z, g_hnorm, skip, w_out, g_post, *x_ins)


def _ffn_body(h_ref, gpre_ref, w1_ref, w2_ref, gpost_ref, out_ref):
    h = h_ref[...]
    xn = _rms(h, gpre_ref[...]).astype(BF16)
    acc = None
    for c in range(D_FF // FF_CHUNK):
        sl = slice(c * FF_CHUNK, (c + 1) * FF_CHUNK)
        a = jnp.square(jnp.maximum(_dot(xn, w1_ref[:, sl].astype(BF16)), 0.0)).astype(BF16)
        part = _dot(a, w2_ref[sl, :].astype(BF16))
        acc = part if acc is None else acc + part
    out_ref[...] = h + _rms(acc, gpost_ref[...])


def _ffn(h, g_pre, w1, w2, g_post, layer):
    s = h.shape[0]
    tm = min(ROW_TILE, s)
    return pl.pallas_call(
        _ffn_body,
        grid=(s // tm,),
        in_specs=[_rows(tm, D_MODEL), _resident(g_pre.shape), _layer_resident(w1.shape, layer),
                  _layer_resident(w2.shape, layer), _resident(g_post.shape)],
        out_specs=_rows(tm, D_MODEL),
        out_shape=jax.ShapeDtypeStruct((s, D_MODEL), F32),
        compiler_params=_cparams(("parallel",)),
        name="sq_relu_mlp",
    )(h, g_pre, w1, w2, g_post)


def _row(g):
    return g.reshape(1, -1)


def kernel(x, mem, positions, g_mix_pre, g_mix_post, g_xattn_pre, g_xattn_post, g_mem, g_ffn_pre, g_ffn_post, ev_w_in, ev_g_q, ev_w_uq, ev_g_kv, ev_w_ukv, ev_w_gate, ev_b_gate, ev_g_gla, ev_w_out, od_w_in, od_conv_w, od_conv_b, od_w_q, od_w_k, od_w_v, od_w_gates, od_b_gates, od_g_hnorm, od_skip, od_w_out, xa_w_q, xa_w_k, xa_w_v, xa_w_o, ffn_w1, ffn_w2):
    batch, seq, d = x.shape
    assert batch == 1 and d == D_MODEL and seq % ROW_TILE == 0 and seq % (2 * ATTN_TK) == 0
    assert seq % (SCAN_CHUNK * GLA_CHUNKS_PER_STEP) == 0 and seq % (SCAN_CHUNK * MLSTM_CHUNKS_PER_STEP) == 0
    assert ROW_TILE == ATTN_TK and seq % OUT_TILE == 0
    h = x.reshape(seq, d)
    mem2 = mem.reshape(mem.shape[1], d)
    cos_t, sin_t = _rope_tables(positions)

    for layer in range(DEPTH):
        j = layer // 2
        xattn_args = (mem2, _row(g_mem[layer]), _row(g_xattn_pre[layer]), xa_w_q, xa_w_k, xa_w_v, xa_w_o,
                      _row(g_xattn_post[layer]))
        if layer % 2 == 0:
            w_in_p, w_uq_p, w_uk_p, w_uvt_p, w_gate_p, b_gate = _prep_even_weights(
                ev_w_in[j], ev_w_uq[j], ev_w_ukv[j], ev_w_gate[j], ev_b_gate[j])
            q, k, vt, gq, gk, gv, log_a, r = _even_proj(
                h, _row(g_mix_pre[layer]), w_in_p, _row(ev_g_q[j]), w_uq_p, _row(ev_g_kv[j]), w_uk_p, w_uvt_p,
                w_gate_p, b_gate, cos_t, sin_t)
            a_t = _mla_attn(q, k, vt)
            o_gla = _gla(gq, gk, gv, log_a)
            h = _even_out_xattn(h, a_t, o_gla, r, _row(ev_g_gla[j]), ev_w_out, _row(g_mix_post[layer]), xattn_args,
                                layer)
        else:
            wg_q, wg_k, wg_v, b_g = _prep_odd_gate_weights(od_w_gates[j], od_b_gates[j])
            w_k = od_w_k[j].astype(BF16)
            q, k, kt, v, gates, x_c, z = _odd_proj(
                h, _row(g_mix_pre[layer]), od_w_in, od_conv_w[j], _row(od_conv_b[j]), od_w_q[j].astype(BF16), w_k,
                w_k.transpose(0, 2, 1), od_w_v[j].astype(BF16), wg_q, wg_k, wg_v, b_g, j)
            hcell = _mlstm(q, k, kt, v, gates)
            h = _odd_out_xattn(h, hcell, x_c, z, _row(od_g_hnorm[j]), _row(od_skip[j]), od_w_out,
                               _row(g_mix_post[layer]), xattn_args, layer)
        h = _ffn(h, _row(g_ffn_pre[layer]), ffn_w1, ffn_w2, _row(g_ffn_post[layer]), layer)
    return h.reshape(batch, seq, d)
```
